```python
import jax, jax.numpy as jnp
from jax import lax
import numpy as np

D_MODEL = 1024
BATCH = 4
SEQ = 4096
DEPTH = 2
DEC_BATCH = 32
DEC_SEQ = 4
PAST_LEN = 8192
PAGE_SIZE = 128

N_AB_LAYERS = (DEPTH + 1) // 2
N_C_LAYERS = DEPTH // 2
N_HEADS_A = 8
HEAD_DIM_A = 64
ATTN_W = N_HEADS_A * HEAD_DIM_A
MOBA_BLOCK = 256
MOBA_TOPK = 3
Q_CHUNK = 32
CONV_CH = D_MODEL // 2
CONV_W = 3
IN_AB = 3 * ATTN_W + 3 * CONV_CH
CHUNK_C = 128
N_GROUPS_C = 8
C_WIDTH = D_MODEL
C_GROUP = C_WIDTH // N_GROUPS_C
D_FF = 4 * D_MODEL
EPS = 1e-6

kernel_name = "moba_shortconv_gmlp_hybrid_step"


def rms_norm(x, g):
    xf = x.astype(jnp.float32)
    xf = xf * lax.rsqrt(jnp.mean(xf * xf, axis=-1, keepdims=True) + EPS)
    return (xf * g.astype(jnp.float32)).astype(x.dtype)


def ada_terms(c, w, b):
    return jnp.split(jax.nn.silu(c) @ w + b, 6, axis=-1)


def modulate(h, shift, scale):
    return h * (1.0 + scale[:, None, :]) + shift[:, None, :]


def alibi_slopes():
    return jnp.exp2(-8.0 * jnp.arange(1, N_HEADS_A + 1, dtype=jnp.float32) / N_HEADS_A)


def moba_core(q, q_pos, sel, own_k, own_v, own_pos, own_ok, slopes):
    scale = HEAD_DIM_A ** -0.5
    sl = slopes[None, :, None, None]
    s_own = jnp.einsum('bhqd,bhmd->bhqm', q, own_k).astype(jnp.float32) * scale
    s_own = s_own - sl * (q_pos[:, None] - own_pos[None, :]).astype(jnp.float32)
    s_own = jnp.where(own_ok, s_own, -jnp.inf)
    if sel is None:
        p = jax.nn.softmax(s_own, axis=-1).astype(q.dtype)
        return jnp.einsum('bhqm,bhmd->bhqd', p, own_v)
    sel_k, sel_v, sel_pos, sel_ok = sel
    B, H, Q = s_own.shape[:3]
    s_sel = jnp.einsum('bhqd,bhqjld->bhqjl', q, sel_k).astype(jnp.float32) * scale
    s_sel = s_sel - sl[..., None] * (q_pos[:, None, None] - sel_pos).astype(jnp.float32)
    s_sel = jnp.where(sel_ok[..., None], s_sel, -jnp.inf)
    n_sel = s_sel.shape[3] * s_sel.shape[4]
    p = jax.nn.softmax(jnp.concatenate([s_sel.reshape(B, H, Q, n_sel), s_own], axis=-1), axis=-1)
    p_sel = p[..., :n_sel].reshape(s_sel.shape).astype(q.dtype)
    p_own = p[..., n_sel:].astype(q.dtype)
    return (jnp.einsum('bhqjl,bhqjld->bhqd', p_sel, sel_v)
            + jnp.einsum('bhqm,bhmd->bhqd', p_own, own_v))


def moba_prompt(q, k, v, slopes):
    B, S, H, Dh = q.shape
    nb = -(-S // MOBA_BLOCK)
    pad = nb * MOBA_BLOCK - S

    def to_blocks(t):
        t = jnp.pad(t, ((0, 0), (0, pad), (0, 0), (0, 0)))
        return t.reshape(B, nb, MOBA_BLOCK, H, Dh).transpose(0, 3, 1, 2, 4)

    kb, vb = to_blocks(k), to_blocks(v)
    k_mean = jnp.mean(kb.astype(jnp.float32), axis=3)
    qh = q.transpose(0, 2, 1, 3)
    n_sel = min(MOBA_TOPK, nb - 1)
    bi = jnp.arange(B)[:, None, None, None]
    hi = jnp.arange(H)[None, :, None, None]
    blk_ids = jnp.arange(nb)
    offs = jnp.arange(MOBA_BLOCK)

    def one_chunk(ci):
        start = ci * Q_CHUNK
        qc = lax.dynamic_slice_in_dim(qh, start, Q_CHUNK, axis=2)
        q_pos = start + jnp.arange(Q_CHUNK)
        ob = start // MOBA_BLOCK
        own_k = lax.dynamic_index_in_dim(kb, ob, axis=2, keepdims=False)
        own_v = lax.dynamic_index_in_dim(vb, ob, axis=2, keepdims=False)
        own_pos = ob * MOBA_BLOCK + offs
        own_ok = own_pos[None, :] <= q_pos[:, None]
        sel = None
        if n_sel > 0:
            cur = q_pos // MOBA_BLOCK
            gate = jnp.einsum('bhqd,bhnd->bhqn', qc.astype(jnp.float32), k_mean)
            gate = jnp.where(blk_ids[None, :] < cur[:, None], gate, -jnp.inf)
            _, idx = lax.top_k(gate, n_sel)
            sel = (kb[bi, hi, idx], vb[bi, hi, idx],
                   idx[..., None] * MOBA_BLOCK + offs,
                   jnp.arange(n_sel)[None, :] < cur[:, None])
        return moba_core(qc, q_pos, sel, own_k, own_v, own_pos, own_ok, slopes)

    out = lax.map(one_chunk, jnp.arange(S // Q_CHUNK))
    return out.transpose(1, 0, 3, 2, 4).reshape(B, S, H * Dh)


def moba_sample(q, k, v, cache_k, cache_v, page_table, slopes):
    Bd, T, H, Dh = q.shape
    n_pages = page_table.shape[1]
    past = n_pages * PAGE_SIZE
    nbp = past // MOBA_BLOCK
    own_start = nbp * MOBA_BLOCK
    own_page0 = own_start // PAGE_SIZE
    n_own_past = past - own_start
    qh = q.transpose(0, 2, 1, 3)
    q_pos = past + jnp.arange(T)
    own_pages = page_table[:, own_page0:]
    own_k = jnp.concatenate([cache_k[own_pages].reshape(Bd, n_own_past, H, Dh), k], axis=1).transpose(0, 2, 1, 3)
    own_v = jnp.concatenate([cache_v[own_pages].reshape(Bd, n_own_past, H, Dh), v], axis=1).transpose(0, 2, 1, 3)
    own_pos = own_start + jnp.arange(n_own_past + T)
    own_ok = own_pos[None, :] <= q_pos[:, None]
    sel = None
    n_sel = min(MOBA_TOPK, nbp)
    if n_sel > 0:
        ppb = MOBA_BLOCK // PAGE_SIZE
        k_rows = cache_k[page_table[:, :own_page0]].astype(jnp.float32)
        k_mean = jnp.mean(k_rows.reshape(Bd, nbp, MOBA_BLOCK, H, Dh), axis=2).transpose(0, 2, 1, 3)
        gate = jnp.einsum('bhqd,bhnd->bhqn', qh.astype(jnp.float32), k_mean)
        _, idx = lax.top_k(gate, n_sel)
        lp = idx[..., None] * ppb + jnp.arange(ppb)
        phys = page_table[jnp.arange(Bd)[:, None, None, None, None], lp]
        hi = jnp.arange(H)[None, :, None, None, None]
        shp = (Bd, H, T, n_sel, MOBA_BLOCK, Dh)
        sel = (cache_k[phys, :, hi, :].reshape(shp), cache_v[phys, :, hi, :].reshape(shp),
               idx[..., None] * MOBA_BLOCK + jnp.arange(MOBA_BLOCK),
               jnp.ones((T, n_sel), bool))
    out = moba_core(qh, q_pos, sel, own_k, own_v, own_pos, own_ok, slopes)
    return out.transpose(0, 2, 1, 3).reshape(Bd, T, H * Dh)


def short_conv(u, prev, w):
    T = u.shape[1]
    full = jnp.concatenate([prev, u], axis=1)
    y = full[:, 0:T] * w[0]
    for i in range(1, CONV_W):
        y = y + full[:, i:i + T] * w[i]
    return y, full[:, -(CONV_W - 1):]


def chunk_gmlp(h, w_in, g_v, w_s, b_s, w_out):
    B, T, _ = h.shape
    u, v = jnp.split(h @ w_in, 2, axis=-1)
    v = rms_norm(v, g_v)
    L = min(T, CHUNK_C)
    ws = jnp.where(jnp.tril(jnp.ones((CHUNK_C, CHUNK_C), bool)), w_s, 0.0).astype(h.dtype)[:, :L, :L]
    vr = v.reshape(B, T // L, L, N_GROUPS_C, C_GROUP)
    s = jnp.einsum('gts,bnsgd->bntgd', ws, vr) + jnp.swapaxes(b_s[:, :L], 0, 1)[:, :, None]
    return (u * s.reshape(B, T, C_WIDTH)) @ w_out, v


def trunk(x, c, paged, norm_g, w_ada, b_ada, w_in_ab, conv_w, w_out_ab, w_in_c, g_v, w_s, b_s,
          w_out_c, w_ff1, w_ff2, g_final):
    B, T, _ = x.shape
    slopes = alibi_slopes()
    splits = [ATTN_W, 2 * ATTN_W, 3 * ATTN_W, 3 * ATTN_W + CONV_CH, 3 * ATTN_W + 2 * CONV_CH]
    ks_, vs_, convs, cvs = [], [], [], []
    for layer in range(DEPTH):
        sh1, sc1, g1, sh2, sc2, g2 = ada_terms(c, w_ada[layer], b_ada[layer])
        h = modulate(rms_norm(x, norm_g[layer, 0]), sh1, sc1)
        if layer % 2 == 0:
            i = layer // 2
            q, k, v, bg, cg, xin = jnp.split(h @ w_in_ab[i], splits, axis=-1)
            q = q.reshape(B, T, N_HEADS_A, HEAD_DIM_A)
            k = k.reshape(B, T, N_HEADS_A, HEAD_DIM_A)
            v = v.reshape(B, T, N_HEADS_A, HEAD_DIM_A)
            if paged is None:
                attn = moba_prompt(q, k, v, slopes)
                prev = jnp.zeros((B, CONV_W - 1, CONV_CH), x.dtype)
                ks_.append(k.reshape(B * T // PAGE_SIZE, PAGE_SIZE, N_HEADS_A, HEAD_DIM_A))
                vs_.append(v.reshape(B * T // PAGE_SIZE, PAGE_SIZE, N_HEADS_A, HEAD_DIM_A))
            else:
                cache_k, cache_v, state_conv, page_table = paged
                attn = moba_sample(q, k, v, cache_k[i], cache_v[i], page_table, slopes)
                prev = state_conv[i]
                ks_.append(k)
                vs_.append(v)
            yc, conv_state = short_conv(cg * xin, prev, conv_w[i])
            convs.append(conv_state)
            mix = jnp.concatenate([attn, bg * yc], axis=-1) @ w_out_ab[i]
        else:
            j = layer // 2
            mix, v_rows = chunk_gmlp(h, w_in_c[j], g_v[j], w_s[j], b_s[j], w_out_c[j])
            cvs.append(v_rows)
        x = x + g1[:, None, :] * mix
        h = modulate(rms_norm(x, norm_g[layer, 1]), sh2, sc2)
        x = x + g2[:, None, :] * (jnp.square(jax.nn.relu(h @ w_ff1[layer])) @ w_ff2[layer])
    return rms_norm(x, g_final), ks_, vs_, convs, cvs


def setup_inputs(seed: int = 0) -> dict:
    key = jax.random.key(seed)
    ks = jax.random.split(key, 24)
    n_pages = PAST_LEN // PAGE_SIZE
    n_used = DEC_BATCH * n_pages
    n_pool = n_used + n_used // 4
    page_table = jax.random.permutation(ks[0], n_pool)[:n_used].reshape(DEC_BATCH, n_pages).astype(jnp.int32)

    def nrm(k, shape, scale=1.0):
        return jax.random.normal(k, shape, jnp.float32) * scale

    return {
        "x_prompt": nrm(ks[1], (BATCH, SEQ, D_MODEL)),
        "x_sample": nrm(ks[2], (DEC_BATCH, DEC_SEQ, D_MODEL)),
        "cache_k": nrm(ks[3], (N_AB_LAYERS, n_pool, PAGE_SIZE, N_HEADS_A, HEAD_DIM_A)),
        "cache_v": nrm(ks[4], (N_AB_LAYERS, n_pool, PAGE_SIZE, N_HEADS_A, HEAD_DIM_A)),
        "state_conv": nrm(ks[5], (N_AB_LAYERS, DEC_BATCH, CONV_W - 1, CONV_CH)),
        "page_table": page_table,
        "c_prompt": nrm(ks[6], (BATCH, D_MODEL)),
        "c_sample": nrm(ks[7], (DEC_BATCH, D_MODEL)),
        "norm_g": 1.0 + nrm(ks[8], (DEPTH, 2, D_MODEL), 0.02),
        "w_ada": nrm(ks[9], (DEPTH, D_MODEL, 6 * D_MODEL), D_MODEL ** -0.5),
        "b_ada": nrm(ks[10], (DEPTH, 6 * D_MODEL), 0.02),
        "w_in_ab": nrm(ks[11], (N_AB_LAYERS, D_MODEL, IN_AB), D_MODEL ** -0.5),
        "conv_w": nrm(ks[12], (N_AB_LAYERS, CONV_W, CONV_CH), CONV_W ** -0.5),
        "w_out_ab": nrm(ks[13], (N_AB_LAYERS, ATTN_W + CONV_CH, D_MODEL), (ATTN_W + CONV_CH) ** -0.5),
        "w_in_c": nrm(ks[14], (N_C_LAYERS, D_MODEL, 2 * C_WIDTH), D_MODEL ** -0.5),
        "g_v": 1.0 + nrm(ks[15], (N_C_LAYERS, C_WIDTH), 0.02),
        "w_s": nrm(ks[16], (N_C_LAYERS, N_GROUPS_C, CHUNK_C, CHUNK_C), CHUNK_C ** -0.5),
        "b_s": 1.0 + nrm(ks[17], (N_C_LAYERS, N_GROUPS_C, CHUNK_C), 0.02),
        "w_out_c": nrm(ks[18], (N_C_LAYERS, C_WIDTH, D_MODEL), C_WIDTH ** -0.5),
        "w_ff1": nrm(ks[19], (DEPTH, D_MODEL, D_FF), D_MODEL ** -0.5),
        "w_ff2": nrm(ks[20], (DEPTH, D_FF, D_MODEL), D_FF ** -0.5),
        "g_final": 1.0 + nrm(ks[21], (D_MODEL,), 0.02),
    }


def reference(x_prompt, x_sample, cache_k, cache_v, state_conv, page_table, c_prompt, c_sample,
              norm_g, w_ada, b_ada, w_in_ab, conv_w, w_out_ab, w_in_c, g_v, w_s, b_s, w_out_c,
              w_ff1, w_ff2, g_final):
    y_prompt, kp, vp, convp, _ = trunk(
        x_prompt, c_prompt, None, norm_g, w_ada, b_ada, w_in_ab, conv_w, w_out_ab,
        w_in_c, g_v, w_s, b_s, w_out_c, w_ff1, w_ff2, g_final)
    y_sample, ksm, vsm, convs, cvs = trunk(
        x_sample, c_sample, (cache_k, cache_v, state_conv, page_table), norm_g, w_ada, b_ada,
        w_in_ab, conv_w, w_out_ab, w_in_c, g_v, w_s, b_s, w_out_c, w_ff1, w_ff2, g_final)
    return (y_prompt, y_sample, jnp.stack(kp), jnp.stack(vp), jnp.stack(convp),
            jnp.stack(ksm), jnp.stack(vsm), jnp.stack(convs), jnp.stack(cvs))
```

```python
import functools

import jax
import jax.numpy as jnp
from jax import lax
from jax.experimental import pallas as pl
from jax.experimental.pallas import tpu as pltpu

F32 = jnp.float32
BF16 = jnp.bfloat16

EPS = 1e-6
MOBA_BLOCK = 256
MOBA_BLOCK_SHIFT = 8
MOBA_TOPK = 3
CONV_W = 3
LANES = 128
SUBLANES = 8
NEG_BIG = -(2.0 ** 100)
VMEM_LIMIT_BYTES = 56 * 1024 * 1024

_NT = (((1,), (1,)), ((), ()))


def _params(sem):
    return pltpu.CompilerParams(dimension_semantics=sem, vmem_limit_bytes=VMEM_LIMIT_BYTES)


def _resident(shape):
    n = len(shape)
    return pl.BlockSpec(shape, lambda *_: (0,) * n, pipeline_mode=pl.Buffered(1))


def _rms(x, g):
    return x * lax.rsqrt(jnp.mean(x * x, axis=-1, keepdims=True) + EPS) * g


def _split_bf16(x):
    hi = x.astype(BF16)
    lo = (x - hi.astype(F32)).astype(BF16)
    return hi, lo


def _ada_kernel(c_ref, w_ref, b_ref, o_ref):
    c = c_ref[...]
    s = (c * jax.nn.sigmoid(c)).astype(BF16)
    o_ref[0] = jnp.dot(s, w_ref[0], preferred_element_type=F32) + b_ref[0]


def _ada_terms(c_all, w_ada, b_ada):
    n_layers, d, d6 = w_ada.shape
    bp = c_all.shape[0]
    tn = d6 // 4
    return pl.pallas_call(
        _ada_kernel,
        grid=(n_layers, d6 // tn),
        in_specs=[
            pl.BlockSpec((bp, d), lambda l, j: (0, 0)),
            pl.BlockSpec((1, d, tn), lambda l, j: (l, 0, j)),
            pl.BlockSpec((1, 1, tn), lambda l, j: (l, 0, j)),
        ],
        out_specs=pl.BlockSpec((1, bp, tn), lambda l, j: (l, 0, j)),
        out_shape=jax.ShapeDtypeStruct((n_layers, bp, d6), F32),
        compiler_params=_params(("arbitrary", "arbitrary")),
        name="ada_terms",
    )(c_all, w_ada, b_ada.reshape(n_layers, 1, d6))


def _front_kernel(multi_seq, seq_len, *refs):
    if multi_seq:
        (x_ref, mod_ref, g_ref, w_ref, cw_ref, pe1_ref, pe2_ref,
         q_ref, k_ref, v_ref, gated_ref, u_ref) = refs
    else:
        (x_ref, mod_ref, g_ref, w_ref, cw_ref, prev_ref,
         q_ref, k_ref, v_ref, gated_ref, cs_ref, carry_ref) = refs
    tm = x_ref.shape[0]
    cw = q_ref.shape[1]
    h = _rms(x_ref[...], g_ref[...]) * (1.0 + mod_ref[1]) + mod_ref[0]
    hb = h.astype(BF16)

    def proj(j):
        return jnp.dot(hb, w_ref[:, j * cw:(j + 1) * cw], preferred_element_type=F32)

    q_ref[...] = proj(0)
    k_ref[...] = proj(1)
    v_ref[...] = proj(2)
    u = proj(4) * proj(5)
    rows = lax.broadcasted_iota(jnp.int32, u.shape, 0)
    u1 = pltpu.roll(u, 1, axis=0)
    u2 = pltpu.roll(u, 2, axis=0)
    if multi_seq:
        assert seq_len & (seq_len - 1) == 0
        t = rows & (seq_len - 1)
        u1 = jnp.where(t == 0, pe1_ref[...], u1)
        u2 = jnp.where(t < 2, pe2_ref[...], u2)
        u_ref[...] = u
    else:
        i = pl.program_id(1)

        @pl.when(i == 0)
        def _():
            carry_ref[SUBLANES - 2:SUBLANES, :] = prev_ref[...]

        p0 = carry_ref[SUBLANES - 2:SUBLANES - 1, :]
        p1 = carry_ref[SUBLANES - 1:SUBLANES, :]
        u1 = jnp.where(rows == 0, p1, u1)
        u2 = jnp.where(rows == 0, p0, jnp.where(rows == 1, p1, u2))
        carry_ref[...] = u[tm - SUBLANES:tm, :]

        @pl.when(i == pl.num_programs(1) - 1)
        def _():
            cs_ref[...] = u[tm - 2:tm, :]

    yc = u2 * cw_ref[0:1, :] + u1 * cw_ref[1:2, :] + u * cw_ref[2:3, :]
    gated_ref[...] = (proj(3) * yc).astype(BF16)


def _front_prompt(x, mods, g, w_in, conv_w, prev, layer, tm):
    n, d = x.shape
    nb = prev.shape[0]
    cw = prev.shape[2]
    nt = n // nb // tm
    row = lambda b, i: (b * nt + i, 0)
    outs = pl.pallas_call(
        functools.partial(_front_kernel, False, 0),
        grid=(nb, nt),
        in_specs=[
            pl.BlockSpec((tm, d), row),
            pl.BlockSpec((None, None, 6, 1, d), lambda b, i: (layer, b, 0, 0, 0)),
            _resident(g.shape),
            _resident(w_in.shape),
            _resident(conv_w.shape),
            pl.BlockSpec((None, CONV_W - 1, cw), lambda b, i: (b, 0, 0)),
        ],
        out_specs=[
            pl.BlockSpec((tm, cw), row),
            pl.BlockSpec((tm, cw), row),
            pl.BlockSpec((tm, cw), row),
            pl.BlockSpec((tm, cw), row),
            pl.BlockSpec((None, CONV_W - 1, cw), lambda b, i: (b, 0, 0)),
        ],
        out_shape=[
            jax.ShapeDtypeStruct((n, cw), F32),
            jax.ShapeDtypeStruct((n, cw), F32),
            jax.ShapeDtypeStruct((n, cw), F32),
            jax.ShapeDtypeStruct((n, cw), BF16),
            jax.ShapeDtypeStruct(prev.shape, F32),
        ],
        scratch_shapes=[pltpu.VMEM((SUBLANES, cw), F32)],
        compiler_params=_params(("arbitrary", "arbitrary")),
        name="front_prompt",
    )(x, mods, g, w_in, conv_w, prev)
    return outs


def _front_sample(x, mods_rows, g, w_in, conv_w, pe1, pe2, layer, seq_len):
    n, d = x.shape
    cw = pe1.shape[1]
    full = lambda i: (0, 0)
    return pl.pallas_call(
        functools.partial(_front_kernel, True, seq_len),
        grid=(1,),
        in_specs=[
            pl.BlockSpec((n, d), full),
            pl.BlockSpec((None, 6, n, d), lambda i: (layer, 0, 0, 0)),
            _resident(g.shape),
            _resident(w_in.shape),
            _resident(conv_w.shape),
            pl.BlockSpec((n, cw), full),
            pl.BlockSpec((n, cw), full),
        ],
        out_specs=[pl.BlockSpec((n, cw), full)] * 5,
        out_shape=[
            jax.ShapeDtypeStruct((n, cw), F32),
            jax.ShapeDtypeStruct((n, cw), F32),
            jax.ShapeDtypeStruct((n, cw), F32),
            jax.ShapeDtypeStruct((n, cw), BF16),
            jax.ShapeDtypeStruct((n, cw), F32),
        ],
        compiler_params=_params(("arbitrary",)),
        name="front_sample",
    )(x, mods_rows, g, w_in, conv_w, pe1, pe2)


def _moba_prompt_kernel(n_blk, nbp, slopes_ref, q_ref, k_ref, v_ref, o_ref,
                        kb_ref, ve_ref, vo_ref, km_ref, m_ref, l_ref, acc_ref):
    b, p, i = pl.program_id(0), pl.program_id(1), pl.program_id(2)
    blk = MOBA_BLOCK
    seq = k_ref.shape[0]
    lane = lax.broadcasted_iota(jnp.int32, (1, LANES), 1)
    head_lanes = [(lane < LANES // 2).astype(F32), (lane >= LANES // 2).astype(F32)]

    @pl.when((b == 0) & (p == 0) & (i == 0))
    def _():
        r = lax.broadcasted_iota(jnp.int32, (seq, LANES), 0)
        c = lax.broadcasted_iota(jnp.int32, (seq, LANES), 1)
        n = lax.shift_right_logical(r, MOBA_BLOCK_SHIFT)
        off = r & (blk - 1)
        feat = jnp.where(c == n, 1.0,
                         jnp.where(c == nbp, n.astype(F32),
                                   jnp.where(c == nbp + 1, off.astype(F32),
                                             jnp.where(c == nbp + 2, 1.0, 0.0))))
        kb_ref[:, LANES:2 * LANES] = feat.astype(BF16)

    @pl.when(i == 0)
    def _():
        kb_ref[:, 0:LANES] = k_ref[...].astype(BF16)
        vv = v_ref[...]
        ve_ref[...] = (vv * head_lanes[0]).astype(BF16)
        vo_ref[...] = (vv * head_lanes[1]).astype(BF16)
        km_ref[...] = jnp.zeros_like(km_ref)
        for n in range(n_blk):
            km_ref[n:n + 1, :] = jnp.sum(k_ref[n * blk:(n + 1) * blk, :], axis=0,
                                         keepdims=True) * (1.0 / blk)

    q = q_ref[...]
    q_hi, q_lo = _split_bf16(q)
    km = km_ref[...]
    blk_row = lax.broadcasted_iota(jnp.int32, (nbp, blk), 0)
    feat_row = lax.broadcasted_iota(jnp.int32, (LANES, blk), 0)
    i_f = i.astype(F32)
    v_refs = (ve_ref, vo_ref)

    qf = []
    for e in range(2):
        slope = slopes_ref[2 * p + e]
        km_hi, km_lo = _split_bf16(km * head_lanes[e])
        gate_t = (lax.dot_general(km_hi, q_hi, _NT, preferred_element_type=F32)
                  + lax.dot_general(km_lo, q_hi, _NT, preferred_element_type=F32)
                  + lax.dot_general(km_hi, q_lo, _NT, preferred_element_type=F32))
        valid = blk_row < i
        gate_t = jnp.where(valid, gate_t, -jnp.inf)
        rank = jnp.zeros((nbp, blk), jnp.int32)
        for m in range(n_blk):
            gm = gate_t[m:m + 1, :]
            beats = jnp.where(gm > gate_t, 1,
                              jnp.where(gm == gate_t, jnp.where(blk_row > m, 1, 0), 0))
            rank = rank + beats
        mask_t = jnp.where(valid, jnp.where(rank < MOBA_TOPK, 0.0, NEG_BIG),
                           jnp.where(blk_row == i, 0.0, NEG_BIG))
        consts = jnp.where(feat_row == nbp, slope * blk,
                           jnp.where(feat_row == nbp + 1, slope,
                                     jnp.where(feat_row == nbp + 2, -slope * blk * i_f, 0.0)))
        feat_t = jnp.concatenate([mask_t, jnp.zeros((LANES - nbp, blk), F32)], axis=0) + consts
        q_e = (q * (head_lanes[e] * 0.125)).astype(BF16)
        qf.append(jnp.concatenate([q_e, feat_t.T.astype(BF16)], axis=1))
        m_ref[e] = jnp.full(m_ref.shape[1:], -jnp.inf, F32)
        l_ref[e] = jnp.zeros(l_ref.shape[1:], F32)
        acc_ref[e] = jnp.zeros(acc_ref.shape[1:], F32)

    def attend(start, causal):
        kblk = kb_ref[pl.ds(start, blk), :]
        for e in range(2):
            s = lax.dot_general(qf[e], kblk, _NT, preferred_element_type=F32)
            if causal:
                rr = lax.broadcasted_iota(jnp.int32, s.shape, 0)
                cc = lax.broadcasted_iota(jnp.int32, s.shape, 1)
                s = jnp.where(cc <= rr, s, NEG_BIG)
            m_prev = m_ref[e]
            m_new = jnp.maximum(m_prev, jnp.max(s, axis=1, keepdims=True))
            alpha = jnp.exp(m_prev - m_new)
            pr = jnp.exp(s - jnp.concatenate([m_new, m_new], axis=1))
            l_ref[e] = alpha * l_ref[e] + jnp.sum(pr, axis=1, keepdims=True)
            acc_ref[e] = alpha * acc_ref[e] + jnp.dot(
                pr.astype(BF16), v_refs[e][pl.ds(start, blk), :], preferred_element_type=F32)
            m_ref[e] = m_new

    def body(n, carry):
        attend(pl.multiple_of(n * blk, blk), False)
        return carry

    lax.fori_loop(0, i, body, 0)
    attend(pl.multiple_of(i * blk, blk), True)
    o_ref[...] = (acc_ref[0] / l_ref[0] + acc_ref[1] / l_ref[1]).astype(o_ref.dtype)


def _moba_prompt(q, k, v, slopes, batch):
    n, aw = q.shape
    seq = n // batch
    blk = MOBA_BLOCK
    assert seq % blk == 0 and aw % LANES == 0
    n_blk = seq // blk
    nbp = -(-n_blk // SUBLANES) * SUBLANES
    assert nbp + 3 <= LANES
    n_pair = aw // LANES
    q_map = lambda b, p, i, s: (b * n_blk + i, p)
    kv_map = lambda b, p, i, s: (b, p)
    return pl.pallas_call(
        functools.partial(_moba_prompt_kernel, n_blk, nbp),
        grid_spec=pltpu.PrefetchScalarGridSpec(
            num_scalar_prefetch=1,
            grid=(batch, n_pair, n_blk),
            in_specs=[
                pl.BlockSpec((blk, LANES), q_map),
                pl.BlockSpec((seq, LANES), kv_map),
                pl.BlockSpec((seq, LANES), kv_map),
            ],
            out_specs=pl.BlockSpec((blk, LANES), q_map),
            scratch_shapes=[
                pltpu.VMEM((seq, 2 * LANES), BF16),
                pltpu.VMEM((seq, LANES), BF16),
                pltpu.VMEM((seq, LANES), BF16),
                pltpu.VMEM((nbp, LANES), F32),
                pltpu.VMEM((2, blk, LANES), F32),
                pltpu.VMEM((2, blk, LANES), F32),
                pltpu.VMEM((2, blk, LANES), F32),
            ],
        ),
        out_shape=jax.ShapeDtypeStruct((n, aw), BF16),
        compiler_params=_params(("arbitrary", "arbitrary", "arbitrary")),
        name="moba_prompt",
    )(slopes, q, k, v)


def _moba_sample_kernel(n_pages, page, n_heads, pt_ref, q_ref, kn_ref, vn_ref, slope_ref,
                        ck_hbm, cv_hbm, o_ref, kv_ref, s_ref, km_ref, sem):
    b = pl.program_id(0)
    blk = MOBA_BLOCK
    tq, aw = q_ref.shape
    hd = aw // n_heads
    rows = tq * n_heads
    past = n_pages * page
    n_blk = past // blk

    def page_copy(src_hbm, pg):
        return pltpu.make_async_copy(src_hbm.at[pt_ref[b, pg]],
                                     kv_ref.at[pl.ds(pg * page, page), :], sem)

    def start_all(src_hbm):
        def go(pg, c):
            page_copy(src_hbm, pg).start()
            return c
        lax.fori_loop(0, n_pages, go, 0)

    def wait_all(src_hbm):
        def go(pg, c):
            page_copy(src_hbm, pg).wait()
            return c
        lax.fori_loop(0, n_pages, go, 0)

    start_all(ck_hbm)

    lane = lax.broadcasted_iota(jnp.int32, (n_heads, aw), 1)
    hrow = lax.broadcasted_iota(jnp.int32, (n_heads, aw), 0)
    head_mask = jnp.where(lane >= hrow * hd, jnp.where(lane < (hrow + 1) * hd, 1.0, 0.0), 0.0)
    hm_rows = jnp.concatenate([head_mask] * tq, axis=0)
    q = q_ref[...]
    q_rows = jnp.concatenate(
        [jnp.broadcast_to(q[t:t + 1, :], (n_heads, aw)) for t in range(tq)], axis=0) * hm_rows
    qs_bf = (q_rows * (float(hd) ** -0.5)).astype(BF16)
    slope = jnp.concatenate([slope_ref[:, 0:1]] * tq, axis=0)
    t_row = jnp.concatenate([jnp.full((n_heads, 1), t, jnp.int32) for t in range(tq)], axis=0)

    wait_all(ck_hbm)
    km_ref[...] = jnp.zeros_like(km_ref)
    for n in range(n_blk):
        kblk = kv_ref[n * blk:(n + 1) * blk, :]
        km_ref[n:n + 1, :] = jnp.sum(kblk, axis=0, keepdims=True) * (1.0 / blk)
        s_ref[n] = lax.dot_general(qs_bf, kblk.astype(BF16), _NT, preferred_element_type=F32)
    start_all(cv_hbm)

    q_hi, q_lo = _split_bf16(q_rows)
    km_hi, km_lo = _split_bf16(km_ref[...])
    gate = (lax.dot_general(q_hi, km_hi, _NT, preferred_element_type=F32)
            + lax.dot_general(q_hi, km_lo, _NT, preferred_element_type=F32)
            + lax.dot_general(q_lo, km_hi, _NT, preferred_element_type=F32))
    blk_lane = lax.broadcasted_iota(jnp.int32, gate.shape, 1)
    gate = jnp.where(blk_lane < n_blk, gate, -jnp.inf)
    rank = jnp.zeros(gate.shape, jnp.int32)
    for m in range(n_blk):
        gm = gate[:, m:m + 1]
        rank = rank + jnp.where(gm > gate, 1,
                                jnp.where(gm == gate, jnp.where(blk_lane > m, 1, 0), 0))
    sel_bias = jnp.where(blk_lane < n_blk,
                         jnp.where(rank < min(MOBA_TOPK, n_blk), 0.0, NEG_BIG), NEG_BIG)

    kn = kn_ref[...]
    vn = vn_ref[...]
    q_sc = q_rows * (float(hd) ** -0.5)
    own = []
    for t in range(tq):
        so = jnp.sum(q_sc * kn[t:t + 1, :], axis=1, keepdims=True)
        so = so - slope * (t_row - t).astype(F32)
        own.append(jnp.where(t_row >= t, so, NEG_BIG))
    m_run = functools.reduce(jnp.maximum, own)
    key_off = lax.broadcasted_iota(jnp.int32, (rows, blk), 1)
    for n in range(n_blk):
        dist = (past - n * blk) + t_row - key_off
        s = s_ref[n] - slope * dist.astype(F32) + sel_bias[:, n:n + 1]
        s_ref[n] = s
        m_run = jnp.maximum(m_run, jnp.max(s, axis=1, keepdims=True))

    acc = jnp.zeros((rows, aw), F32)
    l_run = jnp.zeros((rows, 1), F32)
    for t in range(tq):
        po = jnp.exp(own[t] - m_run)
        l_run = l_run + po
        acc = acc + po * vn[t:t + 1, :]
    wait_all(cv_hbm)
    for n in range(n_blk):
        pr = jnp.exp(s_ref[n] - m_run)
        l_run = l_run + jnp.sum(pr, axis=1, keepdims=True)
        acc = acc + jnp.dot(pr.astype(BF16), kv_ref[n * blk:(n + 1) * blk, :].astype(BF16),
                            preferred_element_type=F32)
    out_rows = acc * hm_rows / l_run
    o_ref[...] = jnp.sum(out_rows.reshape(tq, n_heads, aw), axis=1).astype(o_ref.dtype)


def _moba_sample(q, k_new, v_new, cache_k, cache_v, page_table, slope_tab, n_heads):
    bd, tq, aw = q.shape
    n_pages = page_table.shape[1]
    page = cache_k.shape[1]
    past = n_pages * page
    assert past % MOBA_BLOCK == 0 and past // MOBA_BLOCK <= LANES
    n_blk = past // MOBA_BLOCK
    rows = tq * n_heads
    tok = lambda b, pt: (b, 0, 0)
    return pl.pallas_call(
        functools.partial(_moba_sample_kernel, n_pages, page, n_heads),
        grid_spec=pltpu.PrefetchScalarGridSpec(
            num_scalar_prefetch=1,
            grid=(bd,),
            in_specs=[
                pl.BlockSpec((None, tq, aw), tok),
                pl.BlockSpec((None, tq, aw), tok),
                pl.BlockSpec((None, tq, aw), tok),
                pl.BlockSpec(slope_tab.shape, lambda b, pt: (0, 0)),
                pl.BlockSpec(memory_space=pl.ANY),
                pl.BlockSpec(memory_space=pl.ANY),
            ],
            out_specs=pl.BlockSpec((None, tq, aw), tok),
            scratch_shapes=[
                pltpu.VMEM((past, aw), F32),
                pltpu.VMEM((n_blk, rows, MOBA_BLOCK), F32),
                pltpu.VMEM((LANES, aw), F32),
                pltpu.SemaphoreType.DMA(()),
            ],
        ),
        out_shape=jax.ShapeDtypeStruct((bd, tq, aw), BF16),
        compiler_params=_params(("arbitrary",)),
        name="moba_sample",
    )(page_table, q, k_new, v_new, slope_tab, cache_k, cache_v)


def _ffn_residual(x1, mod_ref, g, w1_ref, w2_ref, n_chunk):
    h = (_rms(x1, g) * (1.0 + mod_ref[4]) + mod_ref[3]).astype(BF16)
    ck = w1_ref.shape[1] // n_chunk
    acc = jnp.zeros(x1.shape, F32)
    for j in range(n_chunk):
        hid = jnp.dot(h, w1_ref[:, j * ck:(j + 1) * ck], preferred_element_type=F32)
        hid = jnp.square(jnp.maximum(hid, 0.0)).astype(BF16)
        acc = acc + jnp.dot(hid, w2_ref[j * ck:(j + 1) * ck, :], preferred_element_type=F32)
    return x1 + mod_ref[5] * acc


def _back_ab_kernel(n_chunk, x_ref, attn_ref, gated_ref, mod_ref, g_ref, wo_ref, w1_ref, w2_ref,
                    o_ref):
    aw = attn_ref.shape[1]
    mix = (jnp.dot(attn_ref[...], wo_ref[0:aw, :], preferred_element_type=F32)
           + jnp.dot(gated_ref[...], wo_ref[aw:, :], preferred_element_type=F32))
    x1 = x_ref[...] + mod_ref[2] * mix
    o_ref[...] = _ffn_residual(x1, mod_ref, g_ref[...], w1_ref, w2_ref, n_chunk)


def _mod_spec(mods, layer, n_rows_per_group, tm):
    d = mods.shape[-1]
    if mods.ndim == 5:
        nt = n_rows_per_group // tm
        return pl.BlockSpec((None, None, 6, 1, d), lambda r: (layer, r // nt, 0, 0, 0))
    return pl.BlockSpec((None, 6, tm, d), lambda r: (layer, 0, r, 0))


def _back_ab(x, attn, gated, mods, g, w_out, w1, w2, layer, rows_per_group, tm):
    n, d = x.shape
    aw = attn.shape[1]
    cw = gated.shape[1]
    row = lambda r: (r, 0)
    return pl.pallas_call(
        functools.partial(_back_ab_kernel, 4),
        grid=(n // tm,),
        in_specs=[
            pl.BlockSpec((tm, d), row),
            pl.BlockSpec((tm, aw), row),
            pl.BlockSpec((tm, cw), row),
            _mod_spec(mods, layer, rows_per_group, tm),
            _resident(g.shape),
            _resident(w_out.shape),
            _resident(w1.shape),
            _resident(w2.shape),
        ],
        out_specs=pl.BlockSpec((tm, d), row),
        out_shape=jax.ShapeDtypeStruct((n, d), F32),
        compiler_params=_params(("arbitrary",)),
        name="back_ab",
    )(x, attn, gated, mods, g, w_out, w1, w2)


def _gmlp_kernel(n_chunk, emit_v, x_ref, mod_ref, g1_ref, g2_ref, wi_ref, gv_ref, ws_ref, bs_ref,
                 wo_ref, w1_ref, w2_ref, gf_ref, *rest):
    if emit_v:
        o_ref, cv_ref, us_ref = rest
    else:
        o_ref, us_ref = rest
    tm = x_ref.shape[0]
    cwid = gv_ref.shape[1]
    n_grp, chunk, _ = ws_ref.shape
    grp = cwid // n_grp
    x = x_ref[...]
    hb = (_rms(x, g1_ref[...]) * (1.0 + mod_ref[1]) + mod_ref[0]).astype(BF16)
    u = jnp.dot(hb, wi_ref[:, 0:cwid], preferred_element_type=F32)
    v = _rms(jnp.dot(hb, wi_ref[:, cwid:], preferred_element_type=F32), gv_ref[...])
    if emit_v:
        cv_ref[...] = v
    vb = v.astype(BF16)
    for c in range(tm // chunk):
        rs = slice(c * chunk, (c + 1) * chunk)
        for gi in range(n_grp):
            ls = slice(gi * grp, (gi + 1) * grp)
            s = jnp.dot(ws_ref[gi], vb[rs, ls], preferred_element_type=F32) + bs_ref[gi]
            us_ref[rs, ls] = (u[rs, ls] * s).astype(BF16)
    mix = jnp.dot(us_ref[...], wo_ref[...], preferred_element_type=F32)
    x1 = x + mod_ref[2] * mix
    x2 = _ffn_residual(x1, mod_ref, g2_ref[...], w1_ref, w2_ref, n_chunk)
    o_ref[...] = _rms(x2, gf_ref[...])


def _gmlp_layer(x, mods, g1, g2, w_in, g_v, ws_mat, bs_full, w_out, w1, w2, g_final, layer,
                rows_per_group, tm, emit_v):
    n, d = x.shape
    cwid = g_v.shape[1]
    row = lambda r: (r, 0)
    out_shape = [jax.ShapeDtypeStruct((n, d), F32)]
    out_specs = [pl.BlockSpec((tm, d), row)]
    if emit_v:
        out_shape.append(jax.ShapeDtypeStruct((n, cwid), F32))
        out_specs.append(pl.BlockSpec((tm, cwid), row))
    return pl.pallas_call(
        functools.partial(_gmlp_kernel, 4, emit_v),
        grid=(n // tm,),
        in_specs=[
            pl.BlockSpec((tm, d), row),
            _mod_spec(mods, layer, rows_per_group, tm),
            _resident(g1.shape),
            _resident(g2.shape),
            _resident(w_in.shape),
            _resident(g_v.shape),
            _resident(ws_mat.shape),
            _resident(bs_full.shape),
            _resident(w_out.shape),
            _resident(w1.shape),
            _resident(w2.shape),
            _resident(g_final.shape),
        ],
        out_specs=out_specs,
        out_shape=out_shape,
        scratch_shapes=[pltpu.VMEM((tm, cwid), BF16)],
        compiler_params=_params(("arbitrary",)),
        name="gmlp_layer",
    )(x, mods, g1, g2, w_in, g_v, ws_mat, bs_full, w_out, w1, w2, g_final)


def _alibi_slopes(n_heads):
    return jnp.exp2(-8.0 * jnp.arange(1, n_heads + 1, dtype=F32) / n_heads)


def kernel(x_prompt, x_sample, cache_k, cache_v, state_conv, page_table, c_prompt, c_sample,
           norm_g, w_ada, b_ada, w_in_ab, conv_w, w_out_ab, w_in_c, g_v, w_s, b_s, w_out_c,
           w_ff1, w_ff2, g_final):
    batch, seq, d = x_prompt.shape
    bd, tq, _ = x_sample.shape
    n_ab, n_pool, page, n_heads, hd = cache_k.shape
    aw = n_heads * hd
    cw = state_conv.shape[-1]
    n_c, n_grp, chunk, _ = w_s.shape
    cwid = g_v.shape[-1]
    depth = norm_g.shape[0]
    assert depth == 2 and n_ab == 1 and n_c == 1 and hd * 2 == LANES
    assert tq <= chunk and chunk % tq == 0 and (bd * tq) % chunk == 0
    n_p, n_s = batch * seq, bd * tq
    tm = 512 if seq % 512 == 0 else chunk

    wb = lambda w: w.astype(BF16)
    w_ada_b, w_in_ab_b, w_out_ab_b = wb(w_ada), wb(w_in_ab[0]), wb(w_out_ab[0])
    w_in_c_b, w_out_c_b = wb(w_in_c[0]), wb(w_out_c[0])
    w_ff1_b, w_ff2_b = wb(w_ff1), wb(w_ff2)

    bp = -(-(batch + bd) // SUBLANES) * SUBLANES
    c_all = jnp.concatenate([c_prompt, c_sample, jnp.zeros((bp - batch - bd, d), F32)], axis=0)
    ada = _ada_terms(c_all, w_ada_b, b_ada)
    mods_p = ada[:, :batch].reshape(depth, batch, 6, 1, d)
    mods_s = jnp.repeat(ada[:, batch:batch + bd].reshape(depth, bd, 6, d), tq, axis=1)
    mods_s = mods_s.transpose(0, 2, 1, 3)

    slopes = _alibi_slopes(n_heads)
    slope_tab = jnp.broadcast_to(slopes[:, None], (n_heads, LANES))
    g = lambda l, j: norm_g[l, j].reshape(1, d)

    tril = jnp.tril(jnp.ones((chunk, chunk), bool))
    ws_p = jnp.where(tril, w_s[0], 0.0).astype(BF16)
    bs_p = jnp.broadcast_to(b_s[0][:, :, None], (n_grp, chunk, cwid // n_grp))
    ws_small = jnp.where(tril[:tq, :tq], w_s[0][:, :tq, :tq], 0.0)
    eye = jnp.eye(chunk // tq, dtype=F32)
    ws_s = jnp.einsum("ab,gts->gatbs", eye, ws_small).reshape(n_grp, chunk, chunk).astype(BF16)
    bs_s = jnp.broadcast_to(jnp.tile(b_s[0][:, :tq], (1, chunk // tq))[:, :, None],
                            (n_grp, chunk, cwid // n_grp))

    xp = x_prompt.reshape(n_p, d)
    prev0 = jnp.zeros((batch, CONV_W - 1, cw), F32)
    qp, kp, vp, gated_p, conv_p = _front_prompt(xp, mods_p, g(0, 0), w_in_ab_b, conv_w[0], prev0,
                                                0, tm)
    attn_p = _moba_prompt(qp, kp, vp, slopes, batch)
    xp = _back_ab(xp, attn_p, gated_p, mods_p, g(0, 1), w_out_ab_b, w_ff1_b[0], w_ff2_b[0], 0,
                  seq, tm)
    (yp,) = _gmlp_layer(xp, mods_p, g(1, 0), g(1, 1), w_in_c_b, g_v, ws_p, bs_p, w_out_c_b,
                        w_ff1_b[1], w_ff2_b[1], g_final.reshape(1, d), 1, seq, tm, False)

    xs = x_sample.reshape(n_s, d)
    st = state_conv[0]
    pad = lambda a: jnp.concatenate([a, jnp.zeros((bd, tq - a.shape[1], cw), F32)], axis=1)
    pe1 = pad(st[:, 1:2]).reshape(n_s, cw)
    pe2 = pad(st).reshape(n_s, cw)
    qs, ks, vs, gated_s, u_s = _front_sample(xs, mods_s, g(0, 0), w_in_ab_b, conv_w[0], pe1, pe2,
                                             0, tq)
    attn_s = _moba_sample(qs.reshape(bd, tq, aw), ks.reshape(bd, tq, aw), vs.reshape(bd, tq, aw),
                          cache_k.reshape(n_pool, page, aw),
                          cache_v.reshape(n_pool, page, aw), page_table, slope_tab, n_heads)
    xs = _back_ab(xs, attn_s.reshape(n_s, aw), gated_s, mods_s, g(0, 1), w_out_ab_b, w_ff1_b[0],
                  w_ff2_b[0], 0, n_s, chunk)
    ys, cv_s = _gmlp_layer(xs, mods_s, g(1, 0), g(1, 1), w_in_c_b, g_v, ws_s, bs_s, w_out_c_b,
                           w_ff1_b[1], w_ff2_b[1], g_final.reshape(1, d), 1, n_s, chunk, True)

    n_pages_new = n_p // page
    return (
        yp.reshape(batch, seq, d),
        ys.reshape(bd, tq, d),
        kp.reshape(1, n_pages_new, page, n_heads, hd),
        vp.reshape(1, n_pages_new, page, n_heads, hd),
        conv_p[None],
        ks.reshape(1, bd, tq, n_heads, hd),
        vs.reshape(1, bd, tq, n_heads, hd),
        u_s.reshape(bd, tq, cw)[:, tq - (CONV_W - 1):][None],
        cv_s.reshape(1, bd, tq, cwid),
    )
```

```python
import functools

import jax
import jax.numpy as jnp
from jax import lax
from jax.experimental import pallas as pl
from jax.experimental.pallas import tpu as pltpu

F32 = jnp.float32
BF16 = jnp.bfloat16

EPS = 1e-6
MOBA_BLOCK = 256
MOBA_BLOCK_SHIFT = 8
MOBA_TOPK = 3
CONV_W = 3
LANES = 128
SUBLANES = 8
NEG_BIG = -(2.0 ** 100)
VMEM_LIMIT_BYTES = 56 * 1024 * 1024

_NT = (((1,), (1,)), ((), ()))


def _params(sem):
    return pltpu.CompilerParams(dimension_semantics=sem, vmem_limit_bytes=VMEM_LIMIT_BYTES)


def _resident(shape):
    n = len(shape)
    return pl.BlockSpec(shape, lambda *_: (0,) * n, pipeline_mode=pl.Buffered(1))


def _rms(x, g):
    return x * lax.rsqrt(jnp.mean(x * x, axis=-1, keepdims=True) + EPS) * g


def _split_bf16(x):
    hi = x.astype(BF16)
    lo = (x - hi.astype(F32)).astype(BF16)
    return hi, lo


def _ada_kernel(c_ref, w_ref, b_ref, o_ref):
    c = c_ref[...]
    s = (c * jax.nn.sigmoid(c)).astype(BF16)
    o_ref[0] = jnp.dot(s, w_ref[0], preferred_element_type=F32) + b_ref[0]


def _ada_terms(c_all, w_ada, b_ada):
    n_layers, d, d6 = w_ada.shape
    bp = c_all.shape[0]
    tn = d6 // 4
    return pl.pallas_call(
        _ada_kernel,
        grid=(n_layers, d6 // tn),
        in_specs=[
            pl.BlockSpec((bp, d), lambda l, j: (0, 0)),
            pl.BlockSpec((1, d, tn), lambda l, j: (l, 0, j)),
            pl.BlockSpec((1, 1, tn), lambda l, j: (l, 0, j)),
        ],
        out_specs=pl.BlockSpec((1, bp, tn), lambda l, j: (l, 0, j)),
        out_shape=jax.ShapeDtypeStruct((n_layers, bp, d6), F32),
        compiler_params=_params(("arbitrary", "arbitrary")),
        name="ada_terms",
    )(c_all, w_ada, b_ada.reshape(n_layers, 1, d6))


def _front_kernel(multi_seq, seq_len, *refs):
    if multi_seq:
        (x_ref, mod_ref, g_ref, w_ref, cw_ref, pe1_ref, pe2_ref,
         q_ref, k_ref, v_ref, gated_ref, u_ref) = refs
    else:
        (x_ref, mod_ref, g_ref, w_ref, cw_ref, prev_ref,
         q_ref, kb_ref, vb_ref, kt_ref, vt_ref, km_ref, gated_ref, cs_ref, carry_ref) = refs
    tm = x_ref.shape[0]
    cw = q_ref.shape[1]
    h = _rms(x_ref[...], g_ref[...]) * (1.0 + mod_ref[1]) + mod_ref[0]
    hb = h.astype(BF16)

    def proj(j):
        return jnp.dot(hb, w_ref[:, j * cw:(j + 1) * cw], preferred_element_type=F32)

    q_ref[...] = proj(0)
    if multi_seq:
        k_ref[...] = proj(1)
        v_ref[...] = proj(2)
    else:
        n_page, n_heads, hd, page = kt_ref.shape
        yk = proj(1)
        kb_ref[...] = yk.astype(BF16)
        for j in range(tm // MOBA_BLOCK):
            km_ref[j] = jnp.mean(yk[j * MOBA_BLOCK:(j + 1) * MOBA_BLOCK, :], axis=0, keepdims=True)
        yv = proj(2)
        vb_ref[...] = yv.astype(BF16)
        for y, t_ref in ((yk, kt_ref), (yv, vt_ref)):
            yt = y.T
            for j in range(n_page):
                t_ref[j] = yt[:, j * page:(j + 1) * page].reshape(n_heads, hd, page)
    u = proj(4) * proj(5)
    rows = lax.broadcasted_iota(jnp.int32, u.shape, 0)
    u1 = pltpu.roll(u, 1, axis=0)
    u2 = pltpu.roll(u, 2, axis=0)
    if multi_seq:
        assert seq_len & (seq_len - 1) == 0
        t = rows & (seq_len - 1)
        u1 = jnp.where(t == 0, pe1_ref[...], u1)
        u2 = jnp.where(t < 2, pe2_ref[...], u2)
        u_ref[...] = u
    else:
        i = pl.program_id(1)

        @pl.when(i == 0)
        def _():
            carry_ref[SUBLANES - 2:SUBLANES, :] = prev_ref[...]

        p0 = carry_ref[SUBLANES - 2:SUBLANES - 1, :]
        p1 = carry_ref[SUBLANES - 1:SUBLANES, :]
        u1 = jnp.where(rows == 0, p1, u1)
        u2 = jnp.where(rows == 0, p0, jnp.where(rows == 1, p1, u2))
        carry_ref[...] = u[tm - SUBLANES:tm, :]

        @pl.when(i == pl.num_programs(1) - 1)
        def _():
            cs_ref[...] = u[tm - 2:tm, :]

    yc = u2 * cw_ref[0:1, :] + u1 * cw_ref[1:2, :] + u * cw_ref[2:3, :]
    gated_ref[...] = (proj(3) * yc).astype(BF16)


def _front_prompt(x, mods, g, w_in, conv_w, prev, layer, tm, n_heads, page):
    n, d = x.shape
    nb = prev.shape[0]
    cw = prev.shape[2]
    nt = n // nb // tm
    assert tm % page == 0 and tm % MOBA_BLOCK == 0
    row = lambda b, i: (b * nt + i, 0)
    pages = (tm // page, n_heads, cw // n_heads, page)
    page_map = lambda b, i: (b * nt + i, 0, 0, 0)
    n_mean = tm // MOBA_BLOCK
    return pl.pallas_call(
        functools.partial(_front_kernel, False, 0),
        grid=(nb, nt),
        in_specs=[
            pl.BlockSpec((tm, d), row),
            pl.BlockSpec((None, None, 6, 1, d), lambda b, i: (layer, b, 0, 0, 0)),
            _resident(g.shape),
            _resident(w_in.shape),
            _resident(conv_w.shape),
            pl.BlockSpec((None, CONV_W - 1, cw), lambda b, i: (b, 0, 0)),
        ],
        out_specs=[
            pl.BlockSpec((tm, cw), row),
            pl.BlockSpec((tm, cw), row),
            pl.BlockSpec((tm, cw), row),
            pl.BlockSpec(pages, page_map),
            pl.BlockSpec(pages, page_map),
            pl.BlockSpec((n_mean, 1, cw), lambda b, i: (b * nt + i, 0, 0)),
            pl.BlockSpec((tm, cw), row),
            pl.BlockSpec((None, CONV_W - 1, cw), lambda b, i: (b, 0, 0)),
        ],
        out_shape=[
            jax.ShapeDtypeStruct((n, cw), F32),
            jax.ShapeDtypeStruct((n, cw), BF16),
            jax.ShapeDtypeStruct((n, cw), BF16),
            jax.ShapeDtypeStruct((n // page,) + pages[1:], F32),
            jax.ShapeDtypeStruct((n // page,) + pages[1:], F32),
            jax.ShapeDtypeStruct((n // MOBA_BLOCK, 1, cw), F32),
            jax.ShapeDtypeStruct((n, cw), BF16),
            jax.ShapeDtypeStruct(prev.shape, F32),
        ],
        scratch_shapes=[pltpu.VMEM((SUBLANES, cw), F32)],
        compiler_params=_params(("arbitrary", "arbitrary")),
        name="front_prompt",
    )(x, mods, g, w_in, conv_w, prev)


def _front_sample(x, mods_rows, g, w_in, conv_w, pe1, pe2, layer, seq_len):
    n, d = x.shape
    cw = pe1.shape[1]
    full = lambda i: (0, 0)
    return pl.pallas_call(
        functools.partial(_front_kernel, True, seq_len),
        grid=(1,),
        in_specs=[
            pl.BlockSpec((n, d), full),
            pl.BlockSpec((None, 6, n, d), lambda i: (layer, 0, 0, 0)),
            _resident(g.shape),
            _resident(w_in.shape),
            _resident(conv_w.shape),
            pl.BlockSpec((n, cw), full),
            pl.BlockSpec((n, cw), full),
        ],
        out_specs=[pl.BlockSpec((n, cw), full)] * 5,
        out_shape=[
            jax.ShapeDtypeStruct((n, cw), F32),
            jax.ShapeDtypeStruct((n, cw), F32),
            jax.ShapeDtypeStruct((n, cw), F32),
            jax.ShapeDtypeStruct((n, cw), BF16),
            jax.ShapeDtypeStruct((n, cw), F32),
        ],
        compiler_params=_params(("arbitrary",)),
        name="front_sample",
    )(x, mods_rows, g, w_in, conv_w, pe1, pe2)


def _moba_prompt_kernel(n_blk, nbp, slopes_ref, q_ref, k_ref, v_ref, km_ref, o_ref,
                        kb_ref, ve_ref, vo_ref, m_ref, l_ref, acc_ref):
    b, p, i = pl.program_id(0), pl.program_id(1), pl.program_id(2)
    blk = MOBA_BLOCK
    seq = k_ref.shape[0]
    lane = lax.broadcasted_iota(jnp.int32, (1, LANES), 1)
    head_lanes = [(lane < LANES // 2).astype(F32), (lane >= LANES // 2).astype(F32)]

    @pl.when((b == 0) & (p == 0) & (i == 0))
    def _():
        r = lax.broadcasted_iota(jnp.int32, (seq, LANES), 0)
        c = lax.broadcasted_iota(jnp.int32, (seq, LANES), 1)
        n = lax.shift_right_logical(r, MOBA_BLOCK_SHIFT)
        off = r & (blk - 1)
        feat = jnp.where(c == n, 1.0,
                         jnp.where(c == nbp, n.astype(F32),
                                   jnp.where(c == nbp + 1, off.astype(F32),
                                             jnp.where(c == nbp + 2, 1.0, 0.0))))
        kb_ref[:, LANES:2 * LANES] = feat.astype(BF16)

    @pl.when(i == 0)
    def _():
        kb_ref[:, 0:LANES] = k_ref[...]
        vv = v_ref[...].astype(F32)
        ve_ref[...] = (vv * head_lanes[0]).astype(BF16)
        vo_ref[...] = (vv * head_lanes[1]).astype(BF16)

    q = q_ref[...]
    q_hi, q_lo = _split_bf16(q)
    km = km_ref[...]
    blk_row = lax.broadcasted_iota(jnp.int32, (nbp, blk), 0)
    feat_row = lax.broadcasted_iota(jnp.int32, (LANES, blk), 0)
    i_f = i.astype(F32)
    v_refs = (ve_ref, vo_ref)

    qf = []
    for e in range(2):
        slope = slopes_ref[2 * p + e]
        km_hi, km_lo = _split_bf16(km * head_lanes[e])
        gate_t = (lax.dot_general(km_hi, q_hi, _NT, preferred_element_type=F32)
                  + lax.dot_general(km_lo, q_hi, _NT, preferred_element_type=F32)
                  + lax.dot_general(km_hi, q_lo, _NT, preferred_element_type=F32))
        valid = blk_row < i
        gate_t = jnp.where(valid, gate_t, -jnp.inf)
        rank = jnp.zeros((nbp, blk), jnp.int32)
        for m in range(n_blk):
            gm = gate_t[m:m + 1, :]
            beats = jnp.where(gm > gate_t, 1,
                              jnp.where(gm == gate_t, jnp.where(blk_row > m, 1, 0), 0))
            rank = rank + beats
        mask_t = jnp.where(valid, jnp.where(rank < MOBA_TOPK, 0.0, NEG_BIG),
                           jnp.where(blk_row == i, 0.0, NEG_BIG))
        consts = jnp.where(feat_row == nbp, slope * blk,
                           jnp.where(feat_row == nbp + 1, slope,
                                     jnp.where(feat_row == nbp + 2, -slope * blk * i_f, 0.0)))
        feat_t = jnp.concatenate([mask_t, jnp.zeros((LANES - nbp, blk), F32)], axis=0) + consts
        q_e = (q * (head_lanes[e] * float(LANES // 2) ** -0.5)).astype(BF16)
        qf.append(jnp.concatenate([q_e, feat_t.T.astype(BF16)], axis=1))
        m_ref[e] = jnp.full(m_ref.shape[1:], -jnp.inf, F32)
        l_ref[e] = jnp.zeros(l_ref.shape[1:], F32)
        acc_ref[e] = jnp.zeros(acc_ref.shape[1:], F32)

    def attend(start, causal):
        kblk = kb_ref[pl.ds(start, blk), :]
        for e in range(2):
            s = lax.dot_general(qf[e], kblk, _NT, preferred_element_type=F32)
            if causal:
                rr = lax.broadcasted_iota(jnp.int32, s.shape, 0)
                cc = lax.broadcasted_iota(jnp.int32, s.shape, 1)
                s = jnp.where(cc <= rr, s, NEG_BIG)
            m_prev = m_ref[e]
            m_new = jnp.maximum(m_prev, jnp.max(s, axis=1, keepdims=True))
            alpha = jnp.exp(m_prev - m_new)
            pr = jnp.exp(s - jnp.concatenate([m_new, m_new], axis=1))
            l_ref[e] = alpha * l_ref[e] + jnp.sum(pr, axis=1, keepdims=True)
            acc_ref[e] = alpha * acc_ref[e] + jnp.dot(
                pr.astype(BF16), v_refs[e][pl.ds(start, blk), :], preferred_element_type=F32)
            m_ref[e] = m_new

    def body(n, carry):
        attend(pl.multiple_of(n * blk, blk), False)
        return carry

    lax.fori_loop(0, i, body, 0)
    attend(pl.multiple_of(i * blk, blk), True)
    o_ref[...] = (acc_ref[0] / l_ref[0] + acc_ref[1] / l_ref[1]).astype(o_ref.dtype)


def _moba_prompt(q, k, v, k_mean, slopes, batch):
    n, aw = q.shape
    seq = n // batch
    blk = MOBA_BLOCK
    assert seq % blk == 0 and aw % LANES == 0
    n_blk = seq // blk
    nbp = n_blk
    assert nbp % SUBLANES == 0 and nbp + 3 <= LANES
    n_pair = aw // LANES
    q_map = lambda b, p, i, s: (b * n_blk + i, p)
    kv_map = lambda b, p, i, s: (b, p)
    return pl.pallas_call(
        functools.partial(_moba_prompt_kernel, n_blk, nbp),
        grid_spec=pltpu.PrefetchScalarGridSpec(
            num_scalar_prefetch=1,
            grid=(batch, n_pair, n_blk),
            in_specs=[
                pl.BlockSpec((blk, LANES), q_map),
                pl.BlockSpec((seq, LANES), kv_map),
                pl.BlockSpec((seq, LANES), kv_map),
                pl.BlockSpec((n_blk, LANES), kv_map),
            ],
            out_specs=pl.BlockSpec((blk, LANES), q_map),
            scratch_shapes=[
                pltpu.VMEM((seq, 2 * LANES), BF16),
                pltpu.VMEM((seq, LANES), BF16),
                pltpu.VMEM((seq, LANES), BF16),
                pltpu.VMEM((2, blk, LANES), F32),
                pltpu.VMEM((2, blk, LANES), F32),
                pltpu.VMEM((2, blk, LANES), F32),
            ],
        ),
        out_shape=jax.ShapeDtypeStruct((n, aw), BF16),
        compiler_params=_params(("arbitrary", "arbitrary", "arbitrary")),
        name="moba_prompt",
    )(slopes, q, k, v, k_mean)


def _moba_sample_kernel(n_pages, n_heads, pt_ref, q_ref, kn_ref, vn_ref, slope_ref,
                        ck_hbm, cv_hbm, o_ref, kbuf, vbuf, s_ref, km_ref, ksem, vsem):
    b = pl.program_id(0)
    blk = MOBA_BLOCK
    tq, aw = q_ref.shape
    page = kbuf.shape[2]
    ppb = blk // page
    hd = aw // n_heads
    rows = tq * n_heads
    past = n_pages * page
    n_blk = past // blk

    def page_copy(src_hbm, buf, sem, sample, pg):
        return pltpu.make_async_copy(src_hbm.at[pt_ref[sample, pg]], buf.at[pg], sem)

    def for_pages(fn):
        def go(pg, c):
            fn(pg)
            return c
        lax.fori_loop(0, n_pages, go, 0)

    @pl.when(b == 0)
    def _():
        for_pages(lambda pg: page_copy(ck_hbm, kbuf, ksem, b, pg).start())

    for_pages(lambda pg: page_copy(cv_hbm, vbuf, vsem, b, pg).start())

    lane = lax.broadcasted_iota(jnp.int32, (n_heads, aw), 1)
    hrow = lax.broadcasted_iota(jnp.int32, (n_heads, aw), 0)
    head_mask = jnp.where(lane >= hrow * hd, jnp.where(lane < (hrow + 1) * hd, 1.0, 0.0), 0.0)
    hm_rows = jnp.concatenate([head_mask] * tq, axis=0)
    q = q_ref[...]
    q_rows = jnp.concatenate(
        [jnp.broadcast_to(q[t:t + 1, :], (n_heads, aw)) for t in range(tq)], axis=0) * hm_rows
    qs_bf = (q_rows * (float(hd) ** -0.5)).astype(BF16)
    slope = jnp.concatenate([slope_ref[:, 0:1]] * tq, axis=0)
    t_row = jnp.concatenate([jnp.full((n_heads, 1), t, jnp.int32) for t in range(tq)], axis=0)

    for_pages(lambda pg: page_copy(ck_hbm, kbuf, ksem, b, pg).wait())
    km_ref[...] = jnp.zeros_like(km_ref)
    for n in range(n_blk):
        ksum = kbuf[n * ppb]
        for j in range(1, ppb):
            ksum = ksum + kbuf[n * ppb + j]
        km_ref[:, n:n + 1] = jnp.sum(ksum, axis=1, keepdims=True) * (1.0 / blk)
    for pg in range(n_pages):
        s_ref[pg] = jnp.dot(qs_bf, kbuf[pg].astype(BF16), preferred_element_type=F32)

    @pl.when(b + 1 < pl.num_programs(0))
    def _():
        for_pages(lambda pg: page_copy(ck_hbm, kbuf, ksem, b + 1, pg).start())

    q_hi, q_lo = _split_bf16(q_rows)
    km_hi, km_lo = _split_bf16(km_ref[...])
    gate = (jnp.dot(q_hi, km_hi, preferred_element_type=F32)
            + jnp.dot(q_hi, km_lo, preferred_element_type=F32)
            + jnp.dot(q_lo, km_hi, preferred_element_type=F32))
    blk_lane = lax.broadcasted_iota(jnp.int32, gate.shape, 1)
    gate = jnp.where(blk_lane < n_blk, gate, -jnp.inf)
    rank = jnp.zeros(gate.shape, jnp.int32)
    for m in range(n_blk):
        gm = gate[:, m:m + 1]
        rank = rank + jnp.where(gm > gate, 1,
                                jnp.where(gm == gate, jnp.where(blk_lane > m, 1, 0), 0))
    sel_bias = jnp.where(blk_lane < n_blk,
                         jnp.where(rank < min(MOBA_TOPK, n_blk), 0.0, NEG_BIG), NEG_BIG)

    kn = kn_ref[...]
    vn = vn_ref[...]
    q_sc = q_rows * (float(hd) ** -0.5)
    own = []
    for t in range(tq):
        so = jnp.sum(q_sc * kn[t:t + 1, :], axis=1, keepdims=True)
        so = so - slope * (t_row - t).astype(F32)
        own.append(jnp.where(t_row >= t, so, NEG_BIG))
    m_run = functools.reduce(jnp.maximum, own)
    key_off = lax.broadcasted_iota(jnp.int32, (rows, page), 1)
    for pg in range(n_pages):
        dist = (past - pg * page) + t_row - key_off
        n = pg // ppb
        s = s_ref[pg] - slope * dist.astype(F32) + sel_bias[:, n:n + 1]
        s_ref[pg] = s
        m_run = jnp.maximum(m_run, jnp.max(s, axis=1, keepdims=True))

    acc = jnp.zeros((rows, aw), F32)
    l_run = jnp.zeros((rows, 1), F32)
    for t in range(tq):
        po = jnp.exp(own[t] - m_run)
        l_run = l_run + po
        acc = acc + po * vn[t:t + 1, :]
    for_pages(lambda pg: page_copy(cv_hbm, vbuf, vsem, b, pg).wait())
    for pg in range(n_pages):
        pr = jnp.exp(s_ref[pg] - m_run)
        l_run = l_run + jnp.sum(pr, axis=1, keepdims=True)
        acc = acc + lax.dot_general(pr.astype(BF16), vbuf[pg].astype(BF16), _NT,
                                    preferred_element_type=F32)
    out_rows = acc * hm_rows / l_run
    o_ref[...] = jnp.sum(out_rows.reshape(tq, n_heads, aw), axis=1).astype(o_ref.dtype)


def _moba_sample(q, k_new, v_new, cache_k, cache_v, page_table, slope_tab, n_heads):
    bd, tq, aw = q.shape
    n_pages = page_table.shape[1]
    page = cache_k.shape[2]
    past = n_pages * page
    assert MOBA_BLOCK % page == 0 and past % MOBA_BLOCK == 0 and past // MOBA_BLOCK <= LANES
    rows = tq * n_heads
    tok = lambda b, pt: (b, 0, 0)
    return pl.pallas_call(
        functools.partial(_moba_sample_kernel, n_pages, n_heads),
        grid_spec=pltpu.PrefetchScalarGridSpec(
            num_scalar_prefetch=1,
            grid=(bd,),
            in_specs=[
                pl.BlockSpec((None, tq, aw), tok),
                pl.BlockSpec((None, tq, aw), tok),
                pl.BlockSpec((None, tq, aw), tok),
                pl.BlockSpec(slope_tab.shape, lambda b, pt: (0, 0)),
                pl.BlockSpec(memory_space=pl.ANY),
                pl.BlockSpec(memory_space=pl.ANY),
            ],
            out_specs=pl.BlockSpec((None, tq, aw), tok),
            scratch_shapes=[
                pltpu.VMEM((n_pages, aw, page), F32),
                pltpu.VMEM((n_pages, aw, page), F32),
                pltpu.VMEM((n_pages, rows, page), F32),
                pltpu.VMEM((aw, LANES), F32),
                pltpu.SemaphoreType.DMA(()),
                pltpu.SemaphoreType.DMA(()),
            ],
        ),
        out_shape=jax.ShapeDtypeStruct((bd, tq, aw), BF16),
        compiler_params=_params(("arbitrary",)),
        name="moba_sample",
    )(page_table, q, k_new, v_new, slope_tab, cache_k, cache_v)


def _ffn_residual(x1, mod_ref, g, w1_ref, w2_ref, n_chunk):
    h = (_rms(x1, g) * (1.0 + mod_ref[4]) + mod_ref[3]).astype(BF16)
    ck = w1_ref.shape[1] // n_chunk
    acc = jnp.zeros(x1.shape, F32)
    for j in range(n_chunk):
        hid = jnp.dot(h, w1_ref[:, j * ck:(j + 1) * ck], preferred_element_type=F32)
        hid = jnp.square(jnp.maximum(hid, 0.0)).astype(BF16)
        acc = acc + jnp.dot(hid, w2_ref[j * ck:(j + 1) * ck, :], preferred_element_type=F32)
    return x1 + mod_ref[5] * acc


def _back_ab_kernel(n_chunk, x_ref, attn_ref, gated_ref, mod_ref, g_ref, wo_ref, w1_ref, w2_ref,
                    o_ref):
    aw = attn_ref.shape[1]
    mix = (jnp.dot(attn_ref[...], wo_ref[0:aw, :], preferred_element_type=F32)
           + jnp.dot(gated_ref[...], wo_ref[aw:, :], preferred_element_type=F32))
    x1 = x_ref[...] + mod_ref[2] * mix
    o_ref[...] = _ffn_residual(x1, mod_ref, g_ref[...], w1_ref, w2_ref, n_chunk)


def _mod_spec(mods, layer, n_rows_per_group, tm):
    d = mods.shape[-1]
    if mods.ndim == 5:
        nt = n_rows_per_group // tm
        return pl.BlockSpec((None, None, 6, 1, d), lambda r: (layer, r // nt, 0, 0, 0))
    return pl.BlockSpec((None, 6, tm, d), lambda r: (layer, 0, r, 0))


def _back_ab(x, attn, gated, mods, g, w_out, w1, w2, layer, rows_per_group, tm):
    n, d = x.shape
    aw = attn.shape[1]
    cw = gated.shape[1]
    row = lambda r: (r, 0)
    return pl.pallas_call(
        functools.partial(_back_ab_kernel, 4),
        grid=(n // tm,),
        in_specs=[
            pl.BlockSpec((tm, d), row),
            pl.BlockSpec((tm, aw), row),
            pl.BlockSpec((tm, cw), row),
            _mod_spec(mods, layer, rows_per_group, tm),
            _resident(g.shape),
            _resident(w_out.shape),
            _resident(w1.shape),
            _resident(w2.shape),
        ],
        out_specs=pl.BlockSpec((tm, d), row),
        out_shape=jax.ShapeDtypeStruct((n, d), F32),
        compiler_params=_params(("arbitrary",)),
        name="back_ab",
    )(x, attn, gated, mods, g, w_out, w1, w2)


def _gmlp_kernel(n_chunk, emit_v, x_ref, mod_ref, g1_ref, g2_ref, wi_ref, gv_ref, ws_ref, bs_ref,
                 wo_ref, w1_ref, w2_ref, gf_ref, *rest):
    if emit_v:
        o_ref, cv_ref, us_ref = rest
    else:
        o_ref, us_ref = rest
    tm = x_ref.shape[0]
    cwid = gv_ref.shape[1]
    n_grp, chunk, _ = ws_ref.shape
    grp = cwid // n_grp
    x = x_ref[...]
    hb = (_rms(x, g1_ref[...]) * (1.0 + mod_ref[1]) + mod_ref[0]).astype(BF16)
    u = jnp.dot(hb, wi_ref[:, 0:cwid], preferred_element_type=F32)
    v = _rms(jnp.dot(hb, wi_ref[:, cwid:], preferred_element_type=F32), gv_ref[...])
    if emit_v:
        cv_ref[...] = v
    vb = v.astype(BF16)
    for c in range(tm // chunk):
        rs = slice(c * chunk, (c + 1) * chunk)
        for gi in range(n_grp):
            ls = slice(gi * grp, (gi + 1) * grp)
            s = jnp.dot(ws_ref[gi], vb[rs, ls], preferred_element_type=F32) + bs_ref[gi]
            us_ref[rs, ls] = (u[rs, ls] * s).astype(BF16)
    mix = jnp.dot(us_ref[...], wo_ref[...], preferred_element_type=F32)
    x1 = x + mod_ref[2] * mix
    x2 = _ffn_residual(x1, mod_ref, g2_ref[...], w1_ref, w2_ref, n_chunk)
    o_ref[...] = _rms(x2, gf_ref[...])


def _gmlp_layer(x, mods, g1, g2, w_in, g_v, ws_mat, bs_full, w_out, w1, w2, g_final, layer,
                rows_per_group, tm, emit_v):
    n, d = x.shape
    cwid = g_v.shape[1]
    row = lambda r: (r, 0)
    out_shape = [jax.ShapeDtypeStruct((n, d), F32)]
    out_specs = [pl.BlockSpec((tm, d), row)]
    if emit_v:
        out_shape.append(jax.ShapeDtypeStruct((n, cwid), F32))
        out_specs.append(pl.BlockSpec((tm, cwid), row))
    return pl.pallas_call(
        functools.partial(_gmlp_kernel, 4, emit_v),
        grid=(n // tm,),
        in_specs=[
            pl.BlockSpec((tm, d), row),
            _mod_spec(mods, layer, rows_per_group, tm),
            _resident(g1.shape),
            _resident(g2.shape),
            _resident(w_in.shape),
            _resident(g_v.shape),
            _resident(ws_mat.shape),
            _resident(bs_full.shape),
            _resident(w_out.shape),
            _resident(w1.shape),
            _resident(w2.shape),
            _resident(g_final.shape),
        ],
        out_specs=out_specs,
        out_shape=out_shape,
        scratch_shapes=[pltpu.VMEM((tm, cwid), BF16)],
        compiler_params=_params(("arbitrary",)),
        name="gmlp_layer",
    )(x, mods, g1, g2, w_in, g_v, ws_mat, bs_full, w_out, w1, w2, g_final)


def _alibi_slopes(n_heads):
    return jnp.exp2(-8.0 * jnp.arange(1, n_heads + 1, dtype=F32) / n_heads)


def kernel(x_prompt, x_sample, cache_k, cache_v, state_conv, page_table, c_prompt, c_sample,
           norm_g, w_ada, b_ada, w_in_ab, conv_w, w_out_ab, w_in_c, g_v, w_s, b_s, w_out_c,
           w_ff1, w_ff2, g_final):
    batch, seq, d = x_prompt.shape
    bd, tq, _ = x_sample.shape
    n_ab, n_pool, page, n_heads, hd = cache_k.shape
    aw = n_heads * hd
    cw = state_conv.shape[-1]
    n_c, n_grp, chunk, _ = w_s.shape
    cwid = g_v.shape[-1]
    depth = norm_g.shape[0]
    assert depth == 2 and n_ab == 1 and n_c == 1 and hd * 2 == LANES
    assert tq <= chunk and chunk % tq == 0 and (bd * tq) % chunk == 0
    n_p, n_s = batch * seq, bd * tq
    tm = 512 if seq % 512 == 0 else chunk

    wb = lambda w: w.astype(BF16)
    w_ada_b, w_in_ab_b, w_out_ab_b = wb(w_ada), wb(w_in_ab[0]), wb(w_out_ab[0])
    w_in_c_b, w_out_c_b = wb(w_in_c[0]), wb(w_out_c[0])
    w_ff1_b, w_ff2_b = wb(w_ff1), wb(w_ff2)

    bp = -(-(batch + bd) // SUBLANES) * SUBLANES
    c_all = jnp.concatenate([c_prompt, c_sample, jnp.zeros((bp - batch - bd, d), F32)], axis=0)
    ada = _ada_terms(c_all, w_ada_b, b_ada)
    mods_p = ada[:, :batch].reshape(depth, batch, 6, 1, d)
    mods_s = jnp.repeat(ada[:, batch:batch + bd].reshape(depth, bd, 6, d), tq, axis=1)
    mods_s = mods_s.transpose(0, 2, 1, 3)

    slopes = _alibi_slopes(n_heads)
    slope_tab = jnp.broadcast_to(slopes[:, None], (n_heads, LANES))
    g = lambda l, j: norm_g[l, j].reshape(1, d)

    tril = jnp.tril(jnp.ones((chunk, chunk), bool))
    ws_p = jnp.where(tril, w_s[0], 0.0).astype(BF16)
    bs_p = jnp.broadcast_to(b_s[0][:, :, None], (n_grp, chunk, cwid // n_grp))
    ws_small = jnp.where(tril[:tq, :tq], w_s[0][:, :tq, :tq], 0.0)
    eye = jnp.eye(chunk // tq, dtype=F32)
    ws_s = jnp.einsum("ab,gts->gatbs", eye, ws_small).reshape(n_grp, chunk, chunk).astype(BF16)
    bs_s = jnp.broadcast_to(jnp.tile(b_s[0][:, :tq], (1, chunk // tq))[:, :, None],
                            (n_grp, chunk, cwid // n_grp))

    xp = x_prompt.reshape(n_p, d)
    prev0 = jnp.zeros((batch, CONV_W - 1, cw), F32)
    qp, kp_b, vp_b, kp_t, vp_t, km_p, gated_p, conv_p = _front_prompt(
        xp, mods_p, g(0, 0), w_in_ab_b, conv_w[0], prev0, 0, tm, n_heads, page)
    attn_p = _moba_prompt(qp, kp_b, vp_b, km_p.reshape(n_p // MOBA_BLOCK, aw), slopes, batch)
    xp = _back_ab(xp, attn_p, gated_p, mods_p, g(0, 1), w_out_ab_b, w_ff1_b[0], w_ff2_b[0], 0,
                  seq, tm)
    (yp,) = _gmlp_layer(xp, mods_p, g(1, 0), g(1, 1), w_in_c_b, g_v, ws_p, bs_p, w_out_c_b,
                        w_ff1_b[1], w_ff2_b[1], g_final.reshape(1, d), 1, seq, tm, False)

    pool_pages = lambda c: c.transpose(0, 1, 3, 4, 2).reshape(n_pool, aw, page)
    xs = x_sample.reshape(n_s, d)
    st = state_conv[0]
    pad = lambda a: jnp.concatenate([a, jnp.zeros((bd, tq - a.shape[1], cw), F32)], axis=1)
    pe1 = pad(st[:, 1:2]).reshape(n_s, cw)
    pe2 = pad(st).reshape(n_s, cw)
    qs, ks, vs, gated_s, u_s = _front_sample(xs, mods_s, g(0, 0), w_in_ab_b, conv_w[0], pe1, pe2,
                                             0, tq)
    attn_s = _moba_sample(qs.reshape(bd, tq, aw), ks.reshape(bd, tq, aw), vs.reshape(bd, tq, aw),
                          pool_pages(cache_k), pool_pages(cache_v), page_table, slope_tab,
                          n_heads)
    xs = _back_ab(xs, attn_s.reshape(n_s, aw), gated_s, mods_s, g(0, 1), w_out_ab_b, w_ff1_b[0],
                  w_ff2_b[0], 0, n_s, chunk)
    ys, cv_s = _gmlp_layer(xs, mods_s, g(1, 0), g(1, 1), w_in_c_b, g_v, ws_s, bs_s, w_out_c_b,
                           w_ff1_b[1], w_ff2_b[1], g_final.reshape(1, d), 1, n_s, chunk, True)

    return (
        yp.reshape(batch, seq, d),
        ys.reshape(bd, tq, d),
        kp_t.transpose(0, 3, 1, 2)[None],
        vp_t.transpose(0, 3, 1, 2)[None],
        conv_p[None],
        ks.reshape(1, bd, tq, n_heads, hd),
        vs.reshape(1, bd, tq, n_heads, hd),
        u_s.reshape(bd, tq, cw)[:, tq - (CONV_W - 1):][None],
        cv_s.reshape(1, bd, tq, cwid),
    )
```

```python
import functools

import jax
import jax.numpy as jnp
from jax import lax
from jax.experimental import pallas as pl
from jax.experimental.pallas import tpu as pltpu

F32 = jnp.float32
BF16 = jnp.bfloat16

EPS = 1e-6
MOBA_BLOCK = 256
MOBA_BLOCK_SHIFT = 8
MOBA_TOPK = 3
CONV_W = 3
LANES = 128
SUBLANES = 8
NEG_BIG = -(2.0 ** 100)
VMEM_LIMIT_BYTES = 56 * 1024 * 1024

_NT = (((1,), (1,)), ((), ()))


def _params(sem):
    return pltpu.CompilerParams(dimension_semantics=sem, vmem_limit_bytes=VMEM_LIMIT_BYTES)


def _resident(shape):
    n = len(shape)
    return pl.BlockSpec(shape, lambda *_: (0,) * n, pipeline_mode=pl.Buffered(1))


def _rms(x, g):
    return x * lax.rsqrt(jnp.mean(x * x, axis=-1, keepdims=True) + EPS) * g


def _split_bf16(x):
    hi = x.astype(BF16)
    lo = (x - hi.astype(F32)).astype(BF16)
    return hi, lo


def _ada_kernel(c_ref, w_ref, b_ref, o_ref):
    c = c_ref[...]
    s = (c * jax.nn.sigmoid(c)).astype(BF16)
    o_ref[0] = jnp.dot(s, w_ref[0], preferred_element_type=F32) + b_ref[0]


def _ada_terms(c_all, w_ada, b_ada):
    n_layers, d, d6 = w_ada.shape
    bp = c_all.shape[0]
    tn = d6 // 4
    return pl.pallas_call(
        _ada_kernel,
        grid=(n_layers, d6 // tn),
        in_specs=[
            pl.BlockSpec((bp, d), lambda l, j: (0, 0)),
            pl.BlockSpec((1, d, tn), lambda l, j: (l, 0, j)),
            pl.BlockSpec((1, 1, tn), lambda l, j: (l, 0, j)),
        ],
        out_specs=pl.BlockSpec((1, bp, tn), lambda l, j: (l, 0, j)),
        out_shape=jax.ShapeDtypeStruct((n_layers, bp, d6), F32),
        compiler_params=_params(("arbitrary", "arbitrary")),
        name="ada_terms",
    )(c_all, w_ada, b_ada.reshape(n_layers, 1, d6))


def _front_kernel(multi_seq, seq_len, *refs):
    if multi_seq:
        (x_ref, mod_ref, g_ref, w_ref, cw_ref, pe1_ref, pe2_ref,
         q_ref, k_ref, v_ref, gated_ref, u_ref) = refs
    else:
        (x_ref, mod_ref, g_ref, w_ref, cw_ref, prev_ref,
         q_ref, kb_ref, kt_ref, vt_ref, km_ref, gated_ref, cs_ref, carry_ref) = refs
    tm = x_ref.shape[0]
    cw = gated_ref.shape[1]
    h = _rms(x_ref[...], g_ref[...]) * (1.0 + mod_ref[1]) + mod_ref[0]
    hb = h.astype(BF16)

    def proj(j):
        return jnp.dot(hb, w_ref[:, j * cw:(j + 1) * cw], preferred_element_type=F32)

    q_ref[...] = proj(0)
    if multi_seq:
        k_ref[...] = proj(1)
        v_ref[...] = proj(2)
    else:
        n_page, n_heads, hd, page = kt_ref.shape
        yk = proj(1)
        kb_ref[...] = yk.astype(BF16)
        for j in range(tm // MOBA_BLOCK):
            km_ref[j] = jnp.mean(yk[j * MOBA_BLOCK:(j + 1) * MOBA_BLOCK, :], axis=0, keepdims=True)
        yv = proj(2)
        for y, t_ref in ((yk, kt_ref), (yv, vt_ref)):
            yt = y.T
            for j in range(n_page):
                t_ref[j] = yt[:, j * page:(j + 1) * page].reshape(n_heads, hd, page)
    u = proj(4) * proj(5)
    rows = lax.broadcasted_iota(jnp.int32, u.shape, 0)
    u1 = pltpu.roll(u, 1, axis=0)
    u2 = pltpu.roll(u, 2, axis=0)
    if multi_seq:
        assert seq_len & (seq_len - 1) == 0
        t = rows & (seq_len - 1)
        u1 = jnp.where(t == 0, pe1_ref[...], u1)
        u2 = jnp.where(t < 2, pe2_ref[...], u2)
        u_ref[...] = u
    else:
        i = pl.program_id(1)

        @pl.when(i == 0)
        def _():
            carry_ref[SUBLANES - 2:SUBLANES, :] = prev_ref[...]

        p0 = carry_ref[SUBLANES - 2:SUBLANES - 1, :]
        p1 = carry_ref[SUBLANES - 1:SUBLANES, :]
        u1 = jnp.where(rows == 0, p1, u1)
        u2 = jnp.where(rows == 0, p0, jnp.where(rows == 1, p1, u2))
        carry_ref[...] = u[tm - SUBLANES:tm, :]

        @pl.when(i == pl.num_programs(1) - 1)
        def _():
            cs_ref[...] = u[tm - 2:tm, :]

    yc = u2 * cw_ref[0:1, :] + u1 * cw_ref[1:2, :] + u * cw_ref[2:3, :]
    gated_ref[...] = (proj(3) * yc).astype(BF16)


def _front_prompt(x, mods, g, w_in, conv_w, prev, layer, tm, n_heads, page):
    n, d = x.shape
    nb = prev.shape[0]
    cw = prev.shape[2]
    nt = n // nb // tm
    assert tm % page == 0 and tm % MOBA_BLOCK == 0
    row = lambda b, i: (b * nt + i, 0)
    pages = (tm // page, n_heads, cw // n_heads, page)
    page_map = lambda b, i: (b * nt + i, 0, 0, 0)
    n_mean = tm // MOBA_BLOCK
    return pl.pallas_call(
        functools.partial(_front_kernel, False, 0),
        grid=(nb, nt),
        in_specs=[
            pl.BlockSpec((tm, d), row),
            pl.BlockSpec((None, None, 6, 1, d), lambda b, i: (layer, b, 0, 0, 0)),
            _resident(g.shape),
            _resident(w_in.shape),
            _resident(conv_w.shape),
            pl.BlockSpec((None, CONV_W - 1, cw), lambda b, i: (b, 0, 0)),
        ],
        out_specs=[
            pl.BlockSpec((tm, cw), row),
            pl.BlockSpec((tm, cw), row),
            pl.BlockSpec(pages, page_map),
            pl.BlockSpec(pages, page_map),
            pl.BlockSpec((n_mean, 1, cw), lambda b, i: (b * nt + i, 0, 0)),
            pl.BlockSpec((tm, cw), row),
            pl.BlockSpec((None, CONV_W - 1, cw), lambda b, i: (b, 0, 0)),
        ],
        out_shape=[
            jax.ShapeDtypeStruct((n, cw), F32),
            jax.ShapeDtypeStruct((n, cw), BF16),
            jax.ShapeDtypeStruct((n // page,) + pages[1:], F32),
            jax.ShapeDtypeStruct((n // page,) + pages[1:], F32),
            jax.ShapeDtypeStruct((n // MOBA_BLOCK, 1, cw), F32),
            jax.ShapeDtypeStruct((n, cw), BF16),
            jax.ShapeDtypeStruct(prev.shape, F32),
        ],
        scratch_shapes=[pltpu.VMEM((SUBLANES, cw), F32)],
        compiler_params=_params(("arbitrary", "arbitrary")),
        name="front_prompt",
    )(x, mods, g, w_in, conv_w, prev)


def _front_sample(x, mods_rows, g, w_in, conv_w, pe1, pe2, layer, seq_len):
    n, d = x.shape
    cw = pe1.shape[1]
    full = lambda i: (0, 0)
    return pl.pallas_call(
        functools.partial(_front_kernel, True, seq_len),
        grid=(1,),
        in_specs=[
            pl.BlockSpec((n, d), full),
            pl.BlockSpec((None, 6, n, d), lambda i: (layer, 0, 0, 0)),
            _resident(g.shape),
            _resident(w_in.shape),
            _resident(conv_w.shape),
            pl.BlockSpec((n, cw), full),
            pl.BlockSpec((n, cw), full),
        ],
        out_specs=[pl.BlockSpec((n, cw), full)] * 5,
        out_shape=[
            jax.ShapeDtypeStruct((n, cw), F32),
            jax.ShapeDtypeStruct((n, cw), F32),
            jax.ShapeDtypeStruct((n, cw), F32),
            jax.ShapeDtypeStruct((n, cw), BF16),
            jax.ShapeDtypeStruct((n, cw), F32),
        ],
        compiler_params=_params(("arbitrary",)),
        name="front_sample",
    )(x, mods_rows, g, w_in, conv_w, pe1, pe2)


def _moba_prompt_kernel(n_blk, slopes_ref, qa_ref, qb_ref, k_ref, v_ref, km_ref, o_ref,
                        kb_ref, vt_ref, qf_ref, st_ref, mx_ref, m_ref, l_ref, acc_ref):
    b, p, j = pl.program_id(0), pl.program_id(1), pl.program_id(2)
    blk = MOBA_BLOCK
    seq = k_ref.shape[0]
    hd = v_ref.shape[2]
    page = v_ref.shape[3]
    ppb = blk // page
    n_prev = n_blk - 1
    tiles = (j, n_blk - 1 - j)
    scale = float(hd) ** -0.5

    @pl.when((b == 0) & (p == 0) & (j == 0))
    def _():
        r = lax.broadcasted_iota(jnp.int32, (seq, LANES), 0)
        c = lax.broadcasted_iota(jnp.int32, (seq, LANES), 1)
        n = lax.shift_right_logical(r, MOBA_BLOCK_SHIFT)
        off = r & (blk - 1)
        feat = jnp.where(c == n, 1.0,
                         jnp.where(c == n_blk, n.astype(F32),
                                   jnp.where(c == n_blk + 1, off.astype(F32),
                                             jnp.where(c == n_blk + 2, 1.0, 0.0))))
        kb_ref[:, LANES:2 * LANES] = feat.astype(BF16)

    @pl.when(j == 0)
    def _():
        kb_ref[:, 0:LANES] = k_ref[...]
        for n in range(n_blk):
            for e in range(2):
                vt_ref[n, e] = jnp.concatenate(
                    [v_ref[n * ppb + g, e] for g in range(ppb)], axis=1).astype(BF16)

    km = km_ref[...]
    lane = lax.broadcasted_iota(jnp.int32, (1, LANES), 1)
    head_lanes = [(lane < hd).astype(F32), (lane >= hd).astype(F32)]
    qrow = lax.broadcasted_iota(jnp.int32, (LANES, blk), 0)
    head_rows = [(qrow < hd).astype(F32), (qrow >= hd).astype(F32)]
    blk_row = lax.broadcasted_iota(jnp.int32, (n_blk, blk), 0)
    q_refs = (qa_ref, qb_ref)

    for t in range(2):
        qt = q_refs[t][...].T
        qt_hi, qt_lo = _split_bf16(qt)
        tile_f = tiles[t].astype(F32)
        valid = blk_row < tiles[t]
        for e in range(2):
            slope = slopes_ref[2 * p + e]
            km_hi, km_lo = _split_bf16(km * head_lanes[e])
            gate_t = (jnp.dot(km_hi, qt_hi, preferred_element_type=F32)
                      + jnp.dot(km_lo, qt_hi, preferred_element_type=F32)
                      + jnp.dot(km_hi, qt_lo, preferred_element_type=F32))
            gate_t = jnp.where(valid, gate_t, -jnp.inf)
            rank = jnp.zeros((n_blk, blk), jnp.int32)
            for m in range(n_blk):
                gm = gate_t[m:m + 1, :]
                beats = jnp.where(gm > gate_t, 1,
                                  jnp.where(gm == gate_t, jnp.where(blk_row > m, 1, 0), 0))
                rank = rank + beats
            mask_t = jnp.where(valid, jnp.where(rank < MOBA_TOPK, 0.0, NEG_BIG), NEG_BIG)
            consts = jnp.where(qrow == n_blk, slope * blk,
                               jnp.where(qrow == n_blk + 1, slope,
                                         jnp.where(qrow == n_blk + 2, -slope * blk * tile_f, 0.0)))
            feat_t = jnp.concatenate([mask_t, jnp.zeros((LANES - n_blk, blk), F32)],
                                     axis=0) + consts
            q_e = (qt * (head_rows[e] * scale)).astype(BF16)
            qf_ref[t, :, e * blk:(e + 1) * blk] = jnp.concatenate(
                [q_e, feat_t.astype(BF16)], axis=0)
        mx_ref[t] = jnp.full(mx_ref.shape[1:], -jnp.inf, F32)
        l_ref[t] = jnp.zeros(l_ref.shape[1:], F32)
        acc_ref[t] = jnp.zeros(acc_ref.shape[1:], F32)

    n_grp = blk // SUBLANES
    wide = 2 * blk
    feat_lane = lax.broadcasted_iota(jnp.int32, (1, 2 * LANES), 1) - LANES
    ind_off = jnp.where((feat_lane >= 0) & (feat_lane < n_blk), 0.0, 1.0).astype(BF16)
    key_row = lax.broadcasted_iota(jnp.int32, (blk, wide), 0)
    q_col = lax.broadcasted_iota(jnp.int32, (blk, wide), 1) & (blk - 1)

    def unit_ids(u):
        is_a = u < j
        return jnp.where(is_a, 0, 1), jnp.where(is_a, u, u - j)

    def col_max(s):
        return jnp.max(s.reshape(n_grp, SUBLANES, wide), axis=0)

    def logits(t, n, own):
        kblk = kb_ref[pl.ds(pl.multiple_of(n * blk, blk), blk), :]
        if own:
            kblk = kblk * ind_off
        s = jnp.dot(kblk, qf_ref[t], preferred_element_type=F32)
        return jnp.where(key_row <= q_col, s, NEG_BIG) if own else s

    def accumulate(u, t, n):
        pr = jnp.exp(st_ref[u].reshape(n_grp, SUBLANES, wide) - m_ref[t][None])
        l_ref[t] = l_ref[t] + jnp.sum(pr, axis=0)
        pb = pr.reshape(blk, wide).astype(BF16)
        for e in range(2):
            acc_ref[t, e] = acc_ref[t, e] + jnp.dot(
                vt_ref[n, e], pb[:, e * blk:(e + 1) * blk], preferred_element_type=F32)

    for u in range(n_prev):
        t, n = unit_ids(u)
        s = logits(t, n, False)
        st_ref[u] = s
        mx_ref[t] = jnp.maximum(mx_ref[t], col_max(s))
    for t in range(2):
        s = logits(t, tiles[t], True)
        st_ref[n_prev + t] = s
        mx = jnp.maximum(mx_ref[t], col_max(s))
        m_ref[t] = jnp.broadcast_to(jnp.max(mx, axis=0, keepdims=True), mx.shape)

    for u in range(n_prev):
        t, n = unit_ids(u)
        accumulate(u, t, n)
    for t in range(2):
        accumulate(n_prev + t, t, tiles[t])
        l_sum = jnp.sum(l_ref[t], axis=0, keepdims=True)
        out_t = jnp.concatenate(
            [acc_ref[t, e] / l_sum[:, e * blk:(e + 1) * blk] for e in range(2)], axis=0)
        o_ref[pl.ds(pl.multiple_of(tiles[t] * blk, blk), blk), :] = out_t.T.astype(o_ref.dtype)


def _moba_prompt(q, k_rows, v_pages, k_mean, slopes, batch):
    n, aw = q.shape
    _, n_heads, hd, page = v_pages.shape
    seq = n // batch
    blk = MOBA_BLOCK
    n_blk = seq // blk
    assert seq % blk == 0 and blk % page == 0 and 2 * hd == LANES
    assert n_blk % SUBLANES == 0 and n_blk + 3 <= LANES
    n_pair = aw // LANES
    seq_map = lambda b, p, j, s: (b, p)
    return pl.pallas_call(
        functools.partial(_moba_prompt_kernel, n_blk),
        grid_spec=pltpu.PrefetchScalarGridSpec(
            num_scalar_prefetch=1,
            grid=(batch, n_pair, n_blk // 2),
            in_specs=[
                pl.BlockSpec((blk, LANES), lambda b, p, j, s: (b * n_blk + j, p)),
                pl.BlockSpec((blk, LANES), lambda b, p, j, s: (b * n_blk + n_blk - 1 - j, p)),
                pl.BlockSpec((seq, LANES), seq_map),
                pl.BlockSpec((seq // page, 2, hd, page), lambda b, p, j, s: (b, p, 0, 0)),
                pl.BlockSpec((n_blk, LANES), seq_map),
            ],
            out_specs=pl.BlockSpec((seq, LANES), seq_map),
            scratch_shapes=[
                pltpu.VMEM((seq, 2 * LANES), BF16),
                pltpu.VMEM((n_blk, 2, hd, blk), BF16),
                pltpu.VMEM((2, 2 * LANES, 2 * blk), BF16),
                pltpu.VMEM((n_blk + 1, blk, 2 * blk), F32),
                pltpu.VMEM((2, SUBLANES, 2 * blk), F32),
                pltpu.VMEM((2, SUBLANES, 2 * blk), F32),
                pltpu.VMEM((2, SUBLANES, 2 * blk), F32),
                pltpu.VMEM((2, 2, hd, blk), F32),
            ],
        ),
        out_shape=jax.ShapeDtypeStruct((n, aw), BF16),
        compiler_params=_params(("arbitrary", "arbitrary", "arbitrary")),
        name="moba_prompt",
    )(slopes, q, q, k_rows, v_pages, k_mean)


def _moba_sample_kernel(n_pages, n_heads, pt_ref, q_ref, kn_ref, vn_ref, slope_ref,
                        ck_hbm, cv_hbm, o_ref, kbuf, vbuf, s_ref, km_ref, ksem, vsem):
    b = pl.program_id(0)
    blk = MOBA_BLOCK
    tq, aw = q_ref.shape
    page = kbuf.shape[2]
    ppb = blk // page
    hd = aw // n_heads
    rows = tq * n_heads
    past = n_pages * page
    n_blk = past // blk

    def page_copy(src_hbm, buf, sem, sample, pg):
        return pltpu.make_async_copy(src_hbm.at[pt_ref[sample, pg]], buf.at[pg], sem)

    def for_pages(fn):
        def go(pg, c):
            fn(pg)
            return c
        lax.fori_loop(0, n_pages, go, 0)

    @pl.when(b == 0)
    def _():
        for_pages(lambda pg: page_copy(ck_hbm, kbuf, ksem, b, pg).start())

    for_pages(lambda pg: page_copy(cv_hbm, vbuf, vsem, b, pg).start())

    lane = lax.broadcasted_iota(jnp.int32, (n_heads, aw), 1)
    hrow = lax.broadcasted_iota(jnp.int32, (n_heads, aw), 0)
    head_mask = jnp.where(lane >= hrow * hd, jnp.where(lane < (hrow + 1) * hd, 1.0, 0.0), 0.0)
    hm_rows = jnp.concatenate([head_mask] * tq, axis=0)
    q = q_ref[...]
    q_rows = jnp.concatenate(
        [jnp.broadcast_to(q[t:t + 1, :], (n_heads, aw)) for t in range(tq)], axis=0) * hm_rows
    qs_bf = (q_rows * (float(hd) ** -0.5)).astype(BF16)
    slope = jnp.concatenate([slope_ref[:, 0:1]] * tq, axis=0)
    t_row = jnp.concatenate([jnp.full((n_heads, 1), t, jnp.int32) for t in range(tq)], axis=0)

    for_pages(lambda pg: page_copy(ck_hbm, kbuf, ksem, b, pg).wait())
    km_ref[...] = jnp.zeros_like(km_ref)
    for n in range(n_blk):
        ksum = kbuf[n * ppb]
        for j in range(1, ppb):
            ksum = ksum + kbuf[n * ppb + j]
        km_ref[:, n:n + 1] = jnp.sum(ksum, axis=1, keepdims=True) * (1.0 / blk)
    for pg in range(n_pages):
        s_ref[pg] = jnp.dot(qs_bf, kbuf[pg].astype(BF16), preferred_element_type=F32)

    @pl.when(b + 1 < pl.num_programs(0))
    def _():
        for_pages(lambda pg: page_copy(ck_hbm, kbuf, ksem, b + 1, pg).start())

    q_hi, q_lo = _split_bf16(q_rows)
    km_hi, km_lo = _split_bf16(km_ref[...])
    gate = (jnp.dot(q_hi, km_hi, preferred_element_type=F32)
            + jnp.dot(q_hi, km_lo, preferred_element_type=F32)
            + jnp.dot(q_lo, km_hi, preferred_element_type=F32))
    blk_lane = lax.broadcasted_iota(jnp.int32, gate.shape, 1)
    gate = jnp.where(blk_lane < n_blk, gate, -jnp.inf)
    rank = jnp.zeros(gate.shape, jnp.int32)
    for m in range(n_blk):
        gm = gate[:, m:m + 1]
        rank = rank + jnp.where(gm > gate, 1,
                                jnp.where(gm == gate, jnp.where(blk_lane > m, 1, 0), 0))
    sel_bias = jnp.where(blk_lane < n_blk,
                         jnp.where(rank < min(MOBA_TOPK, n_blk), 0.0, NEG_BIG), NEG_BIG)

    kn = kn_ref[...]
    vn = vn_ref[...]
    q_sc = q_rows * (float(hd) ** -0.5)
    own = []
    for t in range(tq):
        so = jnp.sum(q_sc * kn[t:t + 1, :], axis=1, keepdims=True)
        so = so - slope * (t_row - t).astype(F32)
        own.append(jnp.where(t_row >= t, so, NEG_BIG))
    m_run = functools.reduce(jnp.maximum, own)
    key_off = lax.broadcasted_iota(jnp.int32, (rows, page), 1)
    for pg in range(n_pages):
        dist = (past - pg * page) + t_row - key_off
        n = pg // ppb
        s = s_ref[pg] - slope * dist.astype(F32) + sel_bias[:, n:n + 1]
        s_ref[pg] = s
        m_run = jnp.maximum(m_run, jnp.max(s, axis=1, keepdims=True))

    acc = jnp.zeros((rows, aw), F32)
    l_run = jnp.zeros((rows, 1), F32)
    for t in range(tq):
        po = jnp.exp(own[t] - m_run)
        l_run = l_run + po
        acc = acc + po * vn[t:t + 1, :]
    for_pages(lambda pg: page_copy(cv_hbm, vbuf, vsem, b, pg).wait())
    for pg in range(n_pages):
        pr = jnp.exp(s_ref[pg] - m_run)
        l_run = l_run + jnp.sum(pr, axis=1, keepdims=True)
        acc = acc + lax.dot_general(pr.astype(BF16), vbuf[pg].astype(BF16), _NT,
                                    preferred_element_type=F32)
    out_rows = acc * hm_rows / l_run
    o_ref[...] = jnp.sum(out_rows.reshape(tq, n_heads, aw), axis=1).astype(o_ref.dtype)


def _moba_sample(q, k_new, v_new, cache_k, cache_v, page_table, slope_tab, n_heads):
    bd, tq, aw = q.shape
    n_pages = page_table.shape[1]
    page = cache_k.shape[2]
    past = n_pages * page
    assert MOBA_BLOCK % page == 0 and past % MOBA_BLOCK == 0 and past // MOBA_BLOCK <= LANES
    rows = tq * n_heads
    tok = lambda b, pt: (b, 0, 0)
    return pl.pallas_call(
        functools.partial(_moba_sample_kernel, n_pages, n_heads),
        grid_spec=pltpu.PrefetchScalarGridSpec(
            num_scalar_prefetch=1,
            grid=(bd,),
            in_specs=[
                pl.BlockSpec((None, tq, aw), tok),
                pl.BlockSpec((None, tq, aw), tok),
                pl.BlockSpec((None, tq, aw), tok),
                pl.BlockSpec(slope_tab.shape, lambda b, pt: (0, 0)),
                pl.BlockSpec(memory_space=pl.ANY),
                pl.BlockSpec(memory_space=pl.ANY),
            ],
            out_specs=pl.BlockSpec((None, tq, aw), tok),
            scratch_shapes=[
                pltpu.VMEM((n_pages, aw, page), F32),
                pltpu.VMEM((n_pages, aw, page), F32),
                pltpu.VMEM((n_pages, rows, page), F32),
                pltpu.VMEM((aw, LANES), F32),
                pltpu.SemaphoreType.DMA(()),
                pltpu.SemaphoreType.DMA(()),
            ],
        ),
        out_shape=jax.ShapeDtypeStruct((bd, tq, aw), BF16),
        compiler_params=_params(("arbitrary",)),
        name="moba_sample",
    )(page_table, q, k_new, v_new, slope_tab, cache_k, cache_v)


def _ffn_residual(x1, mod_ref, g, w1_ref, w2_ref, n_chunk):
    h = (_rms(x1, g) * (1.0 + mod_ref[4]) + mod_ref[3]).astype(BF16)
    ck = w1_ref.shape[1] // n_chunk
    acc = jnp.zeros(x1.shape, F32)
    for j in range(n_chunk):
        hid = jnp.dot(h, w1_ref[:, j * ck:(j + 1) * ck], preferred_element_type=F32)
        hid = jnp.square(jnp.maximum(hid, 0.0)).astype(BF16)
        acc = acc + jnp.dot(hid, w2_ref[j * ck:(j + 1) * ck, :], preferred_element_type=F32)
    return x1 + mod_ref[5] * acc


def _back_ab_kernel(n_chunk, x_ref, attn_ref, gated_ref, mod_ref, g_ref, wo_ref, w1_ref, w2_ref,
                    o_ref):
    aw = attn_ref.shape[1]
    mix = (jnp.dot(attn_ref[...], wo_ref[0:aw, :], preferred_element_type=F32)
           + jnp.dot(gated_ref[...], wo_ref[aw:, :], preferred_element_type=F32))
    x1 = x_ref[...] + mod_ref[2] * mix
    o_ref[...] = _ffn_residual(x1, mod_ref, g_ref[...], w1_ref, w2_ref, n_chunk)


def _mod_spec(mods, layer, n_rows_per_group, tm):
    d = mods.shape[-1]
    if mods.ndim == 5:
        nt = n_rows_per_group // tm
        return pl.BlockSpec((None, None, 6, 1, d), lambda r: (layer, r // nt, 0, 0, 0))
    return pl.BlockSpec((None, 6, tm, d), lambda r: (layer, 0, r, 0))


def _back_ab(x, attn, gated, mods, g, w_out, w1, w2, layer, rows_per_group, tm):
    n, d = x.shape
    aw = attn.shape[1]
    cw = gated.shape[1]
    row = lambda r: (r, 0)
    return pl.pallas_call(
        functools.partial(_back_ab_kernel, 4),
        grid=(n // tm,),
        in_specs=[
            pl.BlockSpec((tm, d), row),
            pl.BlockSpec((tm, aw), row),
            pl.BlockSpec((tm, cw), row),
            _mod_spec(mods, layer, rows_per_group, tm),
            _resident(g.shape),
            _resident(w_out.shape),
            _resident(w1.shape),
            _resident(w2.shape),
        ],
        out_specs=pl.BlockSpec((tm, d), row),
        out_shape=jax.ShapeDtypeStruct((n, d), F32),
        compiler_params=_params(("arbitrary",)),
        name="back_ab",
    )(x, attn, gated, mods, g, w_out, w1, w2)


def _gmlp_kernel(n_chunk, emit_v, x_ref, mod_ref, g1_ref, g2_ref, wi_ref, gv_ref, ws_ref, bs_ref,
                 wo_ref, w1_ref, w2_ref, gf_ref, *rest):
    if emit_v:
        o_ref, cv_ref, us_ref = rest
    else:
        o_ref, us_ref = rest
    tm = x_ref.shape[0]
    cwid = gv_ref.shape[1]
    n_grp, chunk, _ = ws_ref.shape
    grp = cwid // n_grp
    x = x_ref[...]
    hb = (_rms(x, g1_ref[...]) * (1.0 + mod_ref[1]) + mod_ref[0]).astype(BF16)
    u = jnp.dot(hb, wi_ref[:, 0:cwid], preferred_element_type=F32)
    v = _rms(jnp.dot(hb, wi_ref[:, cwid:], preferred_element_type=F32), gv_ref[...])
    if emit_v:
        cv_ref[...] = v
    vb = v.astype(BF16)
    for c in range(tm // chunk):
        rs = slice(c * chunk, (c + 1) * chunk)
        for gi in range(n_grp):
            ls = slice(gi * grp, (gi + 1) * grp)
            s = jnp.dot(ws_ref[gi], vb[rs, ls], preferred_element_type=F32) + bs_ref[gi]
            us_ref[rs, ls] = (u[rs, ls] * s).astype(BF16)
    mix = jnp.dot(us_ref[...], wo_ref[...], preferred_element_type=F32)
    x1 = x + mod_ref[2] * mix
    x2 = _ffn_residual(x1, mod_ref, g2_ref[...], w1_ref, w2_ref, n_chunk)
    o_ref[...] = _rms(x2, gf_ref[...])


def _gmlp_layer(x, mods, g1, g2, w_in, g_v, ws_mat, bs_full, w_out, w1, w2, g_final, layer,
                rows_per_group, tm, emit_v):
    n, d = x.shape
    cwid = g_v.shape[1]
    row = lambda r: (r, 0)
    out_shape = [jax.ShapeDtypeStruct((n, d), F32)]
    out_specs = [pl.BlockSpec((tm, d), row)]
    if emit_v:
        out_shape.append(jax.ShapeDtypeStruct((n, cwid), F32))
        out_specs.append(pl.BlockSpec((tm, cwid), row))
    return pl.pallas_call(
        functools.partial(_gmlp_kernel, 4, emit_v),
        grid=(n // tm,),
        in_specs=[
            pl.BlockSpec((tm, d), row),
            _mod_spec(mods, layer, rows_per_group, tm),
            _resident(g1.shape),
            _resident(g2.shape),
            _resident(w_in.shape),
            _resident(g_v.shape),
            _resident(ws_mat.shape),
            _resident(bs_full.shape),
            _resident(w_out.shape),
            _resident(w1.shape),
            _resident(w2.shape),
            _resident(g_final.shape),
        ],
        out_specs=out_specs,
        out_shape=out_shape,
        scratch_shapes=[pltpu.VMEM((tm, cwid), BF16)],
        compiler_params=_params(("arbitrary",)),
        name="gmlp_layer",
    )(x, mods, g1, g2, w_in, g_v, ws_mat, bs_full, w_out, w1, w2, g_final)


def _alibi_slopes(n_heads):
    return jnp.exp2(-8.0 * jnp.arange(1, n_heads + 1, dtype=F32) / n_heads)


def kernel(x_prompt, x_sample, cache_k, cache_v, state_conv, page_table, c_prompt, c_sample,
           norm_g, w_ada, b_ada, w_in_ab, conv_w, w_out_ab, w_in_c, g_v, w_s, b_s, w_out_c,
           w_ff1, w_ff2, g_final):
    batch, seq, d = x_prompt.shape
    bd, tq, _ = x_sample.shape
    n_ab, n_pool, page, n_heads, hd = cache_k.shape
    aw = n_heads * hd
    cw = state_conv.shape[-1]
    n_c, n_grp, chunk, _ = w_s.shape
    cwid = g_v.shape[-1]
    depth = norm_g.shape[0]
    assert depth == 2 and n_ab == 1 and n_c == 1 and hd * 2 == LANES
    assert tq <= chunk and chunk % tq == 0 and (bd * tq) % chunk == 0
    n_p, n_s = batch * seq, bd * tq
    tm = 512 if seq % 512 == 0 else chunk

    wb = lambda w: w.astype(BF16)
    w_ada_b, w_in_ab_b, w_out_ab_b = wb(w_ada), wb(w_in_ab[0]), wb(w_out_ab[0])
    w_in_c_b, w_out_c_b = wb(w_in_c[0]), wb(w_out_c[0])
    w_ff1_b, w_ff2_b = wb(w_ff1), wb(w_ff2)

    bp = -(-(batch + bd) // SUBLANES) * SUBLANES
    c_all = jnp.concatenate([c_prompt, c_sample, jnp.zeros((bp - batch - bd, d), F32)], axis=0)
    ada = _ada_terms(c_all, w_ada_b, b_ada)
    mods_p = ada[:, :batch].reshape(depth, batch, 6, 1, d)
    mods_s = jnp.repeat(ada[:, batch:batch + bd].reshape(depth, bd, 6, d), tq, axis=1)
    mods_s = mods_s.transpose(0, 2, 1, 3)

    slopes = _alibi_slopes(n_heads)
    slope_tab = jnp.broadcast_to(slopes[:, None], (n_heads, LANES))
    g = lambda l, j: norm_g[l, j].reshape(1, d)

    tril = jnp.tril(jnp.ones((chunk, chunk), bool))
    ws_p = jnp.where(tril, w_s[0], 0.0).astype(BF16)
    bs_p = jnp.broadcast_to(b_s[0][:, :, None], (n_grp, chunk, cwid // n_grp))
    ws_small = jnp.where(tril[:tq, :tq], w_s[0][:, :tq, :tq], 0.0)
    eye = jnp.eye(chunk // tq, dtype=F32)
    ws_s = jnp.einsum("ab,gts->gatbs", eye, ws_small).reshape(n_grp, chunk, chunk).astype(BF16)
    bs_s = jnp.broadcast_to(jnp.tile(b_s[0][:, :tq], (1, chunk // tq))[:, :, None],
                            (n_grp, chunk, cwid // n_grp))

    xp = x_prompt.reshape(n_p, d)
    prev0 = jnp.zeros((batch, CONV_W - 1, cw), F32)
    qp, kp_b, kp_t, vp_t, km_p, gated_p, conv_p = _front_prompt(
        xp, mods_p, g(0, 0), w_in_ab_b, conv_w[0], prev0, 0, tm, n_heads, page)
    attn_p = _moba_prompt(qp, kp_b, vp_t, km_p.reshape(n_p // MOBA_BLOCK, aw), slopes, batch)
    xp = _back_ab(xp, attn_p, gated_p, mods_p, g(0, 1), w_out_ab_b, w_ff1_b[0], w_ff2_b[0], 0,
                  seq, tm)
    (yp,) = _gmlp_layer(xp, mods_p, g(1, 0), g(1, 1), w_in_c_b, g_v, ws_p, bs_p, w_out_c_b,
                        w_ff1_b[1], w_ff2_b[1], g_final.reshape(1, d), 1, seq, tm, False)

    pool_pages = lambda c: c.transpose(0, 1, 3, 4, 2).reshape(n_pool, aw, page)
    xs = x_sample.reshape(n_s, d)
    st = state_conv[0]
    pad = lambda a: jnp.concatenate([a, jnp.zeros((bd, tq - a.shape[1], cw), F32)], axis=1)
    pe1 = pad(st[:, 1:2]).reshape(n_s, cw)
    pe2 = pad(st).reshape(n_s, cw)
    qs, ks, vs, gated_s, u_s = _front_sample(xs, mods_s, g(0, 0), w_in_ab_b, conv_w[0], pe1, pe2,
                                             0, tq)
    attn_s = _moba_sample(qs.reshape(bd, tq, aw), ks.reshape(bd, tq, aw), vs.reshape(bd, tq, aw),
                          pool_pages(cache_k), pool_pages(cache_v), page_table, slope_tab,
                          n_heads)
    xs = _back_ab(xs, attn_s.reshape(n_s, aw), gated_s, mods_s, g(0, 1), w_out_ab_b, w_ff1_b[0],
                  w_ff2_b[0], 0, n_s, chunk)
    ys, cv_s = _gmlp_layer(xs, mods_s, g(1, 0), g(1, 1), w_in_c_b, g_v, ws_s, bs_s, w_out_c_b,
                           w_ff1_b[1], w_ff2_b[1], g_final.reshape(1, d), 1, n_s, chunk, True)

    return (
        yp.reshape(batch, seq, d),
        ys.reshape(bd, tq, d),
        kp_t.transpose(0, 3, 1, 2)[None],
        vp_t.transpose(0, 3, 1, 2)[None],
        conv_p[None],
        ks.reshape(1, bd, tq, n_heads, hd),
        vs.reshape(1, bd, tq, n_heads, hd),
        u_s.reshape(bd, tq, cw)[:, tq - (CONV_W - 1):][None],
        cv_s.reshape(1, bd, tq, cwid),
    )
```

```python
import functools

import jax
import jax.numpy as jnp
from jax import lax
from jax.experimental import pallas as pl
from jax.experimental.pallas import tpu as pltpu

F32 = jnp.float32
BF16 = jnp.bfloat16

EPS = 1e-6
MOBA_BLOCK = 256
MOBA_BLOCK_SHIFT = 8
MOBA_TOPK = 3
CONV_W = 3
LANES = 128
SUBLANES = 8
NEG_BIG = -(2.0 ** 100)
VMEM_LIMIT_BYTES = 56 * 1024 * 1024

_NT = (((1,), (1,)), ((), ()))


def _params(sem):
    return pltpu.CompilerParams(dimension_semantics=sem, vmem_limit_bytes=VMEM_LIMIT_BYTES)


def _resident(shape):
    n = len(shape)
    return pl.BlockSpec(shape, lambda *_: (0,) * n, pipeline_mode=pl.Buffered(1))


def _rms(x, g):
    return x * lax.rsqrt(jnp.mean(x * x, axis=-1, keepdims=True) + EPS) * g


def _split_bf16(x):
    hi = x.astype(BF16)
    lo = (x - hi.astype(F32)).astype(BF16)
    return hi, lo


def _ada_kernel(c_ref, w_ref, b_ref, o_ref):
    c = c_ref[...]
    s = (c * jax.nn.sigmoid(c)).astype(BF16)
    o_ref[0] = jnp.dot(s, w_ref[0].astype(BF16), preferred_element_type=F32) + b_ref[0]


def _ada_terms(c_all, w_ada, b_ada):
    n_layers, d, d6 = w_ada.shape
    bp = c_all.shape[0]
    tn = d6 // 4
    return pl.pallas_call(
        _ada_kernel,
        grid=(n_layers, d6 // tn),
        in_specs=[
            pl.BlockSpec((bp, d), lambda l, j: (0, 0)),
            pl.BlockSpec((1, d, tn), lambda l, j: (l, 0, j)),
            pl.BlockSpec((1, 1, tn), lambda l, j: (l, 0, j)),
        ],
        out_specs=pl.BlockSpec((1, bp, tn), lambda l, j: (l, 0, j)),
        out_shape=jax.ShapeDtypeStruct((n_layers, bp, d6), F32),
        compiler_params=_params(("arbitrary", "arbitrary")),
        name="ada_terms",
    )(c_all, w_ada, b_ada.reshape(n_layers, 1, d6))


def _front_kernel(multi_seq, seq_len, *refs):
    if multi_seq:
        (x_ref, mod_ref, g_ref, w_ref, cw_ref, pe1_ref, pe2_ref,
         q_ref, k_ref, v_ref, gated_ref, u_ref) = refs
    else:
        (x_ref, mod_ref, g_ref, w_ref, cw_ref, prev_ref,
         q_ref, kb_ref, kt_ref, vt_ref, km_ref, gated_ref, cs_ref, carry_ref) = refs
    tm = x_ref.shape[0]
    cw = gated_ref.shape[1]
    h = _rms(x_ref[...], g_ref[...]) * (1.0 + mod_ref[1]) + mod_ref[0]
    hb = h.astype(BF16)

    def proj(j):
        return jnp.dot(hb, w_ref[:, j * cw:(j + 1) * cw], preferred_element_type=F32)

    q_ref[...] = proj(0)
    if multi_seq:
        k_ref[...] = proj(1)
        v_ref[...] = proj(2)
    else:
        n_page, n_heads, hd, page = kt_ref.shape
        yk = proj(1)
        kb_ref[...] = yk.astype(BF16)
        for j in range(tm // MOBA_BLOCK):
            km_ref[j] = jnp.mean(yk[j * MOBA_BLOCK:(j + 1) * MOBA_BLOCK, :], axis=0, keepdims=True)
        yv = proj(2)
        for y, t_ref in ((yk, kt_ref), (yv, vt_ref)):
            yt = y.T
            for j in range(n_page):
                t_ref[j] = yt[:, j * page:(j + 1) * page].reshape(n_heads, hd, page)
    u = proj(4) * proj(5)
    rows = lax.broadcasted_iota(jnp.int32, u.shape, 0)
    u1 = pltpu.roll(u, 1, axis=0)
    u2 = pltpu.roll(u, 2, axis=0)
    if multi_seq:
        assert seq_len & (seq_len - 1) == 0
        t = rows & (seq_len - 1)
        u1 = jnp.where(t == 0, pe1_ref[...], u1)
        u2 = jnp.where(t < 2, pe2_ref[...], u2)
        u_ref[...] = u
    else:
        i = pl.program_id(1)

        @pl.when(i == 0)
        def _():
            carry_ref[SUBLANES - 2:SUBLANES, :] = prev_ref[...]

        p0 = carry_ref[SUBLANES - 2:SUBLANES - 1, :]
        p1 = carry_ref[SUBLANES - 1:SUBLANES, :]
        u1 = jnp.where(rows == 0, p1, u1)
        u2 = jnp.where(rows == 0, p0, jnp.where(rows == 1, p1, u2))
        carry_ref[...] = u[tm - SUBLANES:tm, :]

        @pl.when(i == pl.num_programs(1) - 1)
        def _():
            cs_ref[...] = u[tm - 2:tm, :]

    yc = u2 * cw_ref[0:1, :] + u1 * cw_ref[1:2, :] + u * cw_ref[2:3, :]
    gated_ref[...] = (proj(3) * yc).astype(BF16)


def _front_prompt(x, mods, g, w_in, conv_w, prev, layer, tm, n_heads, page):
    n, d = x.shape
    nb = prev.shape[0]
    cw = prev.shape[2]
    nt = n // nb // tm
    assert tm % page == 0 and tm % MOBA_BLOCK == 0
    row = lambda b, i: (b * nt + i, 0)
    pages = (tm // page, n_heads, cw // n_heads, page)
    page_map = lambda b, i: (b * nt + i, 0, 0, 0)
    n_mean = tm // MOBA_BLOCK
    return pl.pallas_call(
        functools.partial(_front_kernel, False, 0),
        grid=(nb, nt),
        in_specs=[
            pl.BlockSpec((tm, d), row),
            pl.BlockSpec((None, None, 6, 1, d), lambda b, i: (layer, b, 0, 0, 0)),
            _resident(g.shape),
            _resident(w_in.shape),
            _resident(conv_w.shape),
            pl.BlockSpec((None, CONV_W - 1, cw), lambda b, i: (b, 0, 0)),
        ],
        out_specs=[
            pl.BlockSpec((tm, cw), row),
            pl.BlockSpec((tm, cw), row),
            pl.BlockSpec(pages, page_map),
            pl.BlockSpec(pages, page_map),
            pl.BlockSpec((n_mean, 1, cw), lambda b, i: (b * nt + i, 0, 0)),
            pl.BlockSpec((tm, cw), row),
            pl.BlockSpec((None, CONV_W - 1, cw), lambda b, i: (b, 0, 0)),
        ],
        out_shape=[
            jax.ShapeDtypeStruct((n, cw), F32),
            jax.ShapeDtypeStruct((n, cw), BF16),
            jax.ShapeDtypeStruct((n // page,) + pages[1:], F32),
            jax.ShapeDtypeStruct((n // page,) + pages[1:], F32),
            jax.ShapeDtypeStruct((n // MOBA_BLOCK, 1, cw), F32),
            jax.ShapeDtypeStruct((n, cw), BF16),
            jax.ShapeDtypeStruct(prev.shape, F32),
        ],
        scratch_shapes=[pltpu.VMEM((SUBLANES, cw), F32)],
        compiler_params=_params(("arbitrary", "arbitrary")),
        name="front_prompt",
    )(x, mods, g, w_in, conv_w, prev)


def _front_sample(x, mods_rows, g, w_in, conv_w, pe1, pe2, layer, seq_len):
    n, d = x.shape
    cw = pe1.shape[1]
    full = lambda i: (0, 0)
    return pl.pallas_call(
        functools.partial(_front_kernel, True, seq_len),
        grid=(1,),
        in_specs=[
            pl.BlockSpec((n, d), full),
            pl.BlockSpec((None, 6, n, d), lambda i: (layer, 0, 0, 0)),
            _resident(g.shape),
            _resident(w_in.shape),
            _resident(conv_w.shape),
            pl.BlockSpec((n, cw), full),
            pl.BlockSpec((n, cw), full),
        ],
        out_specs=[pl.BlockSpec((n, cw), full)] * 5,
        out_shape=[
            jax.ShapeDtypeStruct((n, cw), F32),
            jax.ShapeDtypeStruct((n, cw), F32),
            jax.ShapeDtypeStruct((n, cw), F32),
            jax.ShapeDtypeStruct((n, cw), BF16),
            jax.ShapeDtypeStruct((n, cw), F32),
        ],
        compiler_params=_params(("arbitrary",)),
        name="front_sample",
    )(x, mods_rows, g, w_in, conv_w, pe1, pe2)


def _moba_item(step, n_items, n_half, n_pair):
    item = jnp.minimum(step, n_items - 1)
    seq_pair = item // n_half
    return seq_pair // n_pair, seq_pair % n_pair, item % n_half


def _moba_prompt_kernel(n_blk, n_pair, slopes_ref, qa_ref, qb_ref, k_ref, v_ref, km_ref, o_ref,
                        kb_ref, vt_ref, qf_ref, st_ref, um_ref, ul_ref, m_ref, acc_ref):
    step = pl.program_id(0)
    n_items = pl.num_programs(0) - 1
    n_half = n_blk // 2
    blk = MOBA_BLOCK
    seq = k_ref.shape[0]
    hd = v_ref.shape[2]
    page = v_ref.shape[3]
    ppb = blk // page
    n_prev = n_blk - 1
    scale = float(hd) ** -0.5
    b, p, j = _moba_item(step, n_items, n_half, n_pair)
    b_old, p_old, j_old = _moba_item(jnp.maximum(step - 1, 0), n_items, n_half, n_pair)
    slot = (b * n_pair + p) % 2
    slot_old = (b_old * n_pair + p_old) % 2
    tiles = (j, n_blk - 1 - j)
    tiles_old = (j_old, n_blk - 1 - j_old)

    @pl.when(step == 0)
    def _():
        r = lax.broadcasted_iota(jnp.int32, (seq, LANES), 0)
        c = lax.broadcasted_iota(jnp.int32, (seq, LANES), 1)
        n = lax.shift_right_logical(r, MOBA_BLOCK_SHIFT)
        off = r & (blk - 1)
        feat = jnp.where(c == n, 1.0,
                         jnp.where(c == n_blk, n.astype(F32),
                                   jnp.where(c == n_blk + 1, off.astype(F32),
                                             jnp.where(c == n_blk + 2, 1.0, 0.0))))
        kb_ref[:, LANES:2 * LANES] = feat.astype(BF16)
        st_ref[...] = jnp.zeros_like(st_ref)
        m_ref[...] = jnp.zeros_like(m_ref)

    @pl.when((j == 0) & (step < n_items))
    def _():
        kb_ref[:, 0:LANES] = k_ref[...]
        for n in range(n_blk):
            for e in range(2):
                vt_ref[slot, n, e] = jnp.concatenate(
                    [v_ref[n * ppb + g, e] for g in range(ppb)], axis=1).astype(BF16)

    km = km_ref[...]
    lane = lax.broadcasted_iota(jnp.int32, (1, LANES), 1)
    head_lanes = [(lane < hd).astype(F32), (lane >= hd).astype(F32)]
    qrow = lax.broadcasted_iota(jnp.int32, (LANES, blk), 0)
    head_rows = [(qrow < hd).astype(F32), (qrow >= hd).astype(F32)]
    blk_row = lax.broadcasted_iota(jnp.int32, (n_blk, blk), 0)
    q_refs = (qa_ref, qb_ref)

    for t in range(2):
        qt = q_refs[t][...].T
        qt_hi, qt_lo = _split_bf16(qt)
        tile_f = tiles[t].astype(F32)
        valid = blk_row < tiles[t]
        for e in range(2):
            slope = slopes_ref[2 * p + e]
            km_hi, km_lo = _split_bf16(km * head_lanes[e])
            gate_t = (jnp.dot(km_hi, qt_hi, preferred_element_type=F32)
                      + jnp.dot(km_lo, qt_hi, preferred_element_type=F32)
                      + jnp.dot(km_hi, qt_lo, preferred_element_type=F32))
            gate_t = jnp.where(valid, gate_t, -jnp.inf)
            rank = jnp.zeros((n_blk, blk), jnp.int32)
            for m in range(n_blk):
                gm = gate_t[m:m + 1, :]
                beats = jnp.where(gm > gate_t, 1,
                                  jnp.where(gm == gate_t, jnp.where(blk_row > m, 1, 0), 0))
                rank = rank + beats
            mask_t = jnp.where(valid, jnp.where(rank < MOBA_TOPK, 0.0, NEG_BIG), NEG_BIG)
            consts = jnp.where(qrow == n_blk, slope * blk,
                               jnp.where(qrow == n_blk + 1, slope,
                                         jnp.where(qrow == n_blk + 2, -slope * blk * tile_f, 0.0)))
            feat_t = jnp.concatenate([mask_t, jnp.zeros((LANES - n_blk, blk), F32)],
                                     axis=0) + consts
            q_e = (qt * (head_rows[e] * scale)).astype(BF16)
            qf_ref[t, :, e * blk:(e + 1) * blk] = jnp.concatenate(
                [q_e, feat_t.astype(BF16)], axis=0)

    n_grp = blk // SUBLANES
    wide = 2 * blk
    feat_lane = lax.broadcasted_iota(jnp.int32, (1, 2 * LANES), 1) - LANES
    ind_off = jnp.where((feat_lane >= 0) & (feat_lane < n_blk), 0.0, 1.0).astype(BF16)
    key_row = lax.broadcasted_iota(jnp.int32, (blk, wide), 0)
    q_col = lax.broadcasted_iota(jnp.int32, (blk, wide), 1) & (blk - 1)

    def unit_ids(u, jj, item_tiles):
        if u >= n_prev:
            return u - n_prev, item_tiles[u - n_prev]
        is_a = u < jj
        return jnp.where(is_a, 0, 1), jnp.where(is_a, u, u - jj)

    def per_tile(per_unit_ref, jj, combine, neutral):
        out = [per_unit_ref[n_prev], per_unit_ref[n_prev + 1]]
        for u in range(n_prev):
            val = per_unit_ref[u]
            is_a = u < jj
            out[0] = combine(out[0], jnp.where(is_a, val, neutral))
            out[1] = combine(out[1], jnp.where(is_a, neutral, val))
        return out

    acc_ref[...] = jnp.zeros_like(acc_ref)
    for u in range(n_prev + 2):
        t_old, n_old = unit_ids(u, j_old, tiles_old)
        pr = jnp.exp(st_ref[u].reshape(n_grp, SUBLANES, wide) - m_ref[t_old][None])
        ul_ref[u] = jnp.sum(pr, axis=0)
        pb = pr.reshape(blk, wide).astype(BF16)
        for e in range(2):
            acc_ref[t_old, e] = acc_ref[t_old, e] + jnp.dot(
                vt_ref[slot_old, n_old, e], pb[:, e * blk:(e + 1) * blk],
                preferred_element_type=F32)
        t_new, n_new = unit_ids(u, j, tiles)
        own = u >= n_prev
        kblk = kb_ref[pl.ds(pl.multiple_of(n_new * blk, blk), blk), :]
        if own:
            kblk = kblk * ind_off
        s = jnp.dot(kblk, qf_ref[t_new], preferred_element_type=F32)
        if own:
            s = jnp.where(key_row <= q_col, s, NEG_BIG)
        st_ref[u] = s
        um_ref[u] = jnp.max(s.reshape(n_grp, SUBLANES, wide), axis=0)

    l_old = per_tile(ul_ref, j_old, jnp.add, 0.0)
    for t in range(2):
        l_sum = jnp.sum(l_old[t], axis=0, keepdims=True)
        out_t = jnp.concatenate(
            [acc_ref[t, e] / l_sum[:, e * blk:(e + 1) * blk] for e in range(2)], axis=0)
        o_ref[pl.ds(pl.multiple_of(tiles_old[t] * blk, blk), blk), :] = (
            out_t.T.astype(o_ref.dtype))
    m_new = per_tile(um_ref, j, jnp.maximum, -jnp.inf)
    for t in range(2):
        m_ref[t] = jnp.broadcast_to(jnp.max(m_new[t], axis=0, keepdims=True), m_new[t].shape)


def _moba_prompt(q, k_rows, v_pages, k_mean, slopes, batch):
    n, aw = q.shape
    _, n_heads, hd, page = v_pages.shape
    seq = n // batch
    blk = MOBA_BLOCK
    n_blk = seq // blk
    assert seq % blk == 0 and blk % page == 0 and 2 * hd == LANES
    assert n_blk % SUBLANES == 0 and n_blk + 3 <= LANES
    n_pair = aw // LANES
    n_half = n_blk // 2
    n_items = batch * n_pair * n_half
    n_unit = n_blk + 1

    def item_map(fn):
        return lambda s, _: fn(*_moba_item(s, n_items, n_half, n_pair))

    def old_item_map(fn):
        return lambda s, _: fn(*_moba_item(jnp.maximum(s - 1, 0), n_items, n_half, n_pair))

    return pl.pallas_call(
        functools.partial(_moba_prompt_kernel, n_blk, n_pair),
        grid_spec=pltpu.PrefetchScalarGridSpec(
            num_scalar_prefetch=1,
            grid=(n_items + 1,),
            in_specs=[
                pl.BlockSpec((blk, LANES), item_map(lambda b, p, j: (b * n_blk + j, p))),
                pl.BlockSpec((blk, LANES),
                             item_map(lambda b, p, j: (b * n_blk + n_blk - 1 - j, p))),
                pl.BlockSpec((seq, LANES), item_map(lambda b, p, j: (b, p))),
                pl.BlockSpec((seq // page, 2, hd, page), item_map(lambda b, p, j: (b, p, 0, 0))),
                pl.BlockSpec((n_blk, LANES), item_map(lambda b, p, j: (b, p))),
            ],
            out_specs=pl.BlockSpec((seq, LANES), old_item_map(lambda b, p, j: (b, p))),
            scratch_shapes=[
                pltpu.VMEM((seq, 2 * LANES), BF16),
                pltpu.VMEM((2, n_blk, 2, hd, blk), BF16),
                pltpu.VMEM((2, 2 * LANES, 2 * blk), BF16),
                pltpu.VMEM((n_unit, blk, 2 * blk), F32),
                pltpu.VMEM((n_unit, SUBLANES, 2 * blk), F32),
                pltpu.VMEM((n_unit, SUBLANES, 2 * blk), F32),
                pltpu.VMEM((2, SUBLANES, 2 * blk), F32),
                pltpu.VMEM((2, 2, hd, blk), F32),
            ],
        ),
        out_shape=jax.ShapeDtypeStruct((n, aw), BF16),
        compiler_params=_params(("arbitrary",)),
        name="moba_prompt",
    )(slopes, q, q, k_rows, v_pages, k_mean)


def _moba_sample_kernel(n_pages, n_heads, pt_ref, q_ref, kn_ref, vn_ref, slope_ref,
                        ck_hbm, cv_hbm, o_ref, kbuf, vbuf, s_ref, km_ref, ksem, vsem):
    b = pl.program_id(0)
    blk = MOBA_BLOCK
    tq, aw = q_ref.shape
    page = kbuf.shape[2]
    ppb = blk // page
    hd = aw // n_heads
    rows = tq * n_heads
    past = n_pages * page
    n_blk = past // blk

    def page_copy(src_hbm, buf, sem, sample, pg):
        return pltpu.make_async_copy(src_hbm.at[pt_ref[sample, pg]], buf.at[pg], sem)

    def for_pages(fn):
        def go(pg, c):
            fn(pg)
            return c
        lax.fori_loop(0, n_pages, go, 0)

    @pl.when(b == 0)
    def _():
        for_pages(lambda pg: page_copy(ck_hbm, kbuf, ksem, b, pg).start())

    for_pages(lambda pg: page_copy(cv_hbm, vbuf, vsem, b, pg).start())

    lane = lax.broadcasted_iota(jnp.int32, (n_heads, aw), 1)
    hrow = lax.broadcasted_iota(jnp.int32, (n_heads, aw), 0)
    head_mask = jnp.where(lane >= hrow * hd, jnp.where(lane < (hrow + 1) * hd, 1.0, 0.0), 0.0)
    hm_rows = jnp.concatenate([head_mask] * tq, axis=0)
    q = q_ref[...]
    q_rows = jnp.concatenate(
        [jnp.broadcast_to(q[t:t + 1, :], (n_heads, aw)) for t in range(tq)], axis=0) * hm_rows
    qs_bf = (q_rows * (float(hd) ** -0.5)).astype(BF16)
    slope = jnp.concatenate([slope_ref[:, 0:1]] * tq, axis=0)
    t_row = jnp.concatenate([jnp.full((n_heads, 1), t, jnp.int32) for t in range(tq)], axis=0)

    for_pages(lambda pg: page_copy(ck_hbm, kbuf, ksem, b, pg).wait())
    km_ref[...] = jnp.zeros_like(km_ref)
    for n in range(n_blk):
        ksum = kbuf[n * ppb]
        for j in range(1, ppb):
            ksum = ksum + kbuf[n * ppb + j]
        km_ref[:, n:n + 1] = jnp.sum(ksum, axis=1, keepdims=True) * (1.0 / blk)
    for pg in range(n_pages):
        s_ref[pg] = jnp.dot(qs_bf, kbuf[pg].astype(BF16), preferred_element_type=F32)

    @pl.when(b + 1 < pl.num_programs(0))
    def _():
        for_pages(lambda pg: page_copy(ck_hbm, kbuf, ksem, b + 1, pg).start())

    q_hi, q_lo = _split_bf16(q_rows)
    km_hi, km_lo = _split_bf16(km_ref[...])
    gate = (jnp.dot(q_hi, km_hi, preferred_element_type=F32)
            + jnp.dot(q_hi, km_lo, preferred_element_type=F32)
            + jnp.dot(q_lo, km_hi, preferred_element_type=F32))
    blk_lane = lax.broadcasted_iota(jnp.int32, gate.shape, 1)
    gate = jnp.where(blk_lane < n_blk, gate, -jnp.inf)
    rank = jnp.zeros(gate.shape, jnp.int32)
    for m in range(n_blk):
        gm = gate[:, m:m + 1]
        rank = rank + jnp.where(gm > gate, 1,
                                jnp.where(gm == gate, jnp.where(blk_lane > m, 1, 0), 0))
    sel_bias = jnp.where(blk_lane < n_blk,
                         jnp.where(rank < min(MOBA_TOPK, n_blk), 0.0, NEG_BIG), NEG_BIG)

    kn = kn_ref[...]
    vn = vn_ref[...]
    q_sc = q_rows * (float(hd) ** -0.5)
    own = []
    for t in range(tq):
        so = jnp.sum(q_sc * kn[t:t + 1, :], axis=1, keepdims=True)
        so = so - slope * (t_row - t).astype(F32)
        own.append(jnp.where(t_row >= t, so, NEG_BIG))
    m_run = functools.reduce(jnp.maximum, own)
    key_off = lax.broadcasted_iota(jnp.int32, (rows, page), 1)
    for pg in range(n_pages):
        dist = (past - pg * page) + t_row - key_off
        n = pg // ppb
        s = s_ref[pg] - slope * dist.astype(F32) + sel_bias[:, n:n + 1]
        s_ref[pg] = s
        m_run = jnp.maximum(m_run, jnp.max(s, axis=1, keepdims=True))

    acc = jnp.zeros((rows, aw), F32)
    l_run = jnp.zeros((rows, 1), F32)
    for t in range(tq):
        po = jnp.exp(own[t] - m_run)
        l_run = l_run + po
        acc = acc + po * vn[t:t + 1, :]
    for_pages(lambda pg: page_copy(cv_hbm, vbuf, vsem, b, pg).wait())
    for pg in range(n_pages):
        pr = jnp.exp(s_ref[pg] - m_run)
        l_run = l_run + jnp.sum(pr, axis=1, keepdims=True)
        acc = acc + lax.dot_general(pr.astype(BF16), vbuf[pg].astype(BF16), _NT,
                                    preferred_element_type=F32)
    out_rows = acc * hm_rows / l_run
    o_ref[...] = jnp.sum(out_rows.reshape(tq, n_heads, aw), axis=1).astype(o_ref.dtype)


def _moba_sample(q, k_new, v_new, cache_k, cache_v, page_table, slope_tab, n_heads):
    bd, tq, aw = q.shape
    n_pages = page_table.shape[1]
    page = cache_k.shape[2]
    past = n_pages * page
    assert MOBA_BLOCK % page == 0 and past % MOBA_BLOCK == 0 and past // MOBA_BLOCK <= LANES
    rows = tq * n_heads
    tok = lambda b, pt: (b, 0, 0)
    return pl.pallas_call(
        functools.partial(_moba_sample_kernel, n_pages, n_heads),
        grid_spec=pltpu.PrefetchScalarGridSpec(
            num_scalar_prefetch=1,
            grid=(bd,),
            in_specs=[
                pl.BlockSpec((None, tq, aw), tok),
                pl.BlockSpec((None, tq, aw), tok),
                pl.BlockSpec((None, tq, aw), tok),
                pl.BlockSpec(slope_tab.shape, lambda b, pt: (0, 0)),
                pl.BlockSpec(memory_space=pl.ANY),
                pl.BlockSpec(memory_space=pl.ANY),
            ],
            out_specs=pl.BlockSpec((None, tq, aw), tok),
            scratch_shapes=[
                pltpu.VMEM((n_pages, aw, page), F32),
                pltpu.VMEM((n_pages, aw, page), F32),
                pltpu.VMEM((n_pages, rows, page), F32),
                pltpu.VMEM((aw, LANES), F32),
                pltpu.SemaphoreType.DMA(()),
                pltpu.SemaphoreType.DMA(()),
            ],
        ),
        out_shape=jax.ShapeDtypeStruct((bd, tq, aw), BF16),
        compiler_params=_params(("arbitrary",)),
        name="moba_sample",
    )(page_table, q, k_new, v_new, slope_tab, cache_k, cache_v)


def _ffn_residual(x1, mod_ref, g, w1_ref, w2_ref, n_chunk):
    h = (_rms(x1, g) * (1.0 + mod_ref[4]) + mod_ref[3]).astype(BF16)
    ck = w1_ref.shape[1] // n_chunk
    acc = jnp.zeros(x1.shape, F32)
    for j in range(n_chunk):
        hid = jnp.dot(h, w1_ref[:, j * ck:(j + 1) * ck], preferred_element_type=F32)
        hid = jnp.square(jnp.maximum(hid, 0.0)).astype(BF16)
        acc = acc + jnp.dot(hid, w2_ref[j * ck:(j + 1) * ck, :], preferred_element_type=F32)
    return x1 + mod_ref[5] * acc


def _back_ab_kernel(n_chunk, x_ref, attn_ref, gated_ref, mod_ref, g_ref, wo_ref, w1_ref, w2_ref,
                    o_ref):
    aw = attn_ref.shape[1]
    mix = (jnp.dot(attn_ref[...], wo_ref[0:aw, :], preferred_element_type=F32)
           + jnp.dot(gated_ref[...], wo_ref[aw:, :], preferred_element_type=F32))
    x1 = x_ref[...] + mod_ref[2] * mix
    o_ref[...] = _ffn_residual(x1, mod_ref, g_ref[...], w1_ref, w2_ref, n_chunk)


def _mod_spec(mods, layer, n_rows_per_group, tm):
    d = mods.shape[-1]
    if mods.ndim == 5:
        nt = n_rows_per_group // tm
        return pl.BlockSpec((None, None, 6, 1, d), lambda r: (layer, r // nt, 0, 0, 0))
    return pl.BlockSpec((None, 6, tm, d), lambda r: (layer, 0, r, 0))


def _back_ab(x, attn, gated, mods, g, w_out, w1, w2, layer, rows_per_group, tm):
    n, d = x.shape
    aw = attn.shape[1]
    cw = gated.shape[1]
    row = lambda r: (r, 0)
    return pl.pallas_call(
        functools.partial(_back_ab_kernel, 4),
        grid=(n // tm,),
        in_specs=[
            pl.BlockSpec((tm, d), row),
            pl.BlockSpec((tm, aw), row),
            pl.BlockSpec((tm, cw), row),
            _mod_spec(mods, layer, rows_per_group, tm),
            _resident(g.shape),
            _resident(w_out.shape),
            _resident(w1.shape),
            _resident(w2.shape),
        ],
        out_specs=pl.BlockSpec((tm, d), row),
        out_shape=jax.ShapeDtypeStruct((n, d), F32),
        compiler_params=_params(("arbitrary",)),
        name="back_ab",
    )(x, attn, gated, mods, g, w_out, w1, w2)


def _gmlp_kernel(n_chunk, emit_v, x_ref, mod_ref, g1_ref, g2_ref, wi_ref, gv_ref, ws_ref, bs_ref,
                 wo_ref, w1_ref, w2_ref, gf_ref, *rest):
    if emit_v:
        o_ref, cv_ref, us_ref = rest
    else:
        o_ref, us_ref = rest
    tm = x_ref.shape[0]
    cwid = gv_ref.shape[1]
    n_grp, chunk, _ = ws_ref.shape
    grp = cwid // n_grp
    x = x_ref[...]
    hb = (_rms(x, g1_ref[...]) * (1.0 + mod_ref[1]) + mod_ref[0]).astype(BF16)
    u = jnp.dot(hb, wi_ref[:, 0:cwid], preferred_element_type=F32)
    v = _rms(jnp.dot(hb, wi_ref[:, cwid:], preferred_element_type=F32), gv_ref[...])
    if emit_v:
        cv_ref[...] = v
    vb = v.astype(BF16)
    for c in range(tm // chunk):
        rs = slice(c * chunk, (c + 1) * chunk)
        for gi in range(n_grp):
            ls = slice(gi * grp, (gi + 1) * grp)
            s = jnp.dot(ws_ref[gi], vb[rs, ls], preferred_element_type=F32) + bs_ref[gi]
            us_ref[rs, ls] = (u[rs, ls] * s).astype(BF16)
    mix = jnp.dot(us_ref[...], wo_ref[...], preferred_element_type=F32)
    x1 = x + mod_ref[2] * mix
    x2 = _ffn_residual(x1, mod_ref, g2_ref[...], w1_ref, w2_ref, n_chunk)
    o_ref[...] = _rms(x2, gf_ref[...])


def _gmlp_layer(x, mods, g1, g2, w_in, g_v, ws_mat, bs_full, w_out, w1, w2, g_final, layer,
                rows_per_group, tm, emit_v):
    n, d = x.shape
    cwid = g_v.shape[1]
    row = lambda r: (r, 0)
    out_shape = [jax.ShapeDtypeStruct((n, d), F32)]
    out_specs = [pl.BlockSpec((tm, d), row)]
    if emit_v:
        out_shape.append(jax.ShapeDtypeStruct((n, cwid), F32))
        out_specs.append(pl.BlockSpec((tm, cwid), row))
    return pl.pallas_call(
        functools.partial(_gmlp_kernel, 4, emit_v),
        grid=(n // tm,),
        in_specs=[
            pl.BlockSpec((tm, d), row),
            _mod_spec(mods, layer, rows_per_group, tm),
            _resident(g1.shape),
            _resident(g2.shape),
            _resident(w_in.shape),
            _resident(g_v.shape),
            _resident(ws_mat.shape),
            _resident(bs_full.shape),
            _resident(w_out.shape),
            _resident(w1.shape),
            _resident(w2.shape),
            _resident(g_final.shape),
        ],
        out_specs=out_specs,
        out_shape=out_shape,
        scratch_shapes=[pltpu.VMEM((tm, cwid), BF16)],
        compiler_params=_params(("arbitrary",)),
        name="gmlp_layer",
    )(x, mods, g1, g2, w_in, g_v, ws_mat, bs_full, w_out, w1, w2, g_final)


def _alibi_slopes(n_heads):
    return jnp.exp2(-8.0 * jnp.arange(1, n_heads + 1, dtype=F32) / n_heads)


def kernel(x_prompt, x_sample, cache_k, cache_v, state_conv, page_table, c_prompt, c_sample,
           norm_g, w_ada, b_ada, w_in_ab, conv_w, w_out_ab, w_in_c, g_v, w_s, b_s, w_out_c,
           w_ff1, w_ff2, g_final):
    batch, seq, d = x_prompt.shape
    bd, tq, _ = x_sample.shape
    n_ab, n_pool, page, n_heads, hd = cache_k.shape
    aw = n_heads * hd
    cw = state_conv.shape[-1]
    n_c, n_grp, chunk, _ = w_s.shape
    cwid = g_v.shape[-1]
    depth = norm_g.shape[0]
    assert depth == 2 and n_ab == 1 and n_c == 1 and hd * 2 == LANES
    assert tq <= chunk and chunk % tq == 0 and (bd * tq) % chunk == 0
    n_p, n_s = batch * seq, bd * tq
    tm = 512 if seq % 512 == 0 else chunk

    wb = lambda w: w.astype(BF16)
    w_in_ab_b, w_out_ab_b = wb(w_in_ab[0]), wb(w_out_ab[0])
    w_in_c_b, w_out_c_b = wb(w_in_c[0]), wb(w_out_c[0])
    w_ff1_b, w_ff2_b = wb(w_ff1), wb(w_ff2)

    bp = -(-(batch + bd) // SUBLANES) * SUBLANES
    c_all = jnp.concatenate([c_prompt, c_sample, jnp.zeros((bp - batch - bd, d), F32)], axis=0)
    ada = _ada_terms(c_all, w_ada, b_ada)
    mods_p = ada[:, :batch].reshape(depth, batch, 6, 1, d)
    mods_s = jnp.repeat(ada[:, batch:batch + bd].reshape(depth, bd, 6, d), tq, axis=1)
    mods_s = mods_s.transpose(0, 2, 1, 3)

    slopes = _alibi_slopes(n_heads)
    slope_tab = jnp.broadcast_to(slopes[:, None], (n_heads, LANES))
    g = lambda l, j: norm_g[l, j].reshape(1, d)

    tril = jnp.tril(jnp.ones((chunk, chunk), bool))
    ws_p = jnp.where(tril, w_s[0], 0.0).astype(BF16)
    bs_p = jnp.broadcast_to(b_s[0][:, :, None], (n_grp, chunk, cwid // n_grp))
    ws_small = jnp.where(tril[:tq, :tq], w_s[0][:, :tq, :tq], 0.0)
    eye = jnp.eye(chunk // tq, dtype=F32)
    ws_s = jnp.einsum("ab,gts->gatbs", eye, ws_small).reshape(n_grp, chunk, chunk).astype(BF16)
    bs_s = jnp.broadcast_to(jnp.tile(b_s[0][:, :tq], (1, chunk // tq))[:, :, None],
                            (n_grp, chunk, cwid // n_grp))

    xp = x_prompt.reshape(n_p, d)
    prev0 = jnp.zeros((batch, CONV_W - 1, cw), F32)
    qp, kp_b, kp_t, vp_t, km_p, gated_p, conv_p = _front_prompt(
        xp, mods_p, g(0, 0), w_in_ab_b, conv_w[0], prev0, 0, tm, n_heads, page)
    attn_p = _moba_prompt(qp, kp_b, vp_t, km_p.reshape(n_p // MOBA_BLOCK, aw), slopes, batch)
    xp = _back_ab(xp, attn_p, gated_p, mods_p, g(0, 1), w_out_ab_b, w_ff1_b[0], w_ff2_b[0], 0,
                  seq, tm)
    (yp,) = _gmlp_layer(xp, mods_p, g(1, 0), g(1, 1), w_in_c_b, g_v, ws_p, bs_p, w_out_c_b,
                        w_ff1_b[1], w_ff2_b[1], g_final.reshape(1, d), 1, seq, tm, False)

    pool_pages = lambda c: c.transpose(0, 1, 3, 4, 2).reshape(n_pool, aw, page)
    xs = x_sample.reshape(n_s, d)
    st = state_conv[0]
    pad = lambda a: jnp.concatenate([a, jnp.zeros((bd, tq - a.shape[1], cw), F32)], axis=1)
    pe1 = pad(st[:, 1:2]).reshape(n_s, cw)
    pe2 = pad(st).reshape(n_s, cw)
    qs, ks, vs, gated_s, u_s = _front_sample(xs, mods_s, g(0, 0), w_in_ab_b, conv_w[0], pe1, pe2,
                                             0, tq)
    attn_s = _moba_sample(qs.reshape(bd, tq, aw), ks.reshape(bd, tq, aw), vs.reshape(bd, tq, aw),
                          pool_pages(cache_k), pool_pages(cache_v), page_table, slope_tab,
                          n_heads)
    xs = _back_ab(xs, attn_s.reshape(n_s, aw), gated_s, mods_s, g(0, 1), w_out_ab_b, w_ff1_b[0],
                  w_ff2_b[0], 0, n_s, chunk)
    ys, cv_s = _gmlp_layer(xs, mods_s, g(1, 0), g(1, 1), w_in_c_b, g_v, ws_s, bs_s, w_out_c_b,
                           w_ff1_b[1], w_ff2_b[1], g_final.reshape(1, d), 1, n_s, chunk, True)

    return (
        yp.reshape(batch, seq, d),
        ys.reshape(bd, tq, d),
        kp_t.transpose(0, 3, 1, 2)[None],
        vp_t.transpose(0, 3, 1, 2)[None],
        conv_p[None],
        ks.reshape(1, bd, tq, n_heads, hd),
        vs.reshape(1, bd, tq, n_heads, hd),
        u_s.reshape(bd, tq, cw)[:, tq - (CONV_W - 1):][None],
        cv_s.reshape(1, bd, tq, cwid),
    )
```

```python
import functools

import jax
import jax.numpy as jnp
from jax import lax
from jax.experimental import pallas as pl
from jax.experimental.pallas import tpu as pltpu

F32 = jnp.float32
BF16 = jnp.bfloat16

EPS = 1e-6
MOBA_BLOCK = 256
MOBA_BLOCK_SHIFT = 8
MOBA_TOPK = 3
CONV_W = 3
LANES = 128
SUBLANES = 8
ONES_ROWS = 2 * SUBLANES
NEG_BIG = -(2.0 ** 100)
VMEM_LIMIT_BYTES = 56 * 1024 * 1024

_NT = (((1,), (1,)), ((), ()))


def _params(sem):
    return pltpu.CompilerParams(dimension_semantics=sem, vmem_limit_bytes=VMEM_LIMIT_BYTES)


def _resident(shape):
    n = len(shape)
    return pl.BlockSpec(shape, lambda *_: (0,) * n, pipeline_mode=pl.Buffered(1))


def _rms(x, g):
    return x * lax.rsqrt(jnp.mean(x * x, axis=-1, keepdims=True) + EPS) * g


def _split_bf16(x):
    hi = x.astype(BF16)
    lo = (x - hi.astype(F32)).astype(BF16)
    return hi, lo


def _ada_kernel(c_ref, w_ref, b_ref, o_ref):
    c = c_ref[...]
    s = (c * jax.nn.sigmoid(c)).astype(BF16)
    o_ref[0] = jnp.dot(s, w_ref[0].astype(BF16), preferred_element_type=F32) + b_ref[0]


def _ada_terms(c_all, w_ada, b_ada):
    n_layers, d, d6 = w_ada.shape
    bp = c_all.shape[0]
    tn = d6 // 4
    return pl.pallas_call(
        _ada_kernel,
        grid=(n_layers, d6 // tn),
        in_specs=[
            pl.BlockSpec((bp, d), lambda l, j: (0, 0)),
            pl.BlockSpec((1, d, tn), lambda l, j: (l, 0, j)),
            pl.BlockSpec((1, 1, tn), lambda l, j: (l, 0, j)),
        ],
        out_specs=pl.BlockSpec((1, bp, tn), lambda l, j: (l, 0, j)),
        out_shape=jax.ShapeDtypeStruct((n_layers, bp, d6), F32),
        compiler_params=_params(("arbitrary", "arbitrary")),
        name="ada_terms",
    )(c_all, w_ada, b_ada.reshape(n_layers, 1, d6))


def _front_kernel(multi_seq, seq_len, *refs):
    if multi_seq:
        (x_ref, mod_ref, g_ref, w_ref, cw_ref, pe1_ref, pe2_ref,
         q_ref, k_ref, v_ref, gated_ref, u_ref) = refs
    else:
        (x_ref, mod_ref, g_ref, w_ref, cw_ref, prev_ref,
         q_ref, kb_ref, kt_ref, vt_ref, km_ref, gated_ref, cs_ref, carry_ref) = refs
    tm = x_ref.shape[0]
    cw = gated_ref.shape[1]
    h = _rms(x_ref[...], g_ref[...]) * (1.0 + mod_ref[1]) + mod_ref[0]
    hb = h.astype(BF16)

    def proj(j):
        return jnp.dot(hb, w_ref[:, j * cw:(j + 1) * cw], preferred_element_type=F32)

    q_ref[...] = proj(0)
    if multi_seq:
        k_ref[...] = proj(1)
        v_ref[...] = proj(2)
    else:
        n_page, n_heads, hd, page = kt_ref.shape
        yk = proj(1)
        kb_ref[...] = yk.astype(BF16)
        for j in range(tm // MOBA_BLOCK):
            km_ref[j] = jnp.mean(yk[j * MOBA_BLOCK:(j + 1) * MOBA_BLOCK, :], axis=0, keepdims=True)
        yv = proj(2)
        for y, t_ref in ((yk, kt_ref), (yv, vt_ref)):
            yt = y.T
            for j in range(n_page):
                t_ref[j] = yt[:, j * page:(j + 1) * page].reshape(n_heads, hd, page)
    u = proj(4) * proj(5)
    rows = lax.broadcasted_iota(jnp.int32, u.shape, 0)
    u1 = pltpu.roll(u, 1, axis=0)
    u2 = pltpu.roll(u, 2, axis=0)
    if multi_seq:
        assert seq_len & (seq_len - 1) == 0
        t = rows & (seq_len - 1)
        u1 = jnp.where(t == 0, pe1_ref[...], u1)
        u2 = jnp.where(t < 2, pe2_ref[...], u2)
        u_ref[...] = u
    else:
        i = pl.program_id(1)

        @pl.when(i == 0)
        def _():
            carry_ref[SUBLANES - 2:SUBLANES, :] = prev_ref[...]

        p0 = carry_ref[SUBLANES - 2:SUBLANES - 1, :]
        p1 = carry_ref[SUBLANES - 1:SUBLANES, :]
        u1 = jnp.where(rows == 0, p1, u1)
        u2 = jnp.where(rows == 0, p0, jnp.where(rows == 1, p1, u2))
        carry_ref[...] = u[tm - SUBLANES:tm, :]

        @pl.when(i == pl.num_programs(1) - 1)
        def _():
            cs_ref[...] = u[tm - 2:tm, :]

    yc = u2 * cw_ref[0:1, :] + u1 * cw_ref[1:2, :] + u * cw_ref[2:3, :]
    gated_ref[...] = (proj(3) * yc).astype(BF16)


def _front_prompt(x, mods, g, w_in, conv_w, prev, layer, tm, n_heads, page):
    n, d = x.shape
    nb = prev.shape[0]
    cw = prev.shape[2]
    nt = n // nb // tm
    assert tm % page == 0 and tm % MOBA_BLOCK == 0
    row = lambda b, i: (b * nt + i, 0)
    pages = (tm // page, n_heads, cw // n_heads, page)
    page_map = lambda b, i: (b * nt + i, 0, 0, 0)
    n_mean = tm // MOBA_BLOCK
    return pl.pallas_call(
        functools.partial(_front_kernel, False, 0),
        grid=(nb, nt),
        in_specs=[
            pl.BlockSpec((tm, d), row),
            pl.BlockSpec((None, None, 6, 1, d), lambda b, i: (layer, b, 0, 0, 0)),
            _resident(g.shape),
            _resident(w_in.shape),
            _resident(conv_w.shape),
            pl.BlockSpec((None, CONV_W - 1, cw), lambda b, i: (b, 0, 0)),
        ],
        out_specs=[
            pl.BlockSpec((tm, cw), row),
            pl.BlockSpec((tm, cw), row),
            pl.BlockSpec(pages, page_map),
            pl.BlockSpec(pages, page_map),
            pl.BlockSpec((n_mean, 1, cw), lambda b, i: (b * nt + i, 0, 0)),
            pl.BlockSpec((tm, cw), row),
            pl.BlockSpec((None, CONV_W - 1, cw), lambda b, i: (b, 0, 0)),
        ],
        out_shape=[
            jax.ShapeDtypeStruct((n, cw), F32),
            jax.ShapeDtypeStruct((n, cw), BF16),
            jax.ShapeDtypeStruct((n // page,) + pages[1:], F32),
            jax.ShapeDtypeStruct((n // page,) + pages[1:], F32),
            jax.ShapeDtypeStruct((n // MOBA_BLOCK, 1, cw), F32),
            jax.ShapeDtypeStruct((n, cw), BF16),
            jax.ShapeDtypeStruct(prev.shape, F32),
        ],
        scratch_shapes=[pltpu.VMEM((SUBLANES, cw), F32)],
        compiler_params=_params(("arbitrary", "arbitrary")),
        name="front_prompt",
    )(x, mods, g, w_in, conv_w, prev)


def _front_sample(x, mods_rows, g, w_in, conv_w, pe1, pe2, layer, seq_len):
    n, d = x.shape
    cw = pe1.shape[1]
    full = lambda i: (0, 0)
    return pl.pallas_call(
        functools.partial(_front_kernel, True, seq_len),
        grid=(1,),
        in_specs=[
            pl.BlockSpec((n, d), full),
            pl.BlockSpec((None, 6, n, d), lambda i: (layer, 0, 0, 0)),
            _resident(g.shape),
            _resident(w_in.shape),
            _resident(conv_w.shape),
            pl.BlockSpec((n, cw), full),
            pl.BlockSpec((n, cw), full),
        ],
        out_specs=[pl.BlockSpec((n, cw), full)] * 5,
        out_shape=[
            jax.ShapeDtypeStruct((n, cw), F32),
            jax.ShapeDtypeStruct((n, cw), F32),
            jax.ShapeDtypeStruct((n, cw), F32),
            jax.ShapeDtypeStruct((n, cw), BF16),
            jax.ShapeDtypeStruct((n, cw), F32),
        ],
        compiler_params=_params(("arbitrary",)),
        name="front_sample",
    )(x, mods_rows, g, w_in, conv_w, pe1, pe2)


def _moba_item(step, lag, n_items, n_half, n_pair):
    item = jnp.clip(step - lag, 0, n_items - 1)
    seq_pair = item // n_half
    return seq_pair // n_pair, seq_pair % n_pair, item % n_half


def _moba_prompt_kernel(n_blk, n_pair, slopes_ref, qa_ref, qb_ref, k_ref, v_ref, km_ref, o_ref,
                        kb_ref, vt_ref, qf_ref, st_ref, um_ref, m_ref, acc_ref):
    step = pl.program_id(0)
    n_items = pl.num_programs(0) - 1
    n_half = n_blk // 2
    blk = MOBA_BLOCK
    seq = k_ref.shape[0]
    hd = v_ref.shape[2]
    page = v_ref.shape[3]
    ppb = blk // page
    n_prev = n_blk - 1
    scale = float(hd) ** -0.5
    b, p, j = _moba_item(step, 0, n_items, n_half, n_pair)
    b_old, p_old, j_old = _moba_item(step, 1, n_items, n_half, n_pair)
    slot = (b * n_pair + p) % 2
    slot_old = (b_old * n_pair + p_old) % 2
    tiles = (j, n_blk - 1 - j)
    tiles_old = (j_old, n_blk - 1 - j_old)

    @pl.when(step == 0)
    def _():
        r = lax.broadcasted_iota(jnp.int32, (seq, LANES), 0)
        c = lax.broadcasted_iota(jnp.int32, (seq, LANES), 1)
        n = lax.shift_right_logical(r, MOBA_BLOCK_SHIFT)
        off = r & (blk - 1)
        feat = jnp.where(c == n, 1.0,
                         jnp.where(c == n_blk, n.astype(F32),
                                   jnp.where(c == n_blk + 1, off.astype(F32),
                                             jnp.where(c == n_blk + 2, 1.0, 0.0))))
        kb_ref[:, LANES:2 * LANES] = feat.astype(BF16)
        st_ref[...] = jnp.zeros_like(st_ref)
        m_ref[...] = jnp.zeros_like(m_ref)
        vt_ref[:, :, :, hd:, :] = jnp.ones(vt_ref.shape[:3] + (ONES_ROWS, blk), BF16)

    @pl.when((j == 0) & (step < n_items))
    def _():
        kb_ref[:, 0:LANES] = k_ref[...]
        for n in range(n_blk):
            for e in range(2):
                vt_ref[slot, n, e, 0:hd, :] = jnp.concatenate(
                    [v_ref[n * ppb + g, e] for g in range(ppb)], axis=1).astype(BF16)

    _moba_setup(n_blk, scale, tiles, p, slopes_ref, qa_ref, qb_ref, km_ref, qf_ref)

    n_grp = blk // SUBLANES
    wide = 2 * blk
    feat_lane = lax.broadcasted_iota(jnp.int32, (1, 2 * LANES), 1) - LANES
    ind_off = jnp.where((feat_lane >= 0) & (feat_lane < n_blk), 0.0, 1.0).astype(BF16)
    key_row = lax.broadcasted_iota(jnp.int32, (blk, wide), 0)
    q_col = lax.broadcasted_iota(jnp.int32, (blk, wide), 1) & (blk - 1)

    def unit_ids(u, jj, item_tiles):
        if u >= n_prev:
            return u - n_prev, item_tiles[u - n_prev]
        is_a = u < jj
        return jnp.where(is_a, 0, 1), jnp.where(is_a, u, u - jj)

    acc_ref[...] = jnp.zeros_like(acc_ref)
    for u in range(n_prev + 2):
        t_old, n_old = unit_ids(u, j_old, tiles_old)
        pr = jnp.exp(st_ref[u].reshape(n_grp, SUBLANES, wide) - m_ref[t_old][None])
        pb = pr.reshape(blk, wide).astype(BF16)
        for e in range(2):
            acc_ref[t_old, e] = acc_ref[t_old, e] + jnp.dot(
                vt_ref[slot_old, n_old, e], pb[:, e * blk:(e + 1) * blk],
                preferred_element_type=F32)
        t_new, n_new = unit_ids(u, j, tiles)
        own = u >= n_prev
        kblk = kb_ref[pl.ds(pl.multiple_of(n_new * blk, blk), blk), :]
        if own:
            kblk = kblk * ind_off
        s = jnp.dot(kblk, qf_ref[t_new], preferred_element_type=F32)
        if own:
            s = jnp.where(key_row <= q_col, s, NEG_BIG)
        st_ref[u] = s
        um_ref[u] = jnp.max(s.reshape(n_grp, SUBLANES, wide), axis=0)

    for t in range(2):
        out_t = jnp.concatenate(
            [acc_ref[t, e, 0:hd, :] / acc_ref[t, e, hd:hd + 1, :] for e in range(2)], axis=0)
        o_ref[pl.ds(pl.multiple_of(tiles_old[t] * blk, blk), blk), :] = (
            out_t.T.astype(o_ref.dtype))
    m_new = [um_ref[n_prev], um_ref[n_prev + 1]]
    for u in range(n_prev):
        val = um_ref[u]
        is_a = u < j
        m_new[0] = jnp.maximum(m_new[0], jnp.where(is_a, val, -jnp.inf))
        m_new[1] = jnp.maximum(m_new[1], jnp.where(is_a, -jnp.inf, val))
    for t in range(2):
        m_ref[t] = jnp.broadcast_to(jnp.max(m_new[t], axis=0, keepdims=True), m_new[t].shape)


def _moba_setup(n_blk, scale, tiles, pair, slopes_ref, qa_ref, qb_ref, km_ref, qf_ref):
    blk = MOBA_BLOCK
    hd = LANES // 2
    km = km_ref[...]
    lane = lax.broadcasted_iota(jnp.int32, (1, LANES), 1)
    head_lanes = [(lane < hd).astype(F32), (lane >= hd).astype(F32)]
    qrow = lax.broadcasted_iota(jnp.int32, (LANES, blk), 0)
    head_rows = [(qrow < hd).astype(F32), (qrow >= hd).astype(F32)]
    blk_row = lax.broadcasted_iota(jnp.int32, (n_blk, blk), 0)
    blk_row_f = blk_row.astype(F32)
    for t, q_ref in enumerate((qa_ref, qb_ref)):
        qt = q_ref[...].T
        qt_hi, qt_lo = _split_bf16(qt)
        tile_f = tiles[t].astype(F32)
        valid = blk_row < tiles[t]
        for e in range(2):
            slope = slopes_ref[2 * pair + e]
            km_hi, km_lo = _split_bf16(km * head_lanes[e])
            gate_t = (jnp.dot(km_hi, qt_hi, preferred_element_type=F32)
                      + jnp.dot(km_lo, qt_hi, preferred_element_type=F32)
                      + jnp.dot(km_hi, qt_lo, preferred_element_type=F32))
            gate_t = jnp.where(valid, gate_t, -jnp.inf)
            taken = jnp.zeros((n_blk, blk), F32)
            for _ in range(MOBA_TOPK):
                is_max = gate_t == jnp.max(gate_t, axis=0, keepdims=True)
                first = jnp.min(jnp.where(is_max, blk_row_f, float(n_blk)), axis=0,
                                keepdims=True)
                pick = blk_row_f == first
                taken = jnp.where(pick, 1.0, taken)
                gate_t = jnp.where(pick, -jnp.inf, gate_t)
            mask_t = jnp.where(valid, jnp.where(taken > 0.0, 0.0, NEG_BIG), NEG_BIG)
            consts = jnp.where(qrow == n_blk, slope * blk,
                               jnp.where(qrow == n_blk + 1, slope,
                                         jnp.where(qrow == n_blk + 2, -slope * blk * tile_f, 0.0)))
            feat_t = jnp.concatenate([mask_t, jnp.zeros((LANES - n_blk, blk), F32)],
                                     axis=0) + consts
            q_e = (qt * (head_rows[e] * scale)).astype(BF16)
            qf_ref[t, :, e * blk:(e + 1) * blk] = jnp.concatenate(
                [q_e, feat_t.astype(BF16)], axis=0)


def _moba_prompt(q, k_rows, v_pages, k_mean, slopes, batch):
    n, aw = q.shape
    _, n_heads, hd, page = v_pages.shape
    seq = n // batch
    blk = MOBA_BLOCK
    n_blk = seq // blk
    assert seq % blk == 0 and blk % page == 0 and 2 * hd == LANES
    assert n_blk % SUBLANES == 0 and n_blk + 3 <= LANES
    n_pair = aw // LANES
    n_half = n_blk // 2
    n_items = batch * n_pair * n_half
    n_unit = n_blk + 1

    def item_map(lag, fn):
        return lambda s, _: fn(*_moba_item(s, lag, n_items, n_half, n_pair))

    return pl.pallas_call(
        functools.partial(_moba_prompt_kernel, n_blk, n_pair),
        grid_spec=pltpu.PrefetchScalarGridSpec(
            num_scalar_prefetch=1,
            grid=(n_items + 1,),
            in_specs=[
                pl.BlockSpec((blk, LANES), item_map(0, lambda b, p, j: (b * n_blk + j, p))),
                pl.BlockSpec((blk, LANES),
                             item_map(0, lambda b, p, j: (b * n_blk + n_blk - 1 - j, p))),
                pl.BlockSpec((seq, LANES), item_map(0, lambda b, p, j: (b, p))),
                pl.BlockSpec((seq // page, 2, hd, page),
                             item_map(0, lambda b, p, j: (b, p, 0, 0))),
                pl.BlockSpec((n_blk, LANES), item_map(0, lambda b, p, j: (b, p))),
            ],
            out_specs=pl.BlockSpec((seq, LANES), item_map(1, lambda b, p, j: (b, p))),
            scratch_shapes=[
                pltpu.VMEM((seq, 2 * LANES), BF16),
                pltpu.VMEM((2, n_blk, 2, hd + ONES_ROWS, blk), BF16),
                pltpu.VMEM((2, 2 * LANES, 2 * blk), BF16),
                pltpu.VMEM((n_unit, blk, 2 * blk), F32),
                pltpu.VMEM((n_unit, SUBLANES, 2 * blk), F32),
                pltpu.VMEM((2, SUBLANES, 2 * blk), F32),
                pltpu.VMEM((2, 2, hd + ONES_ROWS, blk), F32),
            ],
        ),
        out_shape=jax.ShapeDtypeStruct((n, aw), BF16),
        compiler_params=_params(("arbitrary",)),
        name="moba_prompt",
    )(slopes, q, q, k_rows, v_pages, k_mean)


def _moba_sample_kernel(n_pages, n_heads, pt_ref, q_ref, kn_ref, vn_ref, slope_ref,
                        ck_hbm, cv_hbm, o_ref, kbuf, vbuf, s_ref, km_ref, bias_ref, ksem, vsem):
    b = pl.program_id(0)
    blk = MOBA_BLOCK
    tq, aw = q_ref.shape
    page = kbuf.shape[2]
    ppb = blk // page
    hd = aw // n_heads
    rows = tq * n_heads
    past = n_pages * page
    n_blk = past // blk

    def page_copy(src_hbm, buf, sem, sample, pg):
        return pltpu.make_async_copy(src_hbm.at[pt_ref[sample, pg]], buf.at[pg], sem)

    def for_pages(fn):
        def go(pg, c):
            fn(pg)
            return c
        lax.fori_loop(0, n_pages, go, 0)

    feat0 = n_blk

    @pl.when(b == 0)
    def _():
        for_pages(lambda pg: page_copy(ck_hbm, kbuf, ksem, b, pg).start())
        blk_id = lax.broadcasted_iota(jnp.int32, bias_ref.shape, 0)
        f = lax.broadcasted_iota(jnp.int32, bias_ref.shape, 1)
        off = lax.broadcasted_iota(jnp.int32, bias_ref.shape, 2)
        bias_ref[...] = jnp.where(
            f == blk_id, 1.0,
            jnp.where(f == feat0, (blk_id * blk).astype(F32),
                      jnp.where(f == feat0 + 1, off.astype(F32),
                                jnp.where((f == feat0 + 2) | (f == feat0 + 3), 1.0, 0.0)))
        ).astype(BF16)

    for_pages(lambda pg: page_copy(cv_hbm, vbuf, vsem, b, pg).start())

    lane = lax.broadcasted_iota(jnp.int32, (n_heads, aw), 1)
    hrow = lax.broadcasted_iota(jnp.int32, (n_heads, aw), 0)
    head_mask = jnp.where(lane >= hrow * hd, jnp.where(lane < (hrow + 1) * hd, 1.0, 0.0), 0.0)
    hm_rows = jnp.concatenate([head_mask] * tq, axis=0)
    q = q_ref[...]
    q_rows = jnp.concatenate(
        [jnp.broadcast_to(q[t:t + 1, :], (n_heads, aw)) for t in range(tq)], axis=0) * hm_rows
    qs_bf = (q_rows * (float(hd) ** -0.5)).astype(BF16)
    slope = jnp.concatenate([slope_ref[:, 0:1]] * tq, axis=0)
    t_row = jnp.concatenate([jnp.full((n_heads, 1), t, jnp.int32) for t in range(tq)], axis=0)

    for_pages(lambda pg: page_copy(ck_hbm, kbuf, ksem, b, pg).wait())
    km_ref[...] = jnp.zeros_like(km_ref)
    for n in range(n_blk):
        pages_n = [kbuf[n * ppb + g] for g in range(ppb)]
        km_ref[:, n:n + 1] = jnp.sum(functools.reduce(jnp.add, pages_n), axis=1,
                                     keepdims=True) * (1.0 / blk)
        s_ref[n] = jnp.dot(qs_bf, jnp.concatenate(pages_n, axis=1).astype(BF16),
                           preferred_element_type=F32)
    km_t = km_ref[...]

    @pl.when(b + 1 < pl.num_programs(0))
    def _():
        for_pages(lambda pg: page_copy(ck_hbm, kbuf, ksem, b + 1, pg).start())

    q_hi, q_lo = _split_bf16(q_rows)
    km_hi, km_lo = _split_bf16(km_t)
    gate = (jnp.dot(q_hi, km_hi, preferred_element_type=F32)
            + jnp.dot(q_hi, km_lo, preferred_element_type=F32)
            + jnp.dot(q_lo, km_hi, preferred_element_type=F32))
    blk_lane = lax.broadcasted_iota(jnp.int32, gate.shape, 1)
    gate = jnp.where(blk_lane < n_blk, gate, -jnp.inf)
    lane_f = blk_lane.astype(F32)
    taken = jnp.zeros(gate.shape, F32)
    for _ in range(min(MOBA_TOPK, n_blk)):
        is_max = gate == jnp.max(gate, axis=1, keepdims=True)
        first = jnp.min(jnp.where(is_max, lane_f, float(LANES)), axis=1, keepdims=True)
        pick = lane_f == first
        taken = jnp.where(pick, 1.0, taken)
        gate = jnp.where(pick, -jnp.inf, gate)
    slope_l = jnp.broadcast_to(slope, gate.shape)
    t_l = jnp.broadcast_to(t_row, gate.shape).astype(F32)
    feat = jnp.where(
        blk_lane < n_blk, jnp.where(taken > 0.0, 0.0, NEG_BIG),
        jnp.where((blk_lane == feat0) | (blk_lane == feat0 + 1), slope_l,
                  jnp.where(blk_lane == feat0 + 2, -slope_l * past,
                            jnp.where(blk_lane == feat0 + 3, -slope_l * t_l, 0.0))))
    feat_bf = feat.astype(BF16)

    kn = kn_ref[...]
    vn = vn_ref[...]
    q_sc = q_rows * (float(hd) ** -0.5)
    own = []
    for t in range(tq):
        so = jnp.sum(q_sc * kn[t:t + 1, :], axis=1, keepdims=True)
        so = so - slope * (t_row - t).astype(F32)
        own.append(jnp.where(t_row >= t, so, NEG_BIG))
    m_elem = jnp.full((rows, blk), NEG_BIG, F32)
    for n in range(n_blk):
        s = s_ref[n] + jnp.dot(feat_bf, bias_ref[n], preferred_element_type=F32)
        s_ref[n] = s
        m_elem = jnp.maximum(m_elem, s)
    m_run = jnp.maximum(functools.reduce(jnp.maximum, own),
                        jnp.max(m_elem, axis=1, keepdims=True))
    m_l = jnp.broadcast_to(m_run, (rows, blk))

    acc = jnp.zeros((rows, aw), F32)
    l_run = jnp.zeros((rows, 1), F32)
    for t in range(tq):
        po = jnp.exp(own[t] - m_run)
        l_run = l_run + po
        acc = acc + po * vn[t:t + 1, :]
    for_pages(lambda pg: page_copy(cv_hbm, vbuf, vsem, b, pg).wait())
    l_elem = jnp.zeros((rows, blk), F32)
    for n in range(n_blk):
        pr = jnp.exp(s_ref[n] - m_l)
        l_elem = l_elem + pr
        v_blk = jnp.concatenate([vbuf[n * ppb + g] for g in range(ppb)], axis=1)
        acc = lax.dot_general(pr.astype(BF16), v_blk.astype(BF16), _NT,
                              preferred_element_type=F32) + acc
    l_run = l_run + jnp.sum(l_elem, axis=1, keepdims=True)
    out_rows = acc * hm_rows / l_run
    o_ref[...] = jnp.sum(out_rows.reshape(tq, n_heads, aw), axis=1).astype(o_ref.dtype)


def _moba_sample(q, k_new, v_new, cache_k, cache_v, page_table, slope_tab, n_heads):
    bd, tq, aw = q.shape
    n_pages = page_table.shape[1]
    page = cache_k.shape[2]
    past = n_pages * page
    assert MOBA_BLOCK % page == 0 and past % MOBA_BLOCK == 0 and page == LANES
    assert past // MOBA_BLOCK + 4 <= LANES
    rows = tq * n_heads
    tok = lambda b, pt: (b, 0, 0)
    return pl.pallas_call(
        functools.partial(_moba_sample_kernel, n_pages, n_heads),
        grid_spec=pltpu.PrefetchScalarGridSpec(
            num_scalar_prefetch=1,
            grid=(bd,),
            in_specs=[
                pl.BlockSpec((None, tq, aw), tok),
                pl.BlockSpec((None, tq, aw), tok),
                pl.BlockSpec((None, tq, aw), tok),
                pl.BlockSpec(slope_tab.shape, lambda b, pt: (0, 0)),
                pl.BlockSpec(memory_space=pl.ANY),
                pl.BlockSpec(memory_space=pl.ANY),
            ],
            out_specs=pl.BlockSpec((None, tq, aw), tok),
            scratch_shapes=[
                pltpu.VMEM((n_pages, aw, page), F32),
                pltpu.VMEM((n_pages, aw, page), F32),
                pltpu.VMEM((past // MOBA_BLOCK, rows, MOBA_BLOCK), F32),
                pltpu.VMEM((aw, LANES), F32),
                pltpu.VMEM((past // MOBA_BLOCK, LANES, MOBA_BLOCK), BF16),
                pltpu.SemaphoreType.DMA(()),
                pltpu.SemaphoreType.DMA(()),
            ],
        ),
        out_shape=jax.ShapeDtypeStruct((bd, tq, aw), BF16),
        compiler_params=_params(("arbitrary",)),
        name="moba_sample",
    )(page_table, q, k_new, v_new, slope_tab, cache_k, cache_v)


def _ffn_residual(x1, mod_ref, g, w1_ref, w2_ref, n_chunk):
    h = (_rms(x1, g) * (1.0 + mod_ref[4]) + mod_ref[3]).astype(BF16)
    ck = w1_ref.shape[1] // n_chunk
    acc = jnp.zeros(x1.shape, F32)
    for j in range(n_chunk):
        hid = jnp.dot(h, w1_ref[:, j * ck:(j + 1) * ck], preferred_element_type=F32)
        hid = jnp.square(jnp.maximum(hid, 0.0)).astype(BF16)
        acc = acc + jnp.dot(hid, w2_ref[j * ck:(j + 1) * ck, :], preferred_element_type=F32)
    return x1 + mod_ref[5] * acc


def _back_ab_kernel(n_chunk, x_ref, attn_ref, gated_ref, mod_ref, g_ref, wo_ref, w1_ref, w2_ref,
                    o_ref):
    aw = attn_ref.shape[1]
    mix = (jnp.dot(attn_ref[...], wo_ref[0:aw, :], preferred_element_type=F32)
           + jnp.dot(gated_ref[...], wo_ref[aw:, :], preferred_element_type=F32))
    x1 = x_ref[...] + mod_ref[2] * mix
    o_ref[...] = _ffn_residual(x1, mod_ref, g_ref[...], w1_ref, w2_ref, n_chunk)


def _mod_spec(mods, layer, n_rows_per_group, tm):
    d = mods.shape[-1]
    if mods.ndim == 5:
        nt = n_rows_per_group // tm
        return pl.BlockSpec((None, None, 6, 1, d), lambda r: (layer, r // nt, 0, 0, 0))
    return pl.BlockSpec((None, 6, tm, d), lambda r: (layer, 0, r, 0))


def _back_ab(x, attn, gated, mods, g, w_out, w1, w2, layer, rows_per_group, tm):
    n, d = x.shape
    aw = attn.shape[1]
    cw = gated.shape[1]
    row = lambda r: (r, 0)
    return pl.pallas_call(
        functools.partial(_back_ab_kernel, 4),
        grid=(n // tm,),
        in_specs=[
            pl.BlockSpec((tm, d), row),
            pl.BlockSpec((tm, aw), row),
            pl.BlockSpec((tm, cw), row),
            _mod_spec(mods, layer, rows_per_group, tm),
            _resident(g.shape),
            _resident(w_out.shape),
            _resident(w1.shape),
            _resident(w2.shape),
        ],
        out_specs=pl.BlockSpec((tm, d), row),
        out_shape=jax.ShapeDtypeStruct((n, d), F32),
        compiler_params=_params(("arbitrary",)),
        name="back_ab",
    )(x, attn, gated, mods, g, w_out, w1, w2)


def _gmlp_kernel(n_chunk, emit_v, x_ref, mod_ref, g1_ref, g2_ref, wi_ref, gv_ref, ws_ref, bs_ref,
                 wo_ref, w1_ref, w2_ref, gf_ref, *rest):
    if emit_v:
        o_ref, cv_ref, us_ref = rest
    else:
        o_ref, us_ref = rest
    tm = x_ref.shape[0]
    cwid = gv_ref.shape[1]
    n_grp, chunk, _ = ws_ref.shape
    grp = cwid // n_grp
    x = x_ref[...]
    hb = (_rms(x, g1_ref[...]) * (1.0 + mod_ref[1]) + mod_ref[0]).astype(BF16)
    u = jnp.dot(hb, wi_ref[:, 0:cwid], preferred_element_type=F32)
    v = _rms(jnp.dot(hb, wi_ref[:, cwid:], preferred_element_type=F32), gv_ref[...])
    if emit_v:
        cv_ref[...] = v
    vb = v.astype(BF16)
    for c in range(tm // chunk):
        rs = slice(c * chunk, (c + 1) * chunk)
        for gi in range(n_grp):
            ls = slice(gi * grp, (gi + 1) * grp)
            s = jnp.dot(ws_ref[gi], vb[rs, ls], preferred_element_type=F32) + bs_ref[gi]
            us_ref[rs, ls] = (u[rs, ls] * s).astype(BF16)
    mix = jnp.dot(us_ref[...], wo_ref[...], preferred_element_type=F32)
    x1 = x + mod_ref[2] * mix
    x2 = _ffn_residual(x1, mod_ref, g2_ref[...], w1_ref, w2_ref, n_chunk)
    o_ref[...] = _rms(x2, gf_ref[...])


def _gmlp_layer(x, mods, g1, g2, w_in, g_v, ws_mat, bs_full, w_out, w1, w2, g_final, layer,
                rows_per_group, tm, emit_v):
    n, d = x.shape
    cwid = g_v.shape[1]
    row = lambda r: (r, 0)
    out_shape = [jax.ShapeDtypeStruct((n, d), F32)]
    out_specs = [pl.BlockSpec((tm, d), row)]
    if emit_v:
        out_shape.append(jax.ShapeDtypeStruct((n, cwid), F32))
        out_specs.append(pl.BlockSpec((tm, cwid), row))
    return pl.pallas_call(
        functools.partial(_gmlp_kernel, 4, emit_v),
        grid=(n // tm,),
        in_specs=[
            pl.BlockSpec((tm, d), row),
            _mod_spec(mods, layer, rows_per_group, tm),
            _resident(g1.shape),
            _resident(g2.shape),
            _resident(w_in.shape),
            _resident(g_v.shape),
            _resident(ws_mat.shape),
            _resident(bs_full.shape),
            _resident(w_out.shape),
            _resident(w1.shape),
            _resident(w2.shape),
            _resident(g_final.shape),
        ],
        out_specs=out_specs,
        out_shape=out_shape,
        scratch_shapes=[pltpu.VMEM((tm, cwid), BF16)],
        compiler_params=_params(("arbitrary",)),
        name="gmlp_layer",
    )(x, mods, g1, g2, w_in, g_v, ws_mat, bs_full, w_out, w1, w2, g_final)


def _alibi_slopes(n_heads):
    return jnp.exp2(-8.0 * jnp.arange(1, n_heads + 1, dtype=F32) / n_heads)


def kernel(x_prompt, x_sample, cache_k, cache_v, state_conv, page_table, c_prompt, c_sample,
           norm_g, w_ada, b_ada, w_in_ab, conv_w, w_out_ab, w_in_c, g_v, w_s, b_s, w_out_c,
           w_ff1, w_ff2, g_final):
    batch, seq, d = x_prompt.shape
    bd, tq, _ = x_sample.shape
    n_ab, n_pool, page, n_heads, hd = cache_k.shape
    aw = n_heads * hd
    cw = state_conv.shape[-1]
    n_c, n_grp, chunk, _ = w_s.shape
    cwid = g_v.shape[-1]
    depth = norm_g.shape[0]
    assert depth == 2 and n_ab == 1 and n_c == 1 and hd * 2 == LANES
    assert tq <= chunk and chunk % tq == 0 and (bd * tq) % chunk == 0
    n_p, n_s = batch * seq, bd * tq
    tm = 512 if seq % 512 == 0 else chunk

    wb = lambda w: w.astype(BF16)
    w_in_ab_b, w_out_ab_b = wb(w_in_ab[0]), wb(w_out_ab[0])
    w_in_c_b, w_out_c_b = wb(w_in_c[0]), wb(w_out_c[0])
    w_ff1_b, w_ff2_b = wb(w_ff1), wb(w_ff2)

    bp = -(-(batch + bd) // SUBLANES) * SUBLANES
    c_all = jnp.concatenate([c_prompt, c_sample, jnp.zeros((bp - batch - bd, d), F32)], axis=0)
    ada = _ada_terms(c_all, w_ada, b_ada)
    mods_p = ada[:, :batch].reshape(depth, batch, 6, 1, d)
    mods_s = jnp.repeat(ada[:, batch:batch + bd].reshape(depth, bd, 6, d), tq, axis=1)
    mods_s = mods_s.transpose(0, 2, 1, 3)

    slopes = _alibi_slopes(n_heads)
    slope_tab = jnp.broadcast_to(slopes[:, None], (n_heads, LANES))
    g = lambda l, j: norm_g[l, j].reshape(1, d)

    tril = jnp.tril(jnp.ones((chunk, chunk), bool))
    ws_p = jnp.where(tril, w_s[0], 0.0).astype(BF16)
    bs_p = jnp.broadcast_to(b_s[0][:, :, None], (n_grp, chunk, cwid // n_grp))
    ws_small = jnp.where(tril[:tq, :tq], w_s[0][:, :tq, :tq], 0.0)
    eye = jnp.eye(chunk // tq, dtype=F32)
    ws_s = jnp.einsum("ab,gts->gatbs", eye, ws_small).reshape(n_grp, chunk, chunk).astype(BF16)
    bs_s = jnp.broadcast_to(jnp.tile(b_s[0][:, :tq], (1, chunk // tq))[:, :, None],
                            (n_grp, chunk, cwid // n_grp))

    xp = x_prompt.reshape(n_p, d)
    prev0 = jnp.zeros((batch, CONV_W - 1, cw), F32)
    qp, kp_b, kp_t, vp_t, km_p, gated_p, conv_p = _front_prompt(
        xp, mods_p, g(0, 0), w_in_ab_b, conv_w[0], prev0, 0, tm, n_heads, page)
    attn_p = _moba_prompt(qp, kp_b, vp_t, km_p.reshape(n_p // MOBA_BLOCK, aw), slopes, batch)
    xp = _back_ab(xp, attn_p, gated_p, mods_p, g(0, 1), w_out_ab_b, w_ff1_b[0], w_ff2_b[0], 0,
                  seq, tm)
    (yp,) = _gmlp_layer(xp, mods_p, g(1, 0), g(1, 1), w_in_c_b, g_v, ws_p, bs_p, w_out_c_b,
                        w_ff1_b[1], w_ff2_b[1], g_final.reshape(1, d), 1, seq, tm, False)

    pool_pages = lambda c: c.transpose(0, 1, 3, 4, 2).reshape(n_pool, aw, page)
    xs = x_sample.reshape(n_s, d)
    st = state_conv[0]
    pad = lambda a: jnp.concatenate([a, jnp.zeros((bd, tq - a.shape[1], cw), F32)], axis=1)
    pe1 = pad(st[:, 1:2]).reshape(n_s, cw)
    pe2 = pad(st).reshape(n_s, cw)
    qs, ks, vs, gated_s, u_s = _front_sample(xs, mods_s, g(0, 0), w_in_ab_b, conv_w[0], pe1, pe2,
                                             0, tq)
    attn_s = _moba_sample(qs.reshape(bd, tq, aw), ks.reshape(bd, tq, aw), vs.reshape(bd, tq, aw),
                          pool_pages(cache_k), pool_pages(cache_v), page_table, slope_tab,
                          n_heads)
    xs = _back_ab(xs, attn_s.reshape(n_s, aw), gated_s, mods_s, g(0, 1), w_out_ab_b, w_ff1_b[0],
                  w_ff2_b[0], 0, n_s, chunk)
    ys, cv_s = _gmlp_layer(xs, mods_s, g(1, 0), g(1, 1), w_in_c_b, g_v, ws_s, bs_s, w_out_c_b,
                           w_ff1_b[1], w_ff2_b[1], g_final.reshape(1, d), 1, n_s, chunk, True)

    return (
        yp.reshape(batch, seq, d),
        ys.reshape(bd, tq, d),
        kp_t.transpose(0, 3, 1, 2)[None],
        vp_t.transpose(0, 3, 1, 2)[None],
        conv_p[None],
        ks.reshape(1, bd, tq, n_heads, hd),
        vs.reshape(1, bd, tq, n_heads, hd),
        u_s.reshape(bd, tq, cw)[:, tq - (CONV_W - 1):][None],
        cv_s.reshape(1, bd, tq, cwid),
    )
```

```python
import functools

import jax
import jax.numpy as jnp
from jax import lax
from jax.experimental import pallas as pl
from jax.experimental.pallas import tpu as pltpu

F32 = jnp.float32
BF16 = jnp.bfloat16

EPS = 1e-6
MOBA_BLOCK = 256
MOBA_BLOCK_SHIFT = 8
MOBA_TOPK = 3
CONV_W = 3
LANES = 128
SUBLANES = 8
ONES_ROWS = 2 * SUBLANES
NEG_BIG = -(2.0 ** 100)
VMEM_LIMIT_BYTES = 56 * 1024 * 1024

_NT = (((1,), (1,)), ((), ()))


def _params(sem):
    return pltpu.CompilerParams(dimension_semantics=sem, vmem_limit_bytes=VMEM_LIMIT_BYTES)


def _resident(shape):
    n = len(shape)
    return pl.BlockSpec(shape, lambda *_: (0,) * n, pipeline_mode=pl.Buffered(1))


def _rms(x, g):
    return x * lax.rsqrt(jnp.mean(x * x, axis=-1, keepdims=True) + EPS) * g


def _split_bf16(x):
    hi = x.astype(BF16)
    lo = (x - hi.astype(F32)).astype(BF16)
    return hi, lo


def _ada_kernel(c_ref, w_ref, b_ref, o_ref):
    c = c_ref[...]
    s = (c * jax.nn.sigmoid(c)).astype(BF16)
    o_ref[0] = jnp.dot(s, w_ref[0].astype(BF16), preferred_element_type=F32) + b_ref[0]


def _ada_terms(c_all, w_ada, b_ada):
    n_layers, d, d6 = w_ada.shape
    bp = c_all.shape[0]
    tn = d6 // 4
    return pl.pallas_call(
        _ada_kernel,
        grid=(n_layers, d6 // tn),
        in_specs=[
            pl.BlockSpec((bp, d), lambda l, j: (0, 0)),
            pl.BlockSpec((1, d, tn), lambda l, j: (l, 0, j)),
            pl.BlockSpec((1, 1, tn), lambda l, j: (l, 0, j)),
        ],
        out_specs=pl.BlockSpec((1, bp, tn), lambda l, j: (l, 0, j)),
        out_shape=jax.ShapeDtypeStruct((n_layers, bp, d6), F32),
        compiler_params=_params(("arbitrary", "arbitrary")),
        name="ada_terms",
    )(c_all, w_ada, b_ada.reshape(n_layers, 1, d6))


def _front_kernel(multi_seq, seq_len, *refs):
    if multi_seq:
        (x_ref, mod_ref, g_ref, w_ref, cw_ref, pe1_ref, pe2_ref,
         q_ref, k_ref, v_ref, gated_ref, u_ref) = refs
    else:
        (x_ref, mod_ref, g_ref, w_ref, cw_ref, prev_ref,
         q_ref, kb_ref, kt_ref, vt_ref, km_ref, gated_ref, cs_ref, carry_ref) = refs
    tm = x_ref.shape[0]
    cw = gated_ref.shape[1]
    h = _rms(x_ref[...], g_ref[...]) * (1.0 + mod_ref[1]) + mod_ref[0]
    hb = h.astype(BF16)

    def proj(j):
        return jnp.dot(hb, w_ref[:, j * cw:(j + 1) * cw], preferred_element_type=F32)

    q_ref[...] = proj(0)
    if multi_seq:
        k_ref[...] = proj(1)
        v_ref[...] = proj(2)
    else:
        n_page, n_heads, hd, page = kt_ref.shape
        yk = proj(1)
        kb_ref[...] = yk.astype(BF16)
        for j in range(tm // MOBA_BLOCK):
            km_ref[j] = jnp.mean(yk[j * MOBA_BLOCK:(j + 1) * MOBA_BLOCK, :], axis=0, keepdims=True)
        yv = proj(2)
        for y, t_ref in ((yk, kt_ref), (yv, vt_ref)):
            yt = y.T
            for j in range(n_page):
                t_ref[j] = yt[:, j * page:(j + 1) * page].reshape(n_heads, hd, page)
    u = proj(4) * proj(5)
    rows = lax.broadcasted_iota(jnp.int32, u.shape, 0)
    u1 = pltpu.roll(u, 1, axis=0)
    u2 = pltpu.roll(u, 2, axis=0)
    if multi_seq:
        assert seq_len & (seq_len - 1) == 0
        t = rows & (seq_len - 1)
        u1 = jnp.where(t == 0, pe1_ref[...], u1)
        u2 = jnp.where(t < 2, pe2_ref[...], u2)
        u_ref[...] = u
    else:
        i = pl.program_id(1)

        @pl.when(i == 0)
        def _():
            carry_ref[SUBLANES - 2:SUBLANES, :] = prev_ref[...]

        p0 = carry_ref[SUBLANES - 2:SUBLANES - 1, :]
        p1 = carry_ref[SUBLANES - 1:SUBLANES, :]
        u1 = jnp.where(rows == 0, p1, u1)
        u2 = jnp.where(rows == 0, p0, jnp.where(rows == 1, p1, u2))
        carry_ref[...] = u[tm - SUBLANES:tm, :]

        @pl.when(i == pl.num_programs(1) - 1)
        def _():
            cs_ref[...] = u[tm - 2:tm, :]

    yc = u2 * cw_ref[0:1, :] + u1 * cw_ref[1:2, :] + u * cw_ref[2:3, :]
    gated_ref[...] = (proj(3) * yc).astype(BF16)


def _front_prompt(x, mods, g, w_in, conv_w, prev, layer, tm, n_heads, page):
    n, d = x.shape
    nb = prev.shape[0]
    cw = prev.shape[2]
    nt = n // nb // tm
    assert tm % page == 0 and tm % MOBA_BLOCK == 0
    row = lambda b, i: (b * nt + i, 0)
    pages = (tm // page, n_heads, cw // n_heads, page)
    page_map = lambda b, i: (b * nt + i, 0, 0, 0)
    n_mean = tm // MOBA_BLOCK
    return pl.pallas_call(
        functools.partial(_front_kernel, False, 0),
        grid=(nb, nt),
        in_specs=[
            pl.BlockSpec((tm, d), row),
            pl.BlockSpec((None, None, 6, 1, d), lambda b, i: (layer, b, 0, 0, 0)),
            _resident(g.shape),
            _resident(w_in.shape),
            _resident(conv_w.shape),
            pl.BlockSpec((None, CONV_W - 1, cw), lambda b, i: (b, 0, 0)),
        ],
        out_specs=[
            pl.BlockSpec((tm, cw), row),
            pl.BlockSpec((tm, cw), row),
            pl.BlockSpec(pages, page_map),
            pl.BlockSpec(pages, page_map),
            pl.BlockSpec((n_mean, 1, cw), lambda b, i: (b * nt + i, 0, 0)),
            pl.BlockSpec((tm, cw), row),
            pl.BlockSpec((None, CONV_W - 1, cw), lambda b, i: (b, 0, 0)),
        ],
        out_shape=[
            jax.ShapeDtypeStruct((n, cw), F32),
            jax.ShapeDtypeStruct((n, cw), BF16),
            jax.ShapeDtypeStruct((n // page,) + pages[1:], F32),
            jax.ShapeDtypeStruct((n // page,) + pages[1:], F32),
            jax.ShapeDtypeStruct((n // MOBA_BLOCK, 1, cw), F32),
            jax.ShapeDtypeStruct((n, cw), BF16),
            jax.ShapeDtypeStruct(prev.shape, F32),
        ],
        scratch_shapes=[pltpu.VMEM((SUBLANES, cw), F32)],
        compiler_params=_params(("arbitrary", "arbitrary")),
        name="front_prompt",
    )(x, mods, g, w_in, conv_w, prev)


def _front_sample(x, mods_rows, g, w_in, conv_w, pe1, pe2, layer, seq_len):
    n, d = x.shape
    cw = pe1.shape[1]
    full = lambda i: (0, 0)
    return pl.pallas_call(
        functools.partial(_front_kernel, True, seq_len),
        grid=(1,),
        in_specs=[
            pl.BlockSpec((n, d), full),
            pl.BlockSpec((None, 6, n, d), lambda i: (layer, 0, 0, 0)),
            _resident(g.shape),
            _resident(w_in.shape),
            _resident(conv_w.shape),
            pl.BlockSpec((n, cw), full),
            pl.BlockSpec((n, cw), full),
        ],
        out_specs=[pl.BlockSpec((n, cw), full)] * 5,
        out_shape=[
            jax.ShapeDtypeStruct((n, cw), F32),
            jax.ShapeDtypeStruct((n, cw), F32),
            jax.ShapeDtypeStruct((n, cw), F32),
            jax.ShapeDtypeStruct((n, cw), BF16),
            jax.ShapeDtypeStruct((n, cw), F32),
        ],
        compiler_params=_params(("arbitrary",)),
        name="front_sample",
    )(x, mods_rows, g, w_in, conv_w, pe1, pe2)


def _moba_item(step, lag, n_items, n_half, n_pair):
    item = jnp.clip(step - lag, 0, n_items - 1)
    seq_pair = item // n_half
    return seq_pair // n_pair, seq_pair % n_pair, item % n_half


def _moba_prompt_kernel(n_blk, n_pair, slopes_ref, qa_ref, qb_ref, k_ref, v_ref, km_ref, o_ref,
                        kb_ref, vt_ref, qf_ref, st_ref, um_ref, m_ref, acc_ref):
    step = pl.program_id(0)
    n_items = pl.num_programs(0) - 1
    n_half = n_blk // 2
    blk = MOBA_BLOCK
    seq = k_ref.shape[0]
    hd = v_ref.shape[2]
    page = v_ref.shape[3]
    ppb = blk // page
    n_prev = n_blk - 1
    scale = float(hd) ** -0.5
    b, p, j = _moba_item(step, 0, n_items, n_half, n_pair)
    b_old, p_old, j_old = _moba_item(step, 1, n_items, n_half, n_pair)
    slot = (b * n_pair + p) % 2
    slot_old = (b_old * n_pair + p_old) % 2
    tiles = (j, n_blk - 1 - j)
    tiles_old = (j_old, n_blk - 1 - j_old)

    @pl.when(step == 0)
    def _():
        r = lax.broadcasted_iota(jnp.int32, (seq, LANES), 0)
        c = lax.broadcasted_iota(jnp.int32, (seq, LANES), 1)
        n = lax.shift_right_logical(r, MOBA_BLOCK_SHIFT)
        off = r & (blk - 1)
        feat = jnp.where(c == n, 1.0,
                         jnp.where(c == n_blk, n.astype(F32),
                                   jnp.where(c == n_blk + 1, off.astype(F32),
                                             jnp.where(c == n_blk + 2, 1.0, 0.0))))
        kb_ref[:, LANES:2 * LANES] = feat.astype(BF16)
        st_ref[...] = jnp.zeros_like(st_ref)
        m_ref[...] = jnp.zeros_like(m_ref)
        vt_ref[:, :, :, hd:, :] = jnp.ones(vt_ref.shape[:3] + (ONES_ROWS, blk), BF16)

    @pl.when((j == 0) & (step < n_items))
    def _():
        kb_ref[:, 0:LANES] = k_ref[...]
        for n in range(n_blk):
            for e in range(2):
                vt_ref[slot, n, e, 0:hd, :] = jnp.concatenate(
                    [v_ref[n * ppb + g, e] for g in range(ppb)], axis=1).astype(BF16)

    _moba_setup(n_blk, scale, tiles, p, slopes_ref, qa_ref, qb_ref, km_ref, qf_ref)

    n_grp = blk // SUBLANES
    wide = 2 * blk
    feat_lane = lax.broadcasted_iota(jnp.int32, (1, 2 * LANES), 1) - LANES
    ind_off = jnp.where((feat_lane >= 0) & (feat_lane < n_blk), 0.0, 1.0).astype(BF16)
    key_row = lax.broadcasted_iota(jnp.int32, (blk, wide), 0)
    q_col = lax.broadcasted_iota(jnp.int32, (blk, wide), 1) & (blk - 1)

    def unit_ids(u, jj, item_tiles):
        if u >= n_prev:
            return u - n_prev, item_tiles[u - n_prev]
        is_a = u < jj
        return jnp.where(is_a, 0, 1), jnp.where(is_a, u, u - jj)

    acc_ref[...] = jnp.zeros_like(acc_ref)
    for u in range(n_prev + 2):
        t_old, n_old = unit_ids(u, j_old, tiles_old)
        pr = jnp.exp(st_ref[u].reshape(n_grp, SUBLANES, wide) - m_ref[t_old][None])
        pb = pr.reshape(blk, wide).astype(BF16)
        for e in range(2):
            acc_ref[t_old, e] = acc_ref[t_old, e] + jnp.dot(
                vt_ref[slot_old, n_old, e], pb[:, e * blk:(e + 1) * blk],
                preferred_element_type=F32)
        t_new, n_new = unit_ids(u, j, tiles)
        own = u >= n_prev
        kblk = kb_ref[pl.ds(pl.multiple_of(n_new * blk, blk), blk), :]
        if own:
            kblk = kblk * ind_off
        s = jnp.dot(kblk, qf_ref[t_new], preferred_element_type=F32)
        if own:
            s = jnp.where(key_row <= q_col, s, NEG_BIG)
        st_ref[u] = s
        um_ref[u] = jnp.max(s.reshape(n_grp, SUBLANES, wide), axis=0)

    for t in range(2):
        out_t = jnp.concatenate(
            [acc_ref[t, e, 0:hd, :] / acc_ref[t, e, hd:hd + 1, :] for e in range(2)], axis=0)
        o_ref[pl.ds(pl.multiple_of(tiles_old[t] * blk, blk), blk), :] = (
            out_t.T.astype(o_ref.dtype))
    m_new = [um_ref[n_prev], um_ref[n_prev + 1]]
    for u in range(n_prev):
        val = um_ref[u]
        is_a = u < j
        m_new[0] = jnp.maximum(m_new[0], jnp.where(is_a, val, -jnp.inf))
        m_new[1] = jnp.maximum(m_new[1], jnp.where(is_a, -jnp.inf, val))
    for t in range(2):
        m_ref[t] = jnp.broadcast_to(jnp.max(m_new[t], axis=0, keepdims=True), m_new[t].shape)


def _moba_setup(n_blk, scale, tiles, pair, slopes_ref, qa_ref, qb_ref, km_ref, qf_ref):
    blk = MOBA_BLOCK
    hd = LANES // 2
    km = km_ref[...]
    lane = lax.broadcasted_iota(jnp.int32, (1, LANES), 1)
    km_heads = jnp.concatenate([km * (lane < hd).astype(F32), km * (lane >= hd).astype(F32)],
                               axis=0)
    km_hi, km_lo = _split_bf16(km_heads)
    km_hi_lo = jnp.concatenate([km_hi, km_lo], axis=0)
    qrow = lax.broadcasted_iota(jnp.int32, (LANES, blk), 0)
    head_rows = [(qrow < hd).astype(F32), (qrow >= hd).astype(F32)]
    blk_row = lax.broadcasted_iota(jnp.int32, (n_blk, blk), 0)
    blk_row_f = blk_row.astype(F32)
    for t, q_ref in enumerate((qa_ref, qb_ref)):
        qt = q_ref[...].T
        qt_hi, qt_lo = _split_bf16(qt)
        g_hi = jnp.dot(km_hi_lo, qt_hi, preferred_element_type=F32)
        gate_both = (g_hi[0:2 * n_blk] + g_hi[2 * n_blk:]
                     + jnp.dot(km_hi, qt_lo, preferred_element_type=F32))
        tile_f = tiles[t].astype(F32)
        valid = blk_row < tiles[t]
        for e in range(2):
            slope = slopes_ref[2 * pair + e]
            gate_t = jnp.where(valid, gate_both[e * n_blk:(e + 1) * n_blk], -jnp.inf)
            taken = jnp.zeros((n_blk, blk), F32)
            for _ in range(MOBA_TOPK):
                is_max = gate_t == jnp.max(gate_t, axis=0, keepdims=True)
                first = jnp.min(jnp.where(is_max, blk_row_f, float(n_blk)), axis=0,
                                keepdims=True)
                pick = blk_row_f == first
                taken = jnp.where(pick, 1.0, taken)
                gate_t = jnp.where(pick, -jnp.inf, gate_t)
            mask_t = jnp.where(valid, jnp.where(taken > 0.0, 0.0, NEG_BIG), NEG_BIG)
            consts = jnp.where(qrow == n_blk, slope * blk,
                               jnp.where(qrow == n_blk + 1, slope,
                                         jnp.where(qrow == n_blk + 2, -slope * blk * tile_f, 0.0)))
            feat_t = jnp.concatenate([mask_t, jnp.zeros((LANES - n_blk, blk), F32)],
                                     axis=0) + consts
            q_e = (qt * (head_rows[e] * scale)).astype(BF16)
            qf_ref[t, :, e * blk:(e + 1) * blk] = jnp.concatenate(
                [q_e, feat_t.astype(BF16)], axis=0)


def _moba_prompt(q, k_rows, v_pages, k_mean, slopes, batch):
    n, aw = q.shape
    _, n_heads, hd, page = v_pages.shape
    seq = n // batch
    blk = MOBA_BLOCK
    n_blk = seq // blk
    assert seq % blk == 0 and blk % page == 0 and 2 * hd == LANES
    assert n_blk % SUBLANES == 0 and n_blk + 3 <= LANES
    n_pair = aw // LANES
    n_half = n_blk // 2
    n_items = batch * n_pair * n_half
    n_unit = n_blk + 1

    def item_map(lag, fn):
        return lambda s, _: fn(*_moba_item(s, lag, n_items, n_half, n_pair))

    return pl.pallas_call(
        functools.partial(_moba_prompt_kernel, n_blk, n_pair),
        grid_spec=pltpu.PrefetchScalarGridSpec(
            num_scalar_prefetch=1,
            grid=(n_items + 1,),
            in_specs=[
                pl.BlockSpec((blk, LANES), item_map(0, lambda b, p, j: (b * n_blk + j, p))),
                pl.BlockSpec((blk, LANES),
                             item_map(0, lambda b, p, j: (b * n_blk + n_blk - 1 - j, p))),
                pl.BlockSpec((seq, LANES), item_map(0, lambda b, p, j: (b, p))),
                pl.BlockSpec((seq // page, 2, hd, page),
                             item_map(0, lambda b, p, j: (b, p, 0, 0))),
                pl.BlockSpec((n_blk, LANES), item_map(0, lambda b, p, j: (b, p))),
            ],
            out_specs=pl.BlockSpec((seq, LANES), item_map(1, lambda b, p, j: (b, p))),
            scratch_shapes=[
                pltpu.VMEM((seq, 2 * LANES), BF16),
                pltpu.VMEM((2, n_blk, 2, hd + ONES_ROWS, blk), BF16),
                pltpu.VMEM((2, 2 * LANES, 2 * blk), BF16),
                pltpu.VMEM((n_unit, blk, 2 * blk), F32),
                pltpu.VMEM((n_unit, SUBLANES, 2 * blk), F32),
                pltpu.VMEM((2, SUBLANES, 2 * blk), F32),
                pltpu.VMEM((2, 2, hd + ONES_ROWS, blk), F32),
            ],
        ),
        out_shape=jax.ShapeDtypeStruct((n, aw), BF16),
        compiler_params=_params(("arbitrary",)),
        name="moba_prompt",
    )(slopes, q, q, k_rows, v_pages, k_mean)


def _moba_sample_kernel(n_pages, n_heads, pt_ref, q_ref, kn_ref, vn_ref, slope_ref,
                        ck_hbm, cv_hbm, o_ref, kbuf, vsel, s_ref, km_ref, bias_ref, idx_ref,
                        idx_smem, stat_ref, own_ref, orow_ref, ksem, vsem, isem):
    step = pl.program_id(0)
    n_samples = pl.num_programs(0) - 1
    blk = MOBA_BLOCK
    tq, aw = q_ref.shape
    page = kbuf.shape[3]
    ppb = blk // page
    hd = aw // n_heads
    rows = tq * n_heads
    past = n_pages * page
    n_blk = past // blk
    n_sel = min(MOBA_TOPK, n_blk)
    slot = step % 2
    slot_old = 1 - slot
    feat0 = n_blk

    def k_copy(sample, pg):
        return pltpu.make_async_copy(ck_hbm.at[pt_ref[sample, pg]], kbuf.at[sample % 2, pg],
                                     ksem.at[sample % 2])

    def for_pages(fn):
        def go(pg, c):
            fn(pg)
            return c
        lax.fori_loop(0, n_pages, go, 0)

    def idx_copy(sl):
        return pltpu.make_async_copy(idx_ref.at[sl], idx_smem.at[sl], isem)

    def v_copies(sample, sl, row_range=range(rows)):
        for r in row_range:
            for j in range(n_sel):
                first_page = idx_smem[sl, r, j] * ppb
                for g in range(ppb):
                    yield pltpu.make_async_copy(
                        cv_hbm.at[pt_ref[sample, first_page + g],
                                  pl.ds((r % n_heads) * hd, hd), :],
                        vsel.at[sl, r * n_sel + j, g], vsem.at[sl])

    lane = lax.broadcasted_iota(jnp.int32, (n_heads, aw), 1)
    hrow = lax.broadcasted_iota(jnp.int32, (n_heads, aw), 0)
    head_mask = jnp.where(lane >= hrow * hd, jnp.where(lane < (hrow + 1) * hd, 1.0, 0.0), 0.0)
    hm_rows = jnp.concatenate([head_mask] * tq, axis=0)

    @pl.when(step == 0)
    def _():
        for_pages(lambda pg: k_copy(0, pg).start())
        blk_id = lax.broadcasted_iota(jnp.int32, bias_ref.shape, 0)
        f = lax.broadcasted_iota(jnp.int32, bias_ref.shape, 1)
        off = lax.broadcasted_iota(jnp.int32, bias_ref.shape, 2)
        bias_ref[...] = jnp.where(
            f == blk_id, 1.0,
            jnp.where(f == feat0, (blk_id * blk).astype(F32),
                      jnp.where(f == feat0 + 1, off.astype(F32),
                                jnp.where((f == feat0 + 2) | (f == feat0 + 3), 1.0, 0.0)))
        ).astype(BF16)

    @pl.when(step < n_samples)
    def _():
        _moba_sample_keys(step, slot, n_pages, n_heads, n_sel, hm_rows, q_ref, kn_ref, vn_ref,
                          slope_ref, kbuf, s_ref, km_ref, bias_ref, idx_ref, stat_ref, own_ref,
                          k_copy, for_pages)
        idx_copy(slot).start()

    def values_and_gathers(do_values, do_gathers):
        if do_values:
            for copy in v_copies(step - 1, slot_old):
                copy.wait()
            inv_l = 1.0 / stat_ref[slot_old]
        hidden_rows = rows // 4 if do_values else 0
        folded = []
        for r in range(rows):
            if do_values:
                acc = jnp.zeros((hd, blk), F32)
                for j in range(n_sel):
                    pr = s_ref[slot_old, idx_smem[slot_old, r, j], r:r + 1, :]
                    v_blk = jnp.concatenate(
                        [vsel[slot_old, r * n_sel + j, g] for g in range(ppb)], axis=1)
                    acc = acc + pr * v_blk
                fold = functools.reduce(
                    jnp.add, [acc[:, c * LANES:(c + 1) * LANES] for c in range(blk // LANES)])
                folded.append(fold * inv_l[r:r + 1, 0:1])
                if r % n_heads == n_heads - 1:
                    tok = r // n_heads
                    tok_t = jnp.concatenate(folded, axis=0).T
                    folded = []
                    orow_ref[tok:tok + 1, :] = (jnp.sum(tok_t, axis=0, keepdims=True)
                                                + own_ref[slot_old, tok:tok + 1, :])
            if do_gathers and r >= hidden_rows:
                if r == hidden_rows:
                    idx_copy(slot).wait()
                first = 0 if r == hidden_rows else r
                for copy in v_copies(step, slot, range(first, r + 1)):
                    copy.start()
        if do_values:
            o_ref[...] = orow_ref[0:tq, :].astype(o_ref.dtype)

    pl.when(step == 0)(lambda: values_and_gathers(False, True))
    pl.when((step > 0) & (step < n_samples))(lambda: values_and_gathers(True, True))
    pl.when(step == n_samples)(lambda: values_and_gathers(True, False))


def _moba_sample_keys(b, slot, n_pages, n_heads, n_sel, hm_rows, q_ref, kn_ref, vn_ref,
                      slope_ref, kbuf, s_ref, km_ref, bias_ref, idx_ref, stat_ref, own_ref,
                      k_copy, for_pages):
    blk = MOBA_BLOCK
    tq, aw = q_ref.shape
    page = kbuf.shape[3]
    ppb = blk // page
    hd = aw // n_heads
    rows = tq * n_heads
    past = n_pages * page
    n_blk = past // blk
    feat0 = n_blk
    q = q_ref[...]
    q_rows = jnp.concatenate(
        [jnp.broadcast_to(q[t:t + 1, :], (n_heads, aw)) for t in range(tq)], axis=0) * hm_rows
    qs_bf = (q_rows * (float(hd) ** -0.5)).astype(BF16)
    slope = jnp.concatenate([slope_ref[:, 0:1]] * tq, axis=0)
    t_row = jnp.concatenate([jnp.full((n_heads, 1), t, jnp.int32) for t in range(tq)], axis=0)

    for_pages(lambda pg: k_copy(b, pg).wait())

    @pl.when(b + 2 < pl.num_programs(0))
    def _():
        for_pages(lambda pg: k_copy(b + 1, pg).start())

    km_ref[...] = jnp.zeros_like(km_ref)
    for n in range(n_blk):
        pages_n = [kbuf[slot, n * ppb + g] for g in range(ppb)]
        km_ref[:, n:n + 1] = jnp.sum(functools.reduce(jnp.add, pages_n), axis=1,
                                     keepdims=True) * (1.0 / blk)
        s_ref[slot, n] = jnp.dot(qs_bf, jnp.concatenate(pages_n, axis=1).astype(BF16),
                                 preferred_element_type=F32)
    km_t = km_ref[...]

    q_hi, q_lo = _split_bf16(q_rows)
    km_hi, km_lo = _split_bf16(km_t)
    gate = (jnp.dot(q_hi, km_hi, preferred_element_type=F32)
            + jnp.dot(q_hi, km_lo, preferred_element_type=F32)
            + jnp.dot(q_lo, km_hi, preferred_element_type=F32))
    blk_lane = lax.broadcasted_iota(jnp.int32, gate.shape, 1)
    gate = jnp.where(blk_lane < n_blk, gate, -jnp.inf)
    lane_f = blk_lane.astype(F32)
    taken = jnp.zeros(gate.shape, F32)
    picked = jnp.zeros(gate.shape, F32)
    for j in range(n_sel):
        is_max = gate == jnp.max(gate, axis=1, keepdims=True)
        first = jnp.min(jnp.where(is_max, lane_f, float(LANES)), axis=1, keepdims=True)
        pick = lane_f == first
        taken = jnp.where(pick, 1.0, taken)
        gate = jnp.where(pick, -jnp.inf, gate)
        picked = jnp.where(blk_lane == j, first, picked)
    idx_ref[slot] = picked.astype(jnp.int32)
    slope_l = jnp.broadcast_to(slope, gate.shape)
    t_l = jnp.broadcast_to(t_row, gate.shape).astype(F32)
    feat = jnp.where(
        blk_lane < n_blk, jnp.where(taken > 0.0, 0.0, NEG_BIG),
        jnp.where((blk_lane == feat0) | (blk_lane == feat0 + 1), slope_l,
                  jnp.where(blk_lane == feat0 + 2, -slope_l * past,
                            jnp.where(blk_lane == feat0 + 3, -slope_l * t_l, 0.0))))
    feat_bf = feat.astype(BF16)

    kn = kn_ref[...]
    vn = vn_ref[...]
    q_sc = q_rows * (float(hd) ** -0.5)
    own = []
    for t in range(tq):
        so = jnp.sum(q_sc * kn[t:t + 1, :], axis=1, keepdims=True)
        so = so - slope * (t_row - t).astype(F32)
        own.append(jnp.where(t_row >= t, so, NEG_BIG))
    m_elem = jnp.full((rows, blk), NEG_BIG, F32)
    for n in range(n_blk):
        s = s_ref[slot, n] + jnp.dot(feat_bf, bias_ref[n], preferred_element_type=F32)
        s_ref[slot, n] = s
        m_elem = jnp.maximum(m_elem, s)
    m_run = jnp.maximum(functools.reduce(jnp.maximum, own),
                        jnp.max(m_elem, axis=1, keepdims=True))

    acc = jnp.zeros((rows, aw), F32)
    l_run = jnp.zeros((rows, 1), F32)
    for t in range(tq):
        po = jnp.exp(own[t] - m_run)
        l_run = l_run + po
        acc = acc + po * vn[t:t + 1, :]
    m_l = jnp.broadcast_to(m_run, (rows, blk))
    l_elem = jnp.zeros((rows, blk), F32)
    for n in range(n_blk):
        pr = jnp.exp(s_ref[slot, n] - m_l)
        s_ref[slot, n] = pr
        l_elem = l_elem + pr
    l_run = l_run + jnp.sum(l_elem, axis=1, keepdims=True)
    stat_ref[slot] = jnp.broadcast_to(l_run, gate.shape)
    own_rows = acc * hm_rows / l_run
    own_ref[slot, 0:tq, :] = jnp.sum(own_rows.reshape(tq, n_heads, aw), axis=1)


def _moba_sample(q, k_new, v_new, cache_k, cache_v, page_table, slope_tab, n_heads):
    bd, tq, aw = q.shape
    n_pages = page_table.shape[1]
    page = cache_k.shape[2]
    past = n_pages * page
    assert MOBA_BLOCK % page == 0 and past % MOBA_BLOCK == 0 and page == LANES
    assert past // MOBA_BLOCK + 4 <= LANES
    rows = tq * n_heads
    n_blk = past // MOBA_BLOCK
    n_sel = min(MOBA_TOPK, n_blk)
    hd = aw // n_heads
    assert tq <= LANES and rows % SUBLANES == 0
    key_tok = lambda s, pt: (jnp.minimum(s, bd - 1), 0, 0)
    val_tok = lambda s, pt: (jnp.maximum(s - 1, 0), 0, 0)
    return pl.pallas_call(
        functools.partial(_moba_sample_kernel, n_pages, n_heads),
        grid_spec=pltpu.PrefetchScalarGridSpec(
            num_scalar_prefetch=1,
            grid=(bd + 1,),
            in_specs=[
                pl.BlockSpec((None, tq, aw), key_tok),
                pl.BlockSpec((None, tq, aw), key_tok),
                pl.BlockSpec((None, tq, aw), key_tok),
                pl.BlockSpec(slope_tab.shape, lambda s, pt: (0, 0)),
                pl.BlockSpec(memory_space=pl.ANY),
                pl.BlockSpec(memory_space=pl.ANY),
            ],
            out_specs=pl.BlockSpec((None, tq, aw), val_tok),
            scratch_shapes=[
                pltpu.VMEM((2, n_pages, aw, page), F32),
                pltpu.VMEM((2, rows * n_sel, MOBA_BLOCK // page, hd, page), F32),
                pltpu.VMEM((2, n_blk, rows, MOBA_BLOCK), F32),
                pltpu.VMEM((aw, LANES), F32),
                pltpu.VMEM((n_blk, LANES, MOBA_BLOCK), BF16),
                pltpu.VMEM((2, rows, LANES), jnp.int32),
                pltpu.SMEM((2, rows, LANES), jnp.int32),
                pltpu.VMEM((2, rows, LANES), F32),
                pltpu.VMEM((2, SUBLANES, aw), F32),
                pltpu.VMEM((SUBLANES, aw), F32),
                pltpu.SemaphoreType.DMA((2,)),
                pltpu.SemaphoreType.DMA((2,)),
                pltpu.SemaphoreType.DMA(()),
            ],
        ),
        out_shape=jax.ShapeDtypeStruct((bd, tq, aw), BF16),
        compiler_params=_params(("arbitrary",)),
        name="moba_sample",
    )(page_table, q, k_new, v_new, slope_tab, cache_k, cache_v)


def _ffn_residual(x1, mod_ref, g, w1_ref, w2_ref, n_chunk):
    h = (_rms(x1, g) * (1.0 + mod_ref[4]) + mod_ref[3]).astype(BF16)
    ck = w1_ref.shape[1] // n_chunk
    acc = jnp.zeros(x1.shape, F32)
    for j in range(n_chunk):
        hid = jnp.dot(h, w1_ref[:, j * ck:(j + 1) * ck], preferred_element_type=F32)
        hid = jnp.square(jnp.maximum(hid, 0.0)).astype(BF16)
        acc = acc + jnp.dot(hid, w2_ref[j * ck:(j + 1) * ck, :], preferred_element_type=F32)
    return x1 + mod_ref[5] * acc


def _back_ab_kernel(n_chunk, x_ref, attn_ref, gated_ref, mod_ref, g_ref, wo_ref, w1_ref, w2_ref,
                    o_ref):
    aw = attn_ref.shape[1]
    mix = (jnp.dot(attn_ref[...], wo_ref[0:aw, :], preferred_element_type=F32)
           + jnp.dot(gated_ref[...], wo_ref[aw:, :], preferred_element_type=F32))
    x1 = x_ref[...] + mod_ref[2] * mix
    o_ref[...] = _ffn_residual(x1, mod_ref, g_ref[...], w1_ref, w2_ref, n_chunk)


def _mod_spec(mods, layer, n_rows_per_group, tm):
    d = mods.shape[-1]
    if mods.ndim == 5:
        nt = n_rows_per_group // tm
        return pl.BlockSpec((None, None, 6, 1, d), lambda r: (layer, r // nt, 0, 0, 0))
    return pl.BlockSpec((None, 6, tm, d), lambda r: (layer, 0, r, 0))


def _back_ab(x, attn, gated, mods, g, w_out, w1, w2, layer, rows_per_group, tm):
    n, d = x.shape
    aw = attn.shape[1]
    cw = gated.shape[1]
    row = lambda r: (r, 0)
    return pl.pallas_call(
        functools.partial(_back_ab_kernel, 4),
        grid=(n // tm,),
        in_specs=[
            pl.BlockSpec((tm, d), row),
            pl.BlockSpec((tm, aw), row),
            pl.BlockSpec((tm, cw), row),
            _mod_spec(mods, layer, rows_per_group, tm),
            _resident(g.shape),
            _resident(w_out.shape),
            _resident(w1.shape),
            _resident(w2.shape),
        ],
        out_specs=pl.BlockSpec((tm, d), row),
        out_shape=jax.ShapeDtypeStruct((n, d), F32),
        compiler_params=_params(("arbitrary",)),
        name="back_ab",
    )(x, attn, gated, mods, g, w_out, w1, w2)


def _gmlp_kernel(n_chunk, emit_v, x_ref, mod_ref, g1_ref, g2_ref, wi_ref, gv_ref, ws_ref, bs_ref,
                 wo_ref, w1_ref, w2_ref, gf_ref, *rest):
    if emit_v:
        o_ref, cv_ref, us_ref = rest
    else:
        o_ref, us_ref = rest
    tm = x_ref.shape[0]
    cwid = gv_ref.shape[1]
    n_grp, chunk, _ = ws_ref.shape
    grp = cwid // n_grp
    x = x_ref[...]
    hb = (_rms(x, g1_ref[...]) * (1.0 + mod_ref[1]) + mod_ref[0]).astype(BF16)
    u = jnp.dot(hb, wi_ref[:, 0:cwid], preferred_element_type=F32)
    v = _rms(jnp.dot(hb, wi_ref[:, cwid:], preferred_element_type=F32), gv_ref[...])
    if emit_v:
        cv_ref[...] = v
    vb = v.astype(BF16)
    for c in range(tm // chunk):
        rs = slice(c * chunk, (c + 1) * chunk)
        for gi in range(n_grp):
            ls = slice(gi * grp, (gi + 1) * grp)
            s = jnp.dot(ws_ref[gi], vb[rs, ls], preferred_element_type=F32) + bs_ref[gi]
            us_ref[rs, ls] = (u[rs, ls] * s).astype(BF16)
    mix = jnp.dot(us_ref[...], wo_ref[...], preferred_element_type=F32)
    x1 = x + mod_ref[2] * mix
    x2 = _ffn_residual(x1, mod_ref, g2_ref[...], w1_ref, w2_ref, n_chunk)
    o_ref[...] = _rms(x2, gf_ref[...])


def _gmlp_layer(x, mods, g1, g2, w_in, g_v, ws_mat, bs_full, w_out, w1, w2, g_final, layer,
                rows_per_group, tm, emit_v):
    n, d = x.shape
    cwid = g_v.shape[1]
    row = lambda r: (r, 0)
    out_shape = [jax.ShapeDtypeStruct((n, d), F32)]
    out_specs = [pl.BlockSpec((tm, d), row)]
    if emit_v:
        out_shape.append(jax.ShapeDtypeStruct((n, cwid), F32))
        out_specs.append(pl.BlockSpec((tm, cwid), row))
    return pl.pallas_call(
        functools.partial(_gmlp_kernel, 4, emit_v),
        grid=(n // tm,),
        in_specs=[
            pl.BlockSpec((tm, d), row),
            _mod_spec(mods, layer, rows_per_group, tm),
            _resident(g1.shape),
            _resident(g2.shape),
            _resident(w_in.shape),
            _resident(g_v.shape),
            _resident(ws_mat.shape),
            _resident(bs_full.shape),
            _resident(w_out.shape),
            _resident(w1.shape),
            _resident(w2.shape),
            _resident(g_final.shape),
        ],
        out_specs=out_specs,
        out_shape=out_shape,
        scratch_shapes=[pltpu.VMEM((tm, cwid), BF16)],
        compiler_params=_params(("arbitrary",)),
        name="gmlp_layer",
    )(x, mods, g1, g2, w_in, g_v, ws_mat, bs_full, w_out, w1, w2, g_final)


def _alibi_slopes(n_heads):
    return jnp.exp2(-8.0 * jnp.arange(1, n_heads + 1, dtype=F32) / n_heads)


def kernel(x_prompt, x_sample, cache_k, cache_v, state_conv, page_table, c_prompt, c_sample,
           norm_g, w_ada, b_ada, w_in_ab, conv_w, w_out_ab, w_in_c, g_v, w_s, b_s, w_out_c,
           w_ff1, w_ff2, g_final):
    batch, seq, d = x_prompt.shape
    bd, tq, _ = x_sample.shape
    n_ab, n_pool, page, n_heads, hd = cache_k.shape
    aw = n_heads * hd
    cw = state_conv.shape[-1]
    n_c, n_grp, chunk, _ = w_s.shape
    cwid = g_v.shape[-1]
    depth = norm_g.shape[0]
    assert depth == 2 and n_ab == 1 and n_c == 1 and hd * 2 == LANES
    assert tq <= chunk and chunk % tq == 0 and (bd * tq) % chunk == 0
    n_p, n_s = batch * seq, bd * tq
    tm = 512 if seq % 512 == 0 else chunk

    wb = lambda w: w.astype(BF16)
    w_in_ab_b, w_out_ab_b = wb(w_in_ab[0]), wb(w_out_ab[0])
    w_in_c_b, w_out_c_b = wb(w_in_c[0]), wb(w_out_c[0])
    w_ff1_b = [wb(w_ff1[l]) for l in range(depth)]
    w_ff2_b = [wb(w_ff2[l]) for l in range(depth)]

    bp = -(-(batch + bd) // SUBLANES) * SUBLANES
    c_all = jnp.concatenate([c_prompt, c_sample, jnp.zeros((bp - batch - bd, d), F32)], axis=0)
    ada = _ada_terms(c_all, w_ada, b_ada)
    mods_p = ada[:, :batch].reshape(depth, batch, 6, 1, d)
    mods_s = jnp.repeat(ada[:, batch:batch + bd].reshape(depth, bd, 6, d), tq, axis=1)
    mods_s = mods_s.transpose(0, 2, 1, 3)

    slopes = _alibi_slopes(n_heads)
    slope_tab = jnp.broadcast_to(slopes[:, None], (n_heads, LANES))
    g = lambda l, j: norm_g[l, j].reshape(1, d)

    tril = jnp.tril(jnp.ones((chunk, chunk), bool))
    ws_p = jnp.where(tril, w_s[0], 0.0).astype(BF16)
    bs_p = jnp.broadcast_to(b_s[0][:, :, None], (n_grp, chunk, cwid // n_grp))
    ws_small = jnp.where(tril[:tq, :tq], w_s[0][:, :tq, :tq], 0.0)
    eye = jnp.eye(chunk // tq, dtype=F32)
    ws_s = jnp.einsum("ab,gts->gatbs", eye, ws_small).reshape(n_grp, chunk, chunk).astype(BF16)
    bs_s = jnp.broadcast_to(jnp.tile(b_s[0][:, :tq], (1, chunk // tq))[:, :, None],
                            (n_grp, chunk, cwid // n_grp))

    xp = x_prompt.reshape(n_p, d)
    prev0 = jnp.zeros((batch, CONV_W - 1, cw), F32)
    qp, kp_b, kp_t, vp_t, km_p, gated_p, conv_p = _front_prompt(
        xp, mods_p, g(0, 0), w_in_ab_b, conv_w[0], prev0, 0, tm, n_heads, page)
    attn_p = _moba_prompt(qp, kp_b, vp_t, km_p.reshape(n_p // MOBA_BLOCK, aw), slopes, batch)
    xp = _back_ab(xp, attn_p, gated_p, mods_p, g(0, 1), w_out_ab_b, w_ff1_b[0], w_ff2_b[0], 0,
                  seq, tm)
    (yp,) = _gmlp_layer(xp, mods_p, g(1, 0), g(1, 1), w_in_c_b, g_v, ws_p, bs_p, w_out_c_b,
                        w_ff1_b[1], w_ff2_b[1], g_final.reshape(1, d), 1, seq, tm, False)

    pool_pages = lambda c: c.transpose(0, 1, 3, 4, 2).reshape(n_pool, aw, page)
    xs = x_sample.reshape(n_s, d)
    st = state_conv[0]
    pad = lambda a: jnp.concatenate([a, jnp.zeros((bd, tq - a.shape[1], cw), F32)], axis=1)
    pe1 = pad(st[:, 1:2]).reshape(n_s, cw)
    pe2 = pad(st).reshape(n_s, cw)
    qs, ks, vs, gated_s, u_s = _front_sample(xs, mods_s, g(0, 0), w_in_ab_b, conv_w[0], pe1, pe2,
                                             0, tq)
    attn_s = _moba_sample(qs.reshape(bd, tq, aw), ks.reshape(bd, tq, aw), vs.reshape(bd, tq, aw),
                          pool_pages(cache_k), pool_pages(cache_v), page_table, slope_tab,
                          n_heads)
    xs = _back_ab(xs, attn_s.reshape(n_s, aw), gated_s, mods_s, g(0, 1), w_out_ab_b, w_ff1_b[0],
                  w_ff2_b[0], 0, n_s, chunk)
    ys, cv_s = _gmlp_layer(xs, mods_s, g(1, 0), g(1, 1), w_in_c_b, g_v, ws_s, bs_s, w_out_c_b,
                           w_ff1_b[1], w_ff2_b[1], g_final.reshape(1, d), 1, n_s, chunk, True)

    return (
        yp.reshape(batch, seq, d),
        ys.reshape(bd, tq, d),
        kp_t.transpose(0, 3, 1, 2)[None],
        vp_t.transpose(0, 3, 1, 2)[None],
        conv_p[None],
        ks.reshape(1, bd, tq, n_heads, hd),
        vs.reshape(1, bd, tq, n_heads, hd),
        u_s.reshape(bd, tq, cw)[:, tq - (CONV_W - 1):][None],
        cv_s.reshape(1, bd, tq, cwid),
    )
```

```python
import functools

import jax
import jax.numpy as jnp
from jax import lax
from jax.experimental import pallas as pl
from jax.experimental.pallas import tpu as pltpu

F32 = jnp.float32
BF16 = jnp.bfloat16

EPS = 1e-6
MOBA_BLOCK = 256
MOBA_BLOCK_SHIFT = 8
MOBA_TOPK = 3
CONV_W = 3
LANES = 128
SUBLANES = 8
ONES_ROWS = 2 * SUBLANES
NEG_BIG = -(2.0 ** 100)
VMEM_LIMIT_BYTES = 56 * 1024 * 1024

_NT = (((1,), (1,)), ((), ()))


def _params(sem):
    return pltpu.CompilerParams(dimension_semantics=sem, vmem_limit_bytes=VMEM_LIMIT_BYTES)


def _resident(shape):
    n = len(shape)
    return pl.BlockSpec(shape, lambda *_: (0,) * n, pipeline_mode=pl.Buffered(1))


def _resident_layer(shape, layer):
    n = len(shape) - 1
    return pl.BlockSpec((None,) + tuple(shape[1:]), lambda *_: (layer,) + (0,) * n,
                        pipeline_mode=pl.Buffered(1))


def _rms(x, g):
    return x * lax.rsqrt(jnp.mean(x * x, axis=-1, keepdims=True) + EPS) * g


def _split_bf16(x):
    hi = x.astype(BF16)
    lo = (x - hi.astype(F32)).astype(BF16)
    return hi, lo


def _ada_kernel(c_ref, w_ref, b_ref, o_ref):
    c = c_ref[...]
    s = (c * jax.nn.sigmoid(c)).astype(BF16)
    o_ref[0, 0] = jnp.dot(s, w_ref[0].astype(BF16), preferred_element_type=F32) + b_ref[0]


def _ada_terms(c_all, w_ada, b_ada):
    n_layers, d, d6 = w_ada.shape
    bp = c_all.shape[0]
    return pl.pallas_call(
        _ada_kernel,
        grid=(n_layers, d6 // d),
        in_specs=[
            pl.BlockSpec((bp, d), lambda l, j: (0, 0)),
            pl.BlockSpec((1, d, d), lambda l, j: (l, 0, j)),
            pl.BlockSpec((1, 1, d), lambda l, j: (l, 0, j)),
        ],
        out_specs=pl.BlockSpec((1, 1, bp, d), lambda l, j: (l, j, 0, 0)),
        out_shape=jax.ShapeDtypeStruct((n_layers, d6 // d, bp, d), F32),
        compiler_params=_params(("arbitrary", "arbitrary")),
        name="ada_terms",
    )(c_all, w_ada, b_ada.reshape(n_layers, 1, d6))


def _front_kernel(multi_seq, seq_len, *refs):
    if multi_seq:
        (x_ref, mod_ref, g_ref, w_ref, cw_ref, pe1_ref, pe2_ref,
         q_ref, k_ref, v_ref, gated_ref, u_ref) = refs
    else:
        (x_ref, mod_ref, g_ref, w_ref, cw_ref, prev_ref,
         q_ref, kb_ref, kt_ref, vt_ref, km_ref, gated_ref, cs_ref, carry_ref) = refs
    tm = x_ref.shape[0]
    cw = gated_ref.shape[1]
    h = _rms(x_ref[...], g_ref[...]) * (1.0 + mod_ref[1]) + mod_ref[0]
    hb = h.astype(BF16)

    def proj(j):
        return jnp.dot(hb, w_ref[:, j * cw:(j + 1) * cw], preferred_element_type=F32)

    q_ref[...] = proj(0)
    if multi_seq:
        k_ref[...] = proj(1)
        v_ref[...] = proj(2)
    else:
        n_page, n_heads, hd, page = kt_ref.shape
        yk = proj(1)
        kb_ref[...] = yk.astype(BF16)
        for j in range(tm // MOBA_BLOCK):
            km_ref[j] = jnp.mean(yk[j * MOBA_BLOCK:(j + 1) * MOBA_BLOCK, :], axis=0, keepdims=True)
        yv = proj(2)
        for y, t_ref in ((yk, kt_ref), (yv, vt_ref)):
            yt = y.T
            for j in range(n_page):
                t_ref[j] = yt[:, j * page:(j + 1) * page].reshape(n_heads, hd, page)
    u = proj(4) * proj(5)
    rows = lax.broadcasted_iota(jnp.int32, u.shape, 0)
    u1 = pltpu.roll(u, 1, axis=0)
    u2 = pltpu.roll(u, 2, axis=0)
    if multi_seq:
        assert seq_len & (seq_len - 1) == 0
        t = rows & (seq_len - 1)
        u1 = jnp.where(t == 0, pe1_ref[...], u1)
        u2 = jnp.where(t < 2, pe2_ref[...], u2)
        u_ref[...] = u
    else:
        i = pl.program_id(1)

        @pl.when(i == 0)
        def _():
            carry_ref[SUBLANES - 2:SUBLANES, :] = prev_ref[...]

        p0 = carry_ref[SUBLANES - 2:SUBLANES - 1, :]
        p1 = carry_ref[SUBLANES - 1:SUBLANES, :]
        u1 = jnp.where(rows == 0, p1, u1)
        u2 = jnp.where(rows == 0, p0, jnp.where(rows == 1, p1, u2))
        carry_ref[...] = u[tm - SUBLANES:tm, :]

        @pl.when(i == pl.num_programs(1) - 1)
        def _():
            cs_ref[...] = u[tm - 2:tm, :]

    yc = u2 * cw_ref[0:1, :] + u1 * cw_ref[1:2, :] + u * cw_ref[2:3, :]
    gated_ref[...] = (proj(3) * yc).astype(BF16)


def _front_prompt(x, mods, g, w_in, conv_w, prev, layer, tm, n_heads, page):
    n, d = x.shape
    nb = prev.shape[0]
    cw = prev.shape[2]
    nt = n // nb // tm
    assert tm % page == 0 and tm % MOBA_BLOCK == 0
    row = lambda b, i: (b * nt + i, 0)
    pages = (tm // page, n_heads, cw // n_heads, page)
    page_map = lambda b, i: (b * nt + i, 0, 0, 0)
    n_mean = tm // MOBA_BLOCK
    return pl.pallas_call(
        functools.partial(_front_kernel, False, 0),
        grid=(nb, nt),
        in_specs=[
            pl.BlockSpec((tm, d), row),
            pl.BlockSpec((None, None, 6, 1, d), lambda b, i: (layer, b, 0, 0, 0)),
            _resident(g.shape),
            _resident(w_in.shape),
            _resident(conv_w.shape),
            pl.BlockSpec((None, CONV_W - 1, cw), lambda b, i: (b, 0, 0)),
        ],
        out_specs=[
            pl.BlockSpec((tm, cw), row),
            pl.BlockSpec((tm, cw), row),
            pl.BlockSpec(pages, page_map),
            pl.BlockSpec(pages, page_map),
            pl.BlockSpec((n_mean, 1, cw), lambda b, i: (b * nt + i, 0, 0)),
            pl.BlockSpec((tm, cw), row),
            pl.BlockSpec((None, CONV_W - 1, cw), lambda b, i: (b, 0, 0)),
        ],
        out_shape=[
            jax.ShapeDtypeStruct((n, cw), F32),
            jax.ShapeDtypeStruct((n, cw), BF16),
            jax.ShapeDtypeStruct((n // page,) + pages[1:], F32),
            jax.ShapeDtypeStruct((n // page,) + pages[1:], F32),
            jax.ShapeDtypeStruct((n // MOBA_BLOCK, 1, cw), F32),
            jax.ShapeDtypeStruct((n, cw), BF16),
            jax.ShapeDtypeStruct(prev.shape, F32),
        ],
        scratch_shapes=[pltpu.VMEM((SUBLANES, cw), F32)],
        compiler_params=_params(("arbitrary", "arbitrary")),
        name="front_prompt",
    )(x, mods, g, w_in, conv_w, prev)


def _front_sample(x, mods_rows, g, w_in, conv_w, pe1, pe2, layer, seq_len):
    n, d = x.shape
    cw = pe1.shape[1]
    full = lambda i: (0, 0)
    return pl.pallas_call(
        functools.partial(_front_kernel, True, seq_len),
        grid=(1,),
        in_specs=[
            pl.BlockSpec((n, d), full),
            pl.BlockSpec((None, 6, n, d), lambda i: (layer, 0, 0, 0)),
            _resident(g.shape),
            _resident(w_in.shape),
            _resident(conv_w.shape),
            pl.BlockSpec((n, cw), full),
            pl.BlockSpec((n, cw), full),
        ],
        out_specs=[pl.BlockSpec((n, cw), full)] * 5,
        out_shape=[
            jax.ShapeDtypeStruct((n, cw), F32),
            jax.ShapeDtypeStruct((n, cw), F32),
            jax.ShapeDtypeStruct((n, cw), F32),
            jax.ShapeDtypeStruct((n, cw), BF16),
            jax.ShapeDtypeStruct((n, cw), F32),
        ],
        compiler_params=_params(("arbitrary",)),
        name="front_sample",
    )(x, mods_rows, g, w_in, conv_w, pe1, pe2)


def _moba_item(step, lag, n_items, n_half, n_pair):
    item = jnp.clip(step - lag, 0, n_items - 1)
    seq_pair = item // n_half
    return seq_pair // n_pair, seq_pair % n_pair, item % n_half


def _moba_prompt_kernel(n_blk, n_pair, slopes_ref, qa_ref, qb_ref, k_ref, v_ref, km_ref, o_ref,
                        kb_ref, vt_ref, qf_ref, st_ref, um_ref, m_ref, acc_ref):
    step = pl.program_id(0)
    n_items = pl.num_programs(0) - 1
    n_half = n_blk // 2
    blk = MOBA_BLOCK
    seq = k_ref.shape[0]
    hd = v_ref.shape[2]
    page = v_ref.shape[3]
    ppb = blk // page
    n_prev = n_blk - 1
    scale = float(hd) ** -0.5
    b, p, j = _moba_item(step, 0, n_items, n_half, n_pair)
    b_old, p_old, j_old = _moba_item(step, 1, n_items, n_half, n_pair)
    slot = (b * n_pair + p) % 2
    slot_old = (b_old * n_pair + p_old) % 2
    tiles = (j, n_blk - 1 - j)
    tiles_old = (j_old, n_blk - 1 - j_old)

    @pl.when(step == 0)
    def _():
        r = lax.broadcasted_iota(jnp.int32, (seq, LANES), 0)
        c = lax.broadcasted_iota(jnp.int32, (seq, LANES), 1)
        n = lax.shift_right_logical(r, MOBA_BLOCK_SHIFT)
        off = r & (blk - 1)
        feat = jnp.where(c == n, 1.0,
                         jnp.where(c == n_blk, n.astype(F32),
                                   jnp.where(c == n_blk + 1, off.astype(F32),
                                             jnp.where(c == n_blk + 2, 1.0, 0.0))))
        kb_ref[:, LANES:2 * LANES] = feat.astype(BF16)
        st_ref[...] = jnp.zeros_like(st_ref)
        m_ref[...] = jnp.zeros_like(m_ref)
        vt_ref[:, :, :, hd:, :] = jnp.ones(vt_ref.shape[:3] + (ONES_ROWS, blk), BF16)

    @pl.when((j == 0) & (step < n_items))
    def _():
        kb_ref[:, 0:LANES] = k_ref[...]
        for n in range(n_blk):
            for e in range(2):
                vt_ref[slot, n, e, 0:hd, :] = jnp.concatenate(
                    [v_ref[n * ppb + g, e] for g in range(ppb)], axis=1).astype(BF16)

    _moba_setup(n_blk, scale, tiles, p, slopes_ref, qa_ref, qb_ref, km_ref, qf_ref)

    n_grp = blk // SUBLANES
    wide = 2 * blk
    feat_lane = lax.broadcasted_iota(jnp.int32, (1, 2 * LANES), 1) - LANES
    ind_off = jnp.where((feat_lane >= 0) & (feat_lane < n_blk), 0.0, 1.0).astype(BF16)
    key_row = lax.broadcasted_iota(jnp.int32, (blk, wide), 0)
    q_col = lax.broadcasted_iota(jnp.int32, (blk, wide), 1) & (blk - 1)

    def unit_ids(u, jj, item_tiles):
        if u >= n_prev:
            return u - n_prev, item_tiles[u - n_prev]
        is_a = u < jj
        return jnp.where(is_a, 0, 1), jnp.where(is_a, u, u - jj)

    acc_ref[...] = jnp.zeros_like(acc_ref)
    for u in range(n_prev + 2):
        t_old, n_old = unit_ids(u, j_old, tiles_old)
        pr = jnp.exp(st_ref[u].reshape(n_grp, SUBLANES, wide) - m_ref[t_old][None])
        pb = pr.reshape(blk, wide).astype(BF16)
        for e in range(2):
            acc_ref[t_old, e] = acc_ref[t_old, e] + jnp.dot(
                vt_ref[slot_old, n_old, e], pb[:, e * blk:(e + 1) * blk],
                preferred_element_type=F32)
        t_new, n_new = unit_ids(u, j, tiles)
        own = u >= n_prev
        kblk = kb_ref[pl.ds(pl.multiple_of(n_new * blk, blk), blk), :]
        if own:
            kblk = kblk * ind_off
        s = jnp.dot(kblk, qf_ref[t_new], preferred_element_type=F32)
        if own:
            s = jnp.where(key_row <= q_col, s, NEG_BIG)
        st_ref[u] = s
        um_ref[u] = jnp.max(s.reshape(n_grp, SUBLANES, wide), axis=0)

    for t in range(2):
        out_t = jnp.concatenate(
            [acc_ref[t, e, 0:hd, :] / acc_ref[t, e, hd:hd + 1, :] for e in range(2)], axis=0)
        o_ref[pl.ds(pl.multiple_of(tiles_old[t] * blk, blk), blk), :] = (
            out_t.T.astype(o_ref.dtype))
    m_new = [um_ref[n_prev], um_ref[n_prev + 1]]
    for u in range(n_prev):
        val = um_ref[u]
        is_a = u < j
        m_new[0] = jnp.maximum(m_new[0], jnp.where(is_a, val, -jnp.inf))
        m_new[1] = jnp.maximum(m_new[1], jnp.where(is_a, -jnp.inf, val))
    for t in range(2):
        m_ref[t] = jnp.broadcast_to(jnp.max(m_new[t], axis=0, keepdims=True), m_new[t].shape)


def _moba_setup(n_blk, scale, tiles, pair, slopes_ref, qa_ref, qb_ref, km_ref, qf_ref):
    blk = MOBA_BLOCK
    hd = LANES // 2
    km = km_ref[...]
    lane = lax.broadcasted_iota(jnp.int32, (1, LANES), 1)
    km_heads = jnp.concatenate([km * (lane < hd).astype(F32), km * (lane >= hd).astype(F32)],
                               axis=0)
    km_hi, km_lo = _split_bf16(km_heads)
    km_hi_lo = jnp.concatenate([km_hi, km_lo], axis=0)
    qrow = lax.broadcasted_iota(jnp.int32, (LANES, blk), 0)
    head_rows = [(qrow < hd).astype(F32), (qrow >= hd).astype(F32)]
    blk_row = lax.broadcasted_iota(jnp.int32, (n_blk, blk), 0)
    blk_row_f = blk_row.astype(F32)
    for t, q_ref in enumerate((qa_ref, qb_ref)):
        qt = q_ref[...].T
        qt_hi, qt_lo = _split_bf16(qt)
        g_hi = jnp.dot(km_hi_lo, qt_hi, preferred_element_type=F32)
        gate_both = (g_hi[0:2 * n_blk] + g_hi[2 * n_blk:]
                     + jnp.dot(km_hi, qt_lo, preferred_element_type=F32))
        tile_f = tiles[t].astype(F32)
        valid = blk_row < tiles[t]
        for e in range(2):
            slope = slopes_ref[2 * pair + e]
            gate_t = jnp.where(valid, gate_both[e * n_blk:(e + 1) * n_blk], -jnp.inf)
            taken = jnp.zeros((n_blk, blk), F32)
            for _ in range(MOBA_TOPK):
                is_max = gate_t == jnp.max(gate_t, axis=0, keepdims=True)
                first = jnp.min(jnp.where(is_max, blk_row_f, float(n_blk)), axis=0,
                                keepdims=True)
                pick = blk_row_f == first
                taken = jnp.where(pick, 1.0, taken)
                gate_t = jnp.where(pick, -jnp.inf, gate_t)
            mask_t = jnp.where(valid, jnp.where(taken > 0.0, 0.0, NEG_BIG), NEG_BIG)
            consts = jnp.where(qrow == n_blk, slope * blk,
                               jnp.where(qrow == n_blk + 1, slope,
                                         jnp.where(qrow == n_blk + 2, -slope * blk * tile_f, 0.0)))
            feat_t = jnp.concatenate([mask_t, jnp.zeros((LANES - n_blk, blk), F32)],
                                     axis=0) + consts
            q_e = (qt * (head_rows[e] * scale)).astype(BF16)
            qf_ref[t, :, e * blk:(e + 1) * blk] = jnp.concatenate(
                [q_e, feat_t.astype(BF16)], axis=0)


def _moba_prompt(q, k_rows, v_pages, k_mean, slopes, batch):
    n, aw = q.shape
    _, n_heads, hd, page = v_pages.shape
    seq = n // batch
    blk = MOBA_BLOCK
    n_blk = seq // blk
    assert seq % blk == 0 and blk % page == 0 and 2 * hd == LANES
    assert n_blk % SUBLANES == 0 and n_blk + 3 <= LANES
    n_pair = aw // LANES
    n_half = n_blk // 2
    n_items = batch * n_pair * n_half
    n_unit = n_blk + 1

    def item_map(lag, fn):
        return lambda s, _: fn(*_moba_item(s, lag, n_items, n_half, n_pair))

    return pl.pallas_call(
        functools.partial(_moba_prompt_kernel, n_blk, n_pair),
        grid_spec=pltpu.PrefetchScalarGridSpec(
            num_scalar_prefetch=1,
            grid=(n_items + 1,),
            in_specs=[
                pl.BlockSpec((blk, LANES), item_map(0, lambda b, p, j: (b * n_blk + j, p))),
                pl.BlockSpec((blk, LANES),
                             item_map(0, lambda b, p, j: (b * n_blk + n_blk - 1 - j, p))),
                pl.BlockSpec((seq, LANES), item_map(0, lambda b, p, j: (b, p))),
                pl.BlockSpec((seq // page, 2, hd, page),
                             item_map(0, lambda b, p, j: (b, p, 0, 0))),
                pl.BlockSpec((n_blk, LANES), item_map(0, lambda b, p, j: (b, p))),
            ],
            out_specs=pl.BlockSpec((seq, LANES), item_map(1, lambda b, p, j: (b, p))),
            scratch_shapes=[
                pltpu.VMEM((seq, 2 * LANES), BF16),
                pltpu.VMEM((2, n_blk, 2, hd + ONES_ROWS, blk), BF16),
                pltpu.VMEM((2, 2 * LANES, 2 * blk), BF16),
                pltpu.VMEM((n_unit, blk, 2 * blk), F32),
                pltpu.VMEM((n_unit, SUBLANES, 2 * blk), F32),
                pltpu.VMEM((2, SUBLANES, 2 * blk), F32),
                pltpu.VMEM((2, 2, hd + ONES_ROWS, blk), F32),
            ],
        ),
        out_shape=jax.ShapeDtypeStruct((n, aw), BF16),
        compiler_params=_params(("arbitrary",)),
        name="moba_prompt",
    )(slopes, q, q, k_rows, v_pages, k_mean)


def _moba_sample_kernel(n_pages, n_heads, pt_ref, q_ref, kn_ref, vn_ref, slope_ref,
                        ck_hbm, cv_hbm, o_ref, kbuf, vsel, s_ref, km_ref, bias_ref, idx_ref,
                        idx_smem, stat_ref, own_ref, orow_ref, ksem, vsem, isem):
    step = pl.program_id(0)
    n_samples = pl.num_programs(0) - 1
    blk = MOBA_BLOCK
    tq, aw = q_ref.shape
    page = kbuf.shape[3]
    ppb = blk // page
    hd = aw // n_heads
    rows = tq * n_heads
    past = n_pages * page
    n_blk = past // blk
    n_sel = min(MOBA_TOPK, n_blk)
    slot = step % 2
    slot_old = 1 - slot
    feat0 = n_blk

    def k_copy(sample, pg):
        return pltpu.make_async_copy(ck_hbm.at[pt_ref[sample, pg]], kbuf.at[sample % 2, pg],
                                     ksem.at[sample % 2])

    def for_pages(fn):
        def go(pg, c):
            fn(pg)
            return c
        lax.fori_loop(0, n_pages, go, 0)

    def idx_copy(sl):
        return pltpu.make_async_copy(idx_ref.at[sl], idx_smem.at[sl], isem)

    def v_copies(sample, sl, row_range=range(rows)):
        for r in row_range:
            for j in range(n_sel):
                first_page = idx_smem[sl, r, j] * ppb
                for g in range(ppb):
                    yield pltpu.make_async_copy(
                        cv_hbm.at[pt_ref[sample, first_page + g],
                                  pl.ds((r % n_heads) * hd, hd), :],
                        vsel.at[sl, r * n_sel + j, g], vsem.at[sl])

    lane = lax.broadcasted_iota(jnp.int32, (n_heads, aw), 1)
    hrow = lax.broadcasted_iota(jnp.int32, (n_heads, aw), 0)
    head_mask = jnp.where(lane >= hrow * hd, jnp.where(lane < (hrow + 1) * hd, 1.0, 0.0), 0.0)
    hm_rows = jnp.concatenate([head_mask] * tq, axis=0)

    @pl.when(step == 0)
    def _():
        for_pages(lambda pg: k_copy(0, pg).start())
        blk_id = lax.broadcasted_iota(jnp.int32, bias_ref.shape, 0)
        f = lax.broadcasted_iota(jnp.int32, bias_ref.shape, 1)
        off = lax.broadcasted_iota(jnp.int32, bias_ref.shape, 2)
        bias_ref[...] = jnp.where(
            f == blk_id, 1.0,
            jnp.where(f == feat0, (blk_id * blk).astype(F32),
                      jnp.where(f == feat0 + 1, off.astype(F32),
                                jnp.where((f == feat0 + 2) | (f == feat0 + 3), 1.0, 0.0)))
        ).astype(BF16)

    @pl.when(step < n_samples)
    def _():
        _moba_sample_keys(step, slot, n_pages, n_heads, n_sel, hm_rows, q_ref, kn_ref, vn_ref,
                          slope_ref, kbuf, s_ref, km_ref, bias_ref, idx_ref, stat_ref, own_ref,
                          k_copy, for_pages)
        idx_copy(slot).start()

    def values_and_gathers(do_values, do_gathers):
        if do_values:
            for copy in v_copies(step - 1, slot_old):
                copy.wait()
            inv_l = 1.0 / stat_ref[slot_old]
        hidden_rows = rows // 4 if do_values else 0
        folded = []
        for r in range(rows):
            if do_values:
                acc = jnp.zeros((hd, blk), F32)
                for j in range(n_sel):
                    pr = s_ref[slot_old, idx_smem[slot_old, r, j], r:r + 1, :]
                    v_blk = jnp.concatenate(
                        [vsel[slot_old, r * n_sel + j, g] for g in range(ppb)], axis=1)
                    acc = acc + pr * v_blk
                fold = functools.reduce(
                    jnp.add, [acc[:, c * LANES:(c + 1) * LANES] for c in range(blk // LANES)])
                folded.append(fold * inv_l[r:r + 1, 0:1])
                if r % n_heads == n_heads - 1:
                    tok = r // n_heads
                    tok_t = jnp.concatenate(folded, axis=0).T
                    folded = []
                    orow_ref[tok:tok + 1, :] = (jnp.sum(tok_t, axis=0, keepdims=True)
                                                + own_ref[slot_old, tok:tok + 1, :])
            if do_gathers and r >= hidden_rows:
                if r == hidden_rows:
                    idx_copy(slot).wait()
                first = 0 if r == hidden_rows else r
                for copy in v_copies(step, slot, range(first, r + 1)):
                    copy.start()
        if do_values:
            o_ref[...] = orow_ref[0:tq, :].astype(o_ref.dtype)

    pl.when(step == 0)(lambda: values_and_gathers(False, True))
    pl.when((step > 0) & (step < n_samples))(lambda: values_and_gathers(True, True))
    pl.when(step == n_samples)(lambda: values_and_gathers(True, False))


def _moba_sample_keys(b, slot, n_pages, n_heads, n_sel, hm_rows, q_ref, kn_ref, vn_ref,
                      slope_ref, kbuf, s_ref, km_ref, bias_ref, idx_ref, stat_ref, own_ref,
                      k_copy, for_pages):
    blk = MOBA_BLOCK
    tq, aw = q_ref.shape
    page = kbuf.shape[3]
    ppb = blk // page
    hd = aw // n_heads
    rows = tq * n_heads
    past = n_pages * page
    n_blk = past // blk
    feat0 = n_blk
    q = q_ref[...]
    q_rows = jnp.concatenate(
        [jnp.broadcast_to(q[t:t + 1, :], (n_heads, aw)) for t in range(tq)], axis=0) * hm_rows
    qs_bf = (q_rows * (float(hd) ** -0.5)).astype(BF16)
    slope = jnp.concatenate([slope_ref[:, 0:1]] * tq, axis=0)
    t_row = jnp.concatenate([jnp.full((n_heads, 1), t, jnp.int32) for t in range(tq)], axis=0)

    for_pages(lambda pg: k_copy(b, pg).wait())

    @pl.when(b + 2 < pl.num_programs(0))
    def _():
        for_pages(lambda pg: k_copy(b + 1, pg).start())

    km_ref[...] = jnp.zeros_like(km_ref)
    for n in range(n_blk):
        pages_n = [kbuf[slot, n * ppb + g] for g in range(ppb)]
        km_ref[:, n:n + 1] = jnp.sum(functools.reduce(jnp.add, pages_n), axis=1,
                                     keepdims=True) * (1.0 / blk)
        s_ref[slot, n] = jnp.dot(qs_bf, jnp.concatenate(pages_n, axis=1).astype(BF16),
                                 preferred_element_type=F32)
    km_t = km_ref[...]

    q_hi, q_lo = _split_bf16(q_rows)
    km_hi, km_lo = _split_bf16(km_t)
    gate = (jnp.dot(q_hi, km_hi, preferred_element_type=F32)
            + jnp.dot(q_hi, km_lo, preferred_element_type=F32)
            + jnp.dot(q_lo, km_hi, preferred_element_type=F32))
    blk_lane = lax.broadcasted_iota(jnp.int32, gate.shape, 1)
    gate = jnp.where(blk_lane < n_blk, gate, -jnp.inf)
    lane_f = blk_lane.astype(F32)
    taken = jnp.zeros(gate.shape, F32)
    picked = jnp.zeros(gate.shape, F32)
    for j in range(n_sel):
        is_max = gate == jnp.max(gate, axis=1, keepdims=True)
        first = jnp.min(jnp.where(is_max, lane_f, float(LANES)), axis=1, keepdims=True)
        pick = lane_f == first
        taken = jnp.where(pick, 1.0, taken)
        gate = jnp.where(pick, -jnp.inf, gate)
        picked = jnp.where(blk_lane == j, first, picked)
    idx_ref[slot] = picked.astype(jnp.int32)
    slope_l = jnp.broadcast_to(slope, gate.shape)
    t_l = jnp.broadcast_to(t_row, gate.shape).astype(F32)
    feat = jnp.where(
        blk_lane < n_blk, jnp.where(taken > 0.0, 0.0, NEG_BIG),
        jnp.where((blk_lane == feat0) | (blk_lane == feat0 + 1), slope_l,
                  jnp.where(blk_lane == feat0 + 2, -slope_l * past,
                            jnp.where(blk_lane == feat0 + 3, -slope_l * t_l, 0.0))))
    feat_bf = feat.astype(BF16)

    kn = kn_ref[...]
    vn = vn_ref[...]
    q_sc = q_rows * (float(hd) ** -0.5)
    own = []
    for t in range(tq):
        so = jnp.sum(q_sc * kn[t:t + 1, :], axis=1, keepdims=True)
        so = so - slope * (t_row - t).astype(F32)
        own.append(jnp.where(t_row >= t, so, NEG_BIG))
    m_elem = jnp.full((rows, blk), NEG_BIG, F32)
    for n in range(n_blk):
        s = s_ref[slot, n] + jnp.dot(feat_bf, bias_ref[n], preferred_element_type=F32)
        s_ref[slot, n] = s
        m_elem = jnp.maximum(m_elem, s)
    m_run = jnp.maximum(functools.reduce(jnp.maximum, own),
                        jnp.max(m_elem, axis=1, keepdims=True))

    acc = jnp.zeros((rows, aw), F32)
    l_run = jnp.zeros((rows, 1), F32)
    for t in range(tq):
        po = jnp.exp(own[t] - m_run)
        l_run = l_run + po
        acc = acc + po * vn[t:t + 1, :]
    m_l = jnp.broadcast_to(m_run, (rows, blk))
    l_elem = jnp.zeros((rows, blk), F32)
    for n in range(n_blk):
        pr = jnp.exp(s_ref[slot, n] - m_l)
        s_ref[slot, n] = pr
        l_elem = l_elem + pr
    l_run = l_run + jnp.sum(l_elem, axis=1, keepdims=True)
    stat_ref[slot] = jnp.broadcast_to(l_run, gate.shape)
    own_rows = acc * hm_rows / l_run
    own_ref[slot, 0:tq, :] = jnp.sum(own_rows.reshape(tq, n_heads, aw), axis=1)


def _moba_sample(q, k_new, v_new, cache_k, cache_v, page_table, slope_tab, n_heads):
    bd, tq, aw = q.shape
    n_pages = page_table.shape[1]
    page = cache_k.shape[2]
    past = n_pages * page
    assert MOBA_BLOCK % page == 0 and past % MOBA_BLOCK == 0 and page == LANES
    assert past // MOBA_BLOCK + 4 <= LANES
    rows = tq * n_heads
    n_blk = past // MOBA_BLOCK
    n_sel = min(MOBA_TOPK, n_blk)
    hd = aw // n_heads
    assert tq <= LANES and rows % SUBLANES == 0
    key_tok = lambda s, pt: (jnp.minimum(s, bd - 1), 0, 0)
    val_tok = lambda s, pt: (jnp.maximum(s - 1, 0), 0, 0)
    return pl.pallas_call(
        functools.partial(_moba_sample_kernel, n_pages, n_heads),
        grid_spec=pltpu.PrefetchScalarGridSpec(
            num_scalar_prefetch=1,
            grid=(bd + 1,),
            in_specs=[
                pl.BlockSpec((None, tq, aw), key_tok),
                pl.BlockSpec((None, tq, aw), key_tok),
                pl.BlockSpec((None, tq, aw), key_tok),
                pl.BlockSpec(slope_tab.shape, lambda s, pt: (0, 0)),
                pl.BlockSpec(memory_space=pl.ANY),
                pl.BlockSpec(memory_space=pl.ANY),
            ],
            out_specs=pl.BlockSpec((None, tq, aw), val_tok),
            scratch_shapes=[
                pltpu.VMEM((2, n_pages, aw, page), F32),
                pltpu.VMEM((2, rows * n_sel, MOBA_BLOCK // page, hd, page), F32),
                pltpu.VMEM((2, n_blk, rows, MOBA_BLOCK), F32),
                pltpu.VMEM((aw, LANES), F32),
                pltpu.VMEM((n_blk, LANES, MOBA_BLOCK), BF16),
                pltpu.VMEM((2, rows, LANES), jnp.int32),
                pltpu.SMEM((2, rows, LANES), jnp.int32),
                pltpu.VMEM((2, rows, LANES), F32),
                pltpu.VMEM((2, SUBLANES, aw), F32),
                pltpu.VMEM((SUBLANES, aw), F32),
                pltpu.SemaphoreType.DMA((2,)),
                pltpu.SemaphoreType.DMA((2,)),
                pltpu.SemaphoreType.DMA(()),
            ],
        ),
        out_shape=jax.ShapeDtypeStruct((bd, tq, aw), BF16),
        compiler_params=_params(("arbitrary",)),
        name="moba_sample",
    )(page_table, q, k_new, v_new, slope_tab, cache_k, cache_v)


def _ffn_residual(x1, mod_ref, g, w1_ref, w2_ref, n_chunk):
    h = (_rms(x1, g) * (1.0 + mod_ref[4]) + mod_ref[3]).astype(BF16)
    ck = w1_ref.shape[1] // n_chunk
    acc = jnp.zeros(x1.shape, F32)
    for j in range(n_chunk):
        hid = jnp.dot(h, w1_ref[:, j * ck:(j + 1) * ck], preferred_element_type=F32)
        hid = jnp.square(jnp.maximum(hid, 0.0)).astype(BF16)
        acc = acc + jnp.dot(hid, w2_ref[j * ck:(j + 1) * ck, :], preferred_element_type=F32)
    return x1 + mod_ref[5] * acc


def _back_ab_kernel(n_chunk, x_ref, attn_ref, gated_ref, mod_ref, g_ref, wo_ref, w1_ref, w2_ref,
                    o_ref):
    aw = attn_ref.shape[1]
    mix = (jnp.dot(attn_ref[...], wo_ref[0:aw, :], preferred_element_type=F32)
           + jnp.dot(gated_ref[...], wo_ref[aw:, :], preferred_element_type=F32))
    x1 = x_ref[...] + mod_ref[2] * mix
    o_ref[...] = _ffn_residual(x1, mod_ref, g_ref[...], w1_ref, w2_ref, n_chunk)


def _mod_spec(mods, layer, n_rows_per_group, tm):
    d = mods.shape[-1]
    if mods.ndim == 5:
        nt = n_rows_per_group // tm
        return pl.BlockSpec((None, None, 6, 1, d), lambda r: (layer, r // nt, 0, 0, 0))
    return pl.BlockSpec((None, 6, tm, d), lambda r: (layer, 0, r, 0))


def _back_ab(x, attn, gated, mods, g, w_out, w1, w2, layer, rows_per_group, tm):
    n, d = x.shape
    aw = attn.shape[1]
    cw = gated.shape[1]
    row = lambda r: (r, 0)
    return pl.pallas_call(
        functools.partial(_back_ab_kernel, 4),
        grid=(n // tm,),
        in_specs=[
            pl.BlockSpec((tm, d), row),
            pl.BlockSpec((tm, aw), row),
            pl.BlockSpec((tm, cw), row),
            _mod_spec(mods, layer, rows_per_group, tm),
            _resident(g.shape),
            _resident(w_out.shape),
            _resident_layer(w1.shape, layer),
            _resident_layer(w2.shape, layer),
        ],
        out_specs=pl.BlockSpec((tm, d), row),
        out_shape=jax.ShapeDtypeStruct((n, d), F32),
        compiler_params=_params(("arbitrary",)),
        name="back_ab",
    )(x, attn, gated, mods, g, w_out, w1, w2)


def _gmlp_kernel(n_chunk, emit_v, x_ref, mod_ref, g1_ref, g2_ref, wi_ref, gv_ref, ws_ref, bs_ref,
                 wo_ref, w1_ref, w2_ref, gf_ref, *rest):
    if emit_v:
        o_ref, cv_ref, us_ref = rest
    else:
        o_ref, us_ref = rest
    tm = x_ref.shape[0]
    cwid = gv_ref.shape[1]
    n_grp, chunk, _ = ws_ref.shape
    grp = cwid // n_grp
    x = x_ref[...]
    hb = (_rms(x, g1_ref[...]) * (1.0 + mod_ref[1]) + mod_ref[0]).astype(BF16)
    u = jnp.dot(hb, wi_ref[:, 0:cwid], preferred_element_type=F32)
    v = _rms(jnp.dot(hb, wi_ref[:, cwid:], preferred_element_type=F32), gv_ref[...])
    if emit_v:
        cv_ref[...] = v
    vb = v.astype(BF16)
    for c in range(tm // chunk):
        rs = slice(c * chunk, (c + 1) * chunk)
        for gi in range(n_grp):
            ls = slice(gi * grp, (gi + 1) * grp)
            s = jnp.dot(ws_ref[gi], vb[rs, ls], preferred_element_type=F32) + bs_ref[gi]
            us_ref[rs, ls] = (u[rs, ls] * s).astype(BF16)
    mix = jnp.dot(us_ref[...], wo_ref[...], preferred_element_type=F32)
    x1 = x + mod_ref[2] * mix
    x2 = _ffn_residual(x1, mod_ref, g2_ref[...], w1_ref, w2_ref, n_chunk)
    o_ref[...] = _rms(x2, gf_ref[...])


def _gmlp_layer(x, mods, g1, g2, w_in, g_v, ws_mat, bs_full, w_out, w1, w2, g_final, layer,
                rows_per_group, tm, emit_v):
    n, d = x.shape
    cwid = g_v.shape[1]
    row = lambda r: (r, 0)
    out_shape = [jax.ShapeDtypeStruct((n, d), F32)]
    out_specs = [pl.BlockSpec((tm, d), row)]
    if emit_v:
        out_shape.append(jax.ShapeDtypeStruct((n, cwid), F32))
        out_specs.append(pl.BlockSpec((tm, cwid), row))
    return pl.pallas_call(
        functools.partial(_gmlp_kernel, 4, emit_v),
        grid=(n // tm,),
        in_specs=[
            pl.BlockSpec((tm, d), row),
            _mod_spec(mods, layer, rows_per_group, tm),
            _resident(g1.shape),
            _resident(g2.shape),
            _resident(w_in.shape),
            _resident(g_v.shape),
            _resident(ws_mat.shape),
            _resident(bs_full.shape),
            _resident(w_out.shape),
            _resident_layer(w1.shape, layer),
            _resident_layer(w2.shape, layer),
            _resident(g_final.shape),
        ],
        out_specs=out_specs,
        out_shape=out_shape,
        scratch_shapes=[pltpu.VMEM((tm, cwid), BF16)],
        compiler_params=_params(("arbitrary",)),
        name="gmlp_layer",
    )(x, mods, g1, g2, w_in, g_v, ws_mat, bs_full, w_out, w1, w2, g_final)


def _alibi_slopes(n_heads):
    return jnp.exp2(-8.0 * jnp.arange(1, n_heads + 1, dtype=F32) / n_heads)


def kernel(x_prompt, x_sample, cache_k, cache_v, state_conv, page_table, c_prompt, c_sample,
           norm_g, w_ada, b_ada, w_in_ab, conv_w, w_out_ab, w_in_c, g_v, w_s, b_s, w_out_c,
           w_ff1, w_ff2, g_final):
    batch, seq, d = x_prompt.shape
    bd, tq, _ = x_sample.shape
    n_ab, n_pool, page, n_heads, hd = cache_k.shape
    aw = n_heads * hd
    cw = state_conv.shape[-1]
    n_c, n_grp, chunk, _ = w_s.shape
    cwid = g_v.shape[-1]
    depth = norm_g.shape[0]
    assert depth == 2 and n_ab == 1 and n_c == 1 and hd * 2 == LANES
    assert tq <= chunk and chunk % tq == 0 and (bd * tq) % chunk == 0
    n_p, n_s = batch * seq, bd * tq
    tm = 512 if seq % 512 == 0 else chunk

    wb = lambda w: w.astype(BF16)
    w_in_ab_b, w_out_ab_b = wb(w_in_ab[0]), wb(w_out_ab[0])
    w_in_c_b, w_out_c_b = wb(w_in_c[0]), wb(w_out_c[0])
    w_ff1_b, w_ff2_b = wb(w_ff1), wb(w_ff2)

    bp = -(-(n_s + batch) // SUBLANES) * SUBLANES
    c_all = jnp.concatenate([jnp.repeat(c_sample, tq, axis=0), c_prompt,
                             jnp.zeros((bp - n_s - batch, d), F32)], axis=0)
    ada = _ada_terms(c_all, w_ada, b_ada)
    mods_p = ada[:, :, n_s:n_s + batch].transpose(0, 2, 1, 3).reshape(depth, batch, 6, 1, d)
    mods_s = ada

    slopes = _alibi_slopes(n_heads)
    slope_tab = jnp.broadcast_to(slopes[:, None], (n_heads, LANES))
    g = lambda l, j: norm_g[l, j].reshape(1, d)

    tril = jnp.tril(jnp.ones((chunk, chunk), bool))
    ws_p = jnp.where(tril, w_s[0], 0.0).astype(BF16)
    bs_p = jnp.broadcast_to(b_s[0][:, :, None], (n_grp, chunk, cwid // n_grp))
    ws_small = jnp.where(tril[:tq, :tq], w_s[0][:, :tq, :tq], 0.0)
    eye = jnp.eye(chunk // tq, dtype=F32)
    ws_s = jnp.einsum("ab,gts->gatbs", eye, ws_small).reshape(n_grp, chunk, chunk).astype(BF16)
    bs_s = jnp.broadcast_to(jnp.tile(b_s[0][:, :tq], (1, chunk // tq))[:, :, None],
                            (n_grp, chunk, cwid // n_grp))

    xp = x_prompt.reshape(n_p, d)
    prev0 = jnp.zeros((batch, CONV_W - 1, cw), F32)
    qp, kp_b, kp_t, vp_t, km_p, gated_p, conv_p = _front_prompt(
        xp, mods_p, g(0, 0), w_in_ab_b, conv_w[0], prev0, 0, tm, n_heads, page)
    attn_p = _moba_prompt(qp, kp_b, vp_t, km_p.reshape(n_p // MOBA_BLOCK, aw), slopes, batch)
    xp = _back_ab(xp, attn_p, gated_p, mods_p, g(0, 1), w_out_ab_b, w_ff1_b, w_ff2_b, 0,
                  seq, tm)
    (yp,) = _gmlp_layer(xp, mods_p, g(1, 0), g(1, 1), w_in_c_b, g_v, ws_p, bs_p, w_out_c_b,
                        w_ff1_b, w_ff2_b, g_final.reshape(1, d), 1, seq, tm, False)

    pool_pages = lambda c: c.transpose(0, 1, 3, 4, 2).reshape(n_pool, aw, page)
    xs = x_sample.reshape(n_s, d)
    st = state_conv[0]
    pad = lambda a: jnp.concatenate([a, jnp.zeros((bd, tq - a.shape[1], cw), F32)], axis=1)
    pe1 = pad(st[:, 1:2]).reshape(n_s, cw)
    pe2 = pad(st).reshape(n_s, cw)
    qs, ks, vs, gated_s, u_s = _front_sample(xs, mods_s, g(0, 0), w_in_ab_b, conv_w[0], pe1, pe2,
                                             0, tq)
    attn_s = _moba_sample(qs.reshape(bd, tq, aw), ks.reshape(bd, tq, aw), vs.reshape(bd, tq, aw),
                          pool_pages(cache_k), pool_pages(cache_v), page_table, slope_tab,
                          n_heads)
    xs = _back_ab(xs, attn_s.reshape(n_s, aw), gated_s, mods_s, g(0, 1), w_out_ab_b, w_ff1_b,
                  w_ff2_b, 0, n_s, chunk)
    ys, cv_s = _gmlp_layer(xs, mods_s, g(1, 0), g(1, 1), w_in_c_b, g_v, ws_s, bs_s, w_out_c_b,
                           w_ff1_b, w_ff2_b, g_final.reshape(1, d), 1, n_s, chunk, True)

    return (
        yp.reshape(batch, seq, d),
        ys.reshape(bd, tq, d),
        kp_t.transpose(0, 3, 1, 2)[None],
        vp_t.transpose(0, 3, 1, 2)[None],
        conv_p[None],
        ks.reshape(1, bd, tq, n_heads, hd),
        vs.reshape(1, bd, tq, n_heads, hd),
        u_s.reshape(bd, tq, cw)[:, tq - (CONV_W - 1):][None],
        cv_s.reshape(1, bd, tq, cwid),
    )
```

```python
import functools

import jax
import jax.numpy as jnp
from jax import lax
from jax.experimental import pallas as pl
from jax.experimental.pallas import tpu as pltpu

F32 = jnp.float32
BF16 = jnp.bfloat16

EPS = 1e-6
MOBA_BLOCK = 256
MOBA_BLOCK_SHIFT = 8
MOBA_TOPK = 3
CONV_W = 3
LANES = 128
SUBLANES = 8
ONES_ROWS = 2 * SUBLANES
NEG_BIG = -(2.0 ** 100)
VMEM_LIMIT_BYTES = 56 * 1024 * 1024

_NT = (((1,), (1,)), ((), ()))


def _params(sem):
    return pltpu.CompilerParams(dimension_semantics=sem, vmem_limit_bytes=VMEM_LIMIT_BYTES)


def _resident(shape):
    n = len(shape)
    return pl.BlockSpec(shape, lambda *_: (0,) * n, pipeline_mode=pl.Buffered(1))


def _resident_layer(shape, layer):
    n = len(shape) - 1
    return pl.BlockSpec((None,) + tuple(shape[1:]), lambda *_: (layer,) + (0,) * n,
                        pipeline_mode=pl.Buffered(1))


def _rms(x, g):
    return x * lax.rsqrt(jnp.mean(x * x, axis=-1, keepdims=True) + EPS) * g


def _split_bf16(x):
    hi = x.astype(BF16)
    lo = (x - hi.astype(F32)).astype(BF16)
    return hi, lo


def _ada_kernel(c_ref, w_ref, b_ref, o_ref):
    c = c_ref[...]
    s = (c * jax.nn.sigmoid(c)).astype(BF16)
    o_ref[0, 0] = jnp.dot(s, w_ref[0].astype(BF16), preferred_element_type=F32) + b_ref[0]


def _ada_terms(c_all, w_ada, b_ada):
    n_layers, d, d6 = w_ada.shape
    bp = c_all.shape[0]
    return pl.pallas_call(
        _ada_kernel,
        grid=(n_layers, d6 // d),
        in_specs=[
            pl.BlockSpec((bp, d), lambda l, j: (0, 0)),
            pl.BlockSpec((1, d, d), lambda l, j: (l, 0, j)),
            pl.BlockSpec((1, 1, d), lambda l, j: (l, 0, j)),
        ],
        out_specs=pl.BlockSpec((1, 1, bp, d), lambda l, j: (l, j, 0, 0)),
        out_shape=jax.ShapeDtypeStruct((n_layers, d6 // d, bp, d), F32),
        compiler_params=_params(("arbitrary", "arbitrary")),
        name="ada_terms",
    )(c_all, w_ada, b_ada.reshape(n_layers, 1, d6))


def _front_kernel(multi_seq, seq_len, *refs):
    if multi_seq:
        (x_ref, mod_ref, g_ref, w_ref, cw_ref, pe1_ref, pe2_ref,
         q_ref, k_ref, v_ref, gated_ref, u_ref) = refs
    else:
        (x_ref, mod_ref, g_ref, w_ref, cw_ref, prev_ref,
         q_ref, kb_ref, kt_ref, vt_ref, km_ref, gated_ref, cs_ref, carry_ref) = refs

        @pl.when(pl.program_id(1) == 0)
        def _():
            carry_ref[SUBLANES - 2:SUBLANES, :] = prev_ref[...]

    tm = x_ref.shape[0]
    cw = gated_ref.shape[1]
    h = _rms(x_ref[...], g_ref[...]) * (1.0 + mod_ref[1]) + mod_ref[0]
    hb = h.astype(BF16)

    def proj(j):
        return jnp.dot(hb, w_ref[:, j * cw:(j + 1) * cw], preferred_element_type=F32)

    u = proj(4) * proj(5)
    rows = lax.broadcasted_iota(jnp.int32, u.shape, 0)
    u1 = pltpu.roll(u, 1, axis=0)
    u2 = pltpu.roll(u, 2, axis=0)
    if multi_seq:
        assert seq_len & (seq_len - 1) == 0
        t = rows & (seq_len - 1)
        u1 = jnp.where(t == 0, pe1_ref[...], u1)
        u2 = jnp.where(t < 2, pe2_ref[...], u2)
        u_ref[...] = u
    else:
        p0 = carry_ref[SUBLANES - 2:SUBLANES - 1, :]
        p1 = carry_ref[SUBLANES - 1:SUBLANES, :]
        u1 = jnp.where(rows == 0, p1, u1)
        u2 = jnp.where(rows == 0, p0, jnp.where(rows == 1, p1, u2))
        carry_ref[...] = u[tm - SUBLANES:tm, :]
        cs_ref[...] = u[tm - 2:tm, :]

    yc = u2 * cw_ref[0:1, :] + u1 * cw_ref[1:2, :] + u * cw_ref[2:3, :]

    if multi_seq:
        k_ref[...] = proj(1)
        v_ref[...] = proj(2)
    else:
        n_page, n_heads, hd, page = kt_ref.shape
        yk = proj(1)
        kb_ref[...] = yk.astype(BF16)
        for j in range(tm // MOBA_BLOCK):
            km_ref[j] = jnp.mean(yk[j * MOBA_BLOCK:(j + 1) * MOBA_BLOCK, :], axis=0, keepdims=True)
        yv = proj(2)
        for y, t_ref in ((yk, kt_ref), (yv, vt_ref)):
            yt = y.T
            for j in range(n_page):
                t_ref[j] = yt[:, j * page:(j + 1) * page].reshape(n_heads, hd, page)
    q_ref[...] = proj(0)
    gated_ref[...] = (proj(3) * yc).astype(BF16)


def _front_prompt(x, mods, g, w_in, conv_w, prev, layer, tm, n_heads, page):
    n, d = x.shape
    nb = prev.shape[0]
    cw = prev.shape[2]
    nt = n // nb // tm
    assert tm % page == 0 and tm % MOBA_BLOCK == 0
    row = lambda b, i: (b * nt + i, 0)
    pages = (tm // page, n_heads, cw // n_heads, page)
    page_map = lambda b, i: (b * nt + i, 0, 0, 0)
    n_mean = tm // MOBA_BLOCK
    return pl.pallas_call(
        functools.partial(_front_kernel, False, 0),
        grid=(nb, nt),
        in_specs=[
            pl.BlockSpec((tm, d), row),
            pl.BlockSpec((None, None, 6, 1, d), lambda b, i: (layer, b, 0, 0, 0)),
            _resident(g.shape),
            _resident(w_in.shape),
            _resident(conv_w.shape),
            pl.BlockSpec((None, CONV_W - 1, cw), lambda b, i: (b, 0, 0)),
        ],
        out_specs=[
            pl.BlockSpec((tm, cw), row),
            pl.BlockSpec((tm, cw), row),
            pl.BlockSpec(pages, page_map),
            pl.BlockSpec(pages, page_map),
            pl.BlockSpec((n_mean, 1, cw), lambda b, i: (b * nt + i, 0, 0)),
            pl.BlockSpec((tm, cw), row),
            pl.BlockSpec((None, CONV_W - 1, cw), lambda b, i: (b, 0, 0)),
        ],
        out_shape=[
            jax.ShapeDtypeStruct((n, cw), F32),
            jax.ShapeDtypeStruct((n, cw), BF16),
            jax.ShapeDtypeStruct((n // page,) + pages[1:], F32),
            jax.ShapeDtypeStruct((n // page,) + pages[1:], F32),
            jax.ShapeDtypeStruct((n // MOBA_BLOCK, 1, cw), F32),
            jax.ShapeDtypeStruct((n, cw), BF16),
            jax.ShapeDtypeStruct(prev.shape, F32),
        ],
        scratch_shapes=[pltpu.VMEM((SUBLANES, cw), F32)],
        compiler_params=_params(("arbitrary", "arbitrary")),
        name="front_prompt",
    )(x, mods, g, w_in, conv_w, prev)


def _front_sample(x, mods_rows, g, w_in, conv_w, pe1, pe2, layer, seq_len):
    n, d = x.shape
    cw = pe1.shape[1]
    full = lambda i: (0, 0)
    return pl.pallas_call(
        functools.partial(_front_kernel, True, seq_len),
        grid=(1,),
        in_specs=[
            pl.BlockSpec((n, d), full),
            pl.BlockSpec((None, 6, n, d), lambda i: (layer, 0, 0, 0)),
            _resident(g.shape),
            _resident(w_in.shape),
            _resident(conv_w.shape),
            pl.BlockSpec((n, cw), full),
            pl.BlockSpec((n, cw), full),
        ],
        out_specs=[pl.BlockSpec((n, cw), full)] * 5,
        out_shape=[
            jax.ShapeDtypeStruct((n, cw), F32),
            jax.ShapeDtypeStruct((n, cw), F32),
            jax.ShapeDtypeStruct((n, cw), F32),
            jax.ShapeDtypeStruct((n, cw), BF16),
            jax.ShapeDtypeStruct((n, cw), F32),
        ],
        compiler_params=_params(("arbitrary",)),
        name="front_sample",
    )(x, mods_rows, g, w_in, conv_w, pe1, pe2)


def _moba_item(step, lag, n_items, n_half, n_pair):
    item = jnp.clip(step - lag, 0, n_items - 1)
    seq_pair = item // n_half
    return seq_pair // n_pair, seq_pair % n_pair, item % n_half


def _moba_prompt_kernel(n_blk, n_pair, slopes_ref, qa_ref, qb_ref, k_ref, v_ref, km_ref, o_ref,
                        kb_ref, vt_ref, qf_ref, st_ref, um_ref, m_ref, acc_ref):
    step = pl.program_id(0)
    n_items = pl.num_programs(0) - 1
    n_half = n_blk // 2
    blk = MOBA_BLOCK
    seq = k_ref.shape[0]
    hd = v_ref.shape[2]
    page = v_ref.shape[3]
    ppb = blk // page
    n_prev = n_blk - 1
    scale = float(hd) ** -0.5
    b, p, j = _moba_item(step, 0, n_items, n_half, n_pair)
    b_old, p_old, j_old = _moba_item(step, 1, n_items, n_half, n_pair)
    slot = (b * n_pair + p) % 2
    slot_old = (b_old * n_pair + p_old) % 2
    tiles = (j, n_blk - 1 - j)
    tiles_old = (j_old, n_blk - 1 - j_old)

    @pl.when(step == 0)
    def _():
        r = lax.broadcasted_iota(jnp.int32, (seq, LANES), 0)
        c = lax.broadcasted_iota(jnp.int32, (seq, LANES), 1)
        n = lax.shift_right_logical(r, MOBA_BLOCK_SHIFT)
        off = r & (blk - 1)
        feat = jnp.where(c == n, 1.0,
                         jnp.where(c == n_blk, n.astype(F32),
                                   jnp.where(c == n_blk + 1, off.astype(F32),
                                             jnp.where(c == n_blk + 2, 1.0, 0.0))))
        kb_ref[:, LANES:2 * LANES] = feat.astype(BF16)
        st_ref[...] = jnp.zeros_like(st_ref)
        m_ref[...] = jnp.zeros_like(m_ref)
        vt_ref[:, :, :, hd:, :] = jnp.ones(vt_ref.shape[:3] + (ONES_ROWS, blk), BF16)

    @pl.when((j == 0) & (step < n_items))
    def _():
        kb_ref[:, 0:LANES] = k_ref[...]
        for n in range(n_blk):
            for e in range(2):
                vt_ref[slot, n, e, 0:hd, :] = jnp.concatenate(
                    [v_ref[n * ppb + g, e] for g in range(ppb)], axis=1).astype(BF16)

    _moba_setup(n_blk, scale, tiles, p, slopes_ref, qa_ref, qb_ref, km_ref, qf_ref)

    n_grp = blk // SUBLANES
    wide = 2 * blk
    feat_lane = lax.broadcasted_iota(jnp.int32, (1, 2 * LANES), 1) - LANES
    ind_off = jnp.where((feat_lane >= 0) & (feat_lane < n_blk), 0.0, 1.0).astype(BF16)
    key_row = lax.broadcasted_iota(jnp.int32, (blk, wide), 0)
    q_col = lax.broadcasted_iota(jnp.int32, (blk, wide), 1) & (blk - 1)

    def unit_ids(u, jj, item_tiles):
        if u >= n_prev:
            return u - n_prev, item_tiles[u - n_prev]
        is_a = u < jj
        return jnp.where(is_a, 0, 1), jnp.where(is_a, u, u - jj)

    acc_ref[...] = jnp.zeros_like(acc_ref)
    for u in range(n_prev + 2):
        t_old, n_old = unit_ids(u, j_old, tiles_old)
        pr = jnp.exp(st_ref[u].reshape(n_grp, SUBLANES, wide) - m_ref[t_old][None])
        pb = pr.reshape(blk, wide).astype(BF16)
        for e in range(2):
            acc_ref[t_old, e] = acc_ref[t_old, e] + jnp.dot(
                vt_ref[slot_old, n_old, e], pb[:, e * blk:(e + 1) * blk],
                preferred_element_type=F32)
        t_new, n_new = unit_ids(u, j, tiles)
        own = u >= n_prev
        kblk = kb_ref[pl.ds(pl.multiple_of(n_new * blk, blk), blk), :]
        if own:
            kblk = kblk * ind_off
        s = jnp.dot(kblk, qf_ref[t_new], preferred_element_type=F32)
        if own:
            s = jnp.where(key_row <= q_col, s, NEG_BIG)
        st_ref[u] = s
        um_ref[u] = jnp.max(s.reshape(n_grp, SUBLANES, wide), axis=0)

    for t in range(2):
        out_t = jnp.concatenate(
            [acc_ref[t, e, 0:hd, :] / acc_ref[t, e, hd:hd + 1, :] for e in range(2)], axis=0)
        o_ref[pl.ds(pl.multiple_of(tiles_old[t] * blk, blk), blk), :] = (
            out_t.T.astype(o_ref.dtype))
    m_new = [um_ref[n_prev], um_ref[n_prev + 1]]
    for u in range(n_prev):
        val = um_ref[u]
        is_a = u < j
        m_new[0] = jnp.maximum(m_new[0], jnp.where(is_a, val, -jnp.inf))
        m_new[1] = jnp.maximum(m_new[1], jnp.where(is_a, -jnp.inf, val))
    for t in range(2):
        m_ref[t] = jnp.broadcast_to(jnp.max(m_new[t], axis=0, keepdims=True), m_new[t].shape)


def _moba_setup(n_blk, scale, tiles, pair, slopes_ref, qa_ref, qb_ref, km_ref, qf_ref):
    blk = MOBA_BLOCK
    hd = LANES // 2
    km = km_ref[...]
    lane = lax.broadcasted_iota(jnp.int32, (1, LANES), 1)
    km_heads = jnp.concatenate([km * (lane < hd).astype(F32), km * (lane >= hd).astype(F32)],
                               axis=0)
    km_hi, km_lo = _split_bf16(km_heads)
    km_hi_lo = jnp.concatenate([km_hi, km_lo], axis=0)
    qrow = lax.broadcasted_iota(jnp.int32, (LANES, blk), 0)
    head_rows = [(qrow < hd).astype(F32), (qrow >= hd).astype(F32)]
    blk_row = lax.broadcasted_iota(jnp.int32, (n_blk, blk), 0)
    blk_row_f = blk_row.astype(F32)
    for t, q_ref in enumerate((qa_ref, qb_ref)):
        qt = q_ref[...].T
        qt_hi, qt_lo = _split_bf16(qt)
        g_hi = jnp.dot(km_hi_lo, qt_hi, preferred_element_type=F32)
        gate_both = (g_hi[0:2 * n_blk] + g_hi[2 * n_blk:]
                     + jnp.dot(km_hi, qt_lo, preferred_element_type=F32))
        tile_f = tiles[t].astype(F32)
        valid = blk_row < tiles[t]
        for e in range(2):
            slope = slopes_ref[2 * pair + e]
            gate_t = jnp.where(valid, gate_both[e * n_blk:(e + 1) * n_blk], -jnp.inf)
            taken = jnp.zeros((n_blk, blk), F32)
            for _ in range(MOBA_TOPK):
                is_max = gate_t == jnp.max(gate_t, axis=0, keepdims=True)
                first = jnp.min(jnp.where(is_max, blk_row_f, float(n_blk)), axis=0,
                                keepdims=True)
                pick = blk_row_f == first
                taken = jnp.where(pick, 1.0, taken)
                gate_t = jnp.where(pick, -jnp.inf, gate_t)
            mask_t = jnp.where(valid, jnp.where(taken > 0.0, 0.0, NEG_BIG), NEG_BIG)
            consts = jnp.where(qrow == n_blk, slope * blk,
                               jnp.where(qrow == n_blk + 1, slope,
                                         jnp.where(qrow == n_blk + 2, -slope * blk * tile_f, 0.0)))
            feat_t = jnp.concatenate([mask_t, jnp.zeros((LANES - n_blk, blk), F32)],
                                     axis=0) + consts
            q_e = (qt * (head_rows[e] * scale)).astype(BF16)
            qf_ref[t, :, e * blk:(e + 1) * blk] = jnp.concatenate(
                [q_e, feat_t.astype(BF16)], axis=0)


def _moba_prompt(q, k_rows, v_pages, k_mean, slopes, batch):
    n, aw = q.shape
    _, n_heads, hd, page = v_pages.shape
    seq = n // batch
    blk = MOBA_BLOCK
    n_blk = seq // blk
    assert seq % blk == 0 and blk % page == 0 and 2 * hd == LANES
    assert n_blk % SUBLANES == 0 and n_blk + 3 <= LANES
    n_pair = aw // LANES
    n_half = n_blk // 2
    n_items = batch * n_pair * n_half
    n_unit = n_blk + 1

    def item_map(lag, fn):
        return lambda s, _: fn(*_moba_item(s, lag, n_items, n_half, n_pair))

    return pl.pallas_call(
        functools.partial(_moba_prompt_kernel, n_blk, n_pair),
        grid_spec=pltpu.PrefetchScalarGridSpec(
            num_scalar_prefetch=1,
            grid=(n_items + 1,),
            in_specs=[
                pl.BlockSpec((blk, LANES), item_map(0, lambda b, p, j: (b * n_blk + j, p))),
                pl.BlockSpec((blk, LANES),
                             item_map(0, lambda b, p, j: (b * n_blk + n_blk - 1 - j, p))),
                pl.BlockSpec((seq, LANES), item_map(0, lambda b, p, j: (b, p))),
                pl.BlockSpec((seq // page, 2, hd, page),
                             item_map(0, lambda b, p, j: (b, p, 0, 0))),
                pl.BlockSpec((n_blk, LANES), item_map(0, lambda b, p, j: (b, p))),
            ],
            out_specs=pl.BlockSpec((seq, LANES), item_map(1, lambda b, p, j: (b, p))),
            scratch_shapes=[
                pltpu.VMEM((seq, 2 * LANES), BF16),
                pltpu.VMEM((2, n_blk, 2, hd + ONES_ROWS, blk), BF16),
                pltpu.VMEM((2, 2 * LANES, 2 * blk), BF16),
                pltpu.VMEM((n_unit, blk, 2 * blk), F32),
                pltpu.VMEM((n_unit, SUBLANES, 2 * blk), F32),
                pltpu.VMEM((2, SUBLANES, 2 * blk), F32),
                pltpu.VMEM((2, 2, hd + ONES_ROWS, blk), F32),
            ],
        ),
        out_shape=jax.ShapeDtypeStruct((n, aw), BF16),
        compiler_params=_params(("arbitrary",)),
        name="moba_prompt",
    )(slopes, q, q, k_rows, v_pages, k_mean)


def _moba_sample_kernel(n_pages, n_heads, pt_ref, q_ref, kn_ref, vn_ref, slope_ref,
                        ck_hbm, cv_hbm, o_ref, kbuf, vsel, s_ref, km_ref, bias_ref, idx_ref,
                        idx_smem, stat_ref, own_ref, orow_ref, ksem, vsem, isem):
    step = pl.program_id(0)
    n_samples = pl.num_programs(0) - 1
    blk = MOBA_BLOCK
    tq, aw = q_ref.shape
    page = kbuf.shape[3]
    ppb = blk // page
    hd = aw // n_heads
    rows = tq * n_heads
    past = n_pages * page
    n_blk = past // blk
    n_sel = min(MOBA_TOPK, n_blk)
    slot = step % 2
    slot_old = 1 - slot
    feat0 = n_blk

    def k_copy(sample, pg):
        return pltpu.make_async_copy(ck_hbm.at[pt_ref[sample, pg]], kbuf.at[sample % 2, pg],
                                     ksem.at[sample % 2])

    def for_pages(fn):
        def go(pg, c):
            fn(pg)
            return c
        lax.fori_loop(0, n_pages, go, 0)

    def idx_copy(sl):
        return pltpu.make_async_copy(idx_ref.at[sl], idx_smem.at[sl], isem)

    def v_copies(sample, sl, row_range=range(rows)):
        for r in row_range:
            for j in range(n_sel):
                first_page = idx_smem[sl, r, j] * ppb
                for g in range(ppb):
                    yield pltpu.make_async_copy(
                        cv_hbm.at[pt_ref[sample, first_page + g],
                                  pl.ds((r % n_heads) * hd, hd), :],
                        vsel.at[sl, r * n_sel + j, g], vsem.at[sl])

    lane = lax.broadcasted_iota(jnp.int32, (n_heads, aw), 1)
    hrow = lax.broadcasted_iota(jnp.int32, (n_heads, aw), 0)
    head_mask = jnp.where(lane >= hrow * hd, jnp.where(lane < (hrow + 1) * hd, 1.0, 0.0), 0.0)
    hm_rows = jnp.concatenate([head_mask] * tq, axis=0)

    @pl.when(step == 0)
    def _():
        for_pages(lambda pg: k_copy(0, pg).start())
        blk_id = lax.broadcasted_iota(jnp.int32, bias_ref.shape, 0)
        f = lax.broadcasted_iota(jnp.int32, bias_ref.shape, 1)
        off = lax.broadcasted_iota(jnp.int32, bias_ref.shape, 2)
        bias_ref[...] = jnp.where(
            f == blk_id, 1.0,
            jnp.where(f == feat0, (blk_id * blk).astype(F32),
                      jnp.where(f == feat0 + 1, off.astype(F32),
                                jnp.where((f == feat0 + 2) | (f == feat0 + 3), 1.0, 0.0)))
        ).astype(BF16)

    @pl.when(step < n_samples)
    def _():
        _moba_sample_keys(step, slot, n_pages, n_heads, n_sel, hm_rows, q_ref, kn_ref, vn_ref,
                          slope_ref, kbuf, s_ref, km_ref, bias_ref, idx_ref, stat_ref, own_ref,
                          k_copy, for_pages)
        idx_copy(slot).start()

    def values_and_gathers(do_values, do_gathers):
        if do_values:
            for copy in v_copies(step - 1, slot_old):
                copy.wait()
            inv_l = 1.0 / stat_ref[slot_old]
        hidden_rows = rows // 4 if do_values else 0
        folded = []
        for r in range(rows):
            if do_values:
                acc = jnp.zeros((hd, blk), F32)
                for j in range(n_sel):
                    pr = s_ref[slot_old, idx_smem[slot_old, r, j], r:r + 1, :]
                    v_blk = jnp.concatenate(
                        [vsel[slot_old, r * n_sel + j, g] for g in range(ppb)], axis=1)
                    acc = acc + pr * v_blk
                fold = functools.reduce(
                    jnp.add, [acc[:, c * LANES:(c + 1) * LANES] for c in range(blk // LANES)])
                folded.append(fold * inv_l[r:r + 1, 0:1])
                if r % n_heads == n_heads - 1:
                    tok = r // n_heads
                    tok_t = jnp.concatenate(folded, axis=0).T
                    folded = []
                    orow_ref[tok:tok + 1, :] = (jnp.sum(tok_t, axis=0, keepdims=True)
                                                + own_ref[slot_old, tok:tok + 1, :])
            if do_gathers and r >= hidden_rows:
                if r == hidden_rows:
                    idx_copy(slot).wait()
                first = 0 if r == hidden_rows else r
                for copy in v_copies(step, slot, range(first, r + 1)):
                    copy.start()
        if do_values:
            o_ref[...] = orow_ref[0:tq, :].astype(o_ref.dtype)

    pl.when(step == 0)(lambda: values_and_gathers(False, True))
    pl.when((step > 0) & (step < n_samples))(lambda: values_and_gathers(True, True))
    pl.when(step == n_samples)(lambda: values_and_gathers(True, False))


def _moba_sample_keys(b, slot, n_pages, n_heads, n_sel, hm_rows, q_ref, kn_ref, vn_ref,
                      slope_ref, kbuf, s_ref, km_ref, bias_ref, idx_ref, stat_ref, own_ref,
                      k_copy, for_pages):
    blk = MOBA_BLOCK
    tq, aw = q_ref.shape
    page = kbuf.shape[3]
    ppb = blk // page
    hd = aw // n_heads
    rows = tq * n_heads
    past = n_pages * page
    n_blk = past // blk
    feat0 = n_blk
    q = q_ref[...]
    q_rows = jnp.concatenate(
        [jnp.broadcast_to(q[t:t + 1, :], (n_heads, aw)) for t in range(tq)], axis=0) * hm_rows
    qs_bf = (q_rows * (float(hd) ** -0.5)).astype(BF16)
    slope = jnp.concatenate([slope_ref[:, 0:1]] * tq, axis=0)
    t_row = jnp.concatenate([jnp.full((n_heads, 1), t, jnp.int32) for t in range(tq)], axis=0)

    for_pages(lambda pg: k_copy(b, pg).wait())

    @pl.when(b + 2 < pl.num_programs(0))
    def _():
        for_pages(lambda pg: k_copy(b + 1, pg).start())

    km_ref[...] = jnp.zeros_like(km_ref)
    for n in range(n_blk):
        pages_n = [kbuf[slot, n * ppb + g] for g in range(ppb)]
        km_ref[:, n:n + 1] = jnp.sum(functools.reduce(jnp.add, pages_n), axis=1,
                                     keepdims=True) * (1.0 / blk)
        s_ref[slot, n] = jnp.dot(qs_bf, jnp.concatenate(pages_n, axis=1).astype(BF16),
                                 preferred_element_type=F32)
    km_t = km_ref[...]

    q_hi, q_lo = _split_bf16(q_rows)
    km_hi, km_lo = _split_bf16(km_t)
    gate = (jnp.dot(q_hi, km_hi, preferred_element_type=F32)
            + jnp.dot(q_hi, km_lo, preferred_element_type=F32)
            + jnp.dot(q_lo, km_hi, preferred_element_type=F32))
    blk_lane = lax.broadcasted_iota(jnp.int32, gate.shape, 1)
    gate = jnp.where(blk_lane < n_blk, gate, -jnp.inf)
    lane_f = blk_lane.astype(F32)
    taken = jnp.zeros(gate.shape, F32)
    picked = jnp.zeros(gate.shape, F32)
    for j in range(n_sel):
        is_max = gate == jnp.max(gate, axis=1, keepdims=True)
        first = jnp.min(jnp.where(is_max, lane_f, float(LANES)), axis=1, keepdims=True)
        pick = lane_f == first
        taken = jnp.where(pick, 1.0, taken)
        gate = jnp.where(pick, -jnp.inf, gate)
        picked = jnp.where(blk_lane == j, first, picked)
    idx_ref[slot] = picked.astype(jnp.int32)
    slope_l = jnp.broadcast_to(slope, gate.shape)
    t_l = jnp.broadcast_to(t_row, gate.shape).astype(F32)
    feat = jnp.where(
        blk_lane < n_blk, jnp.where(taken > 0.0, 0.0, NEG_BIG),
        jnp.where((blk_lane == feat0) | (blk_lane == feat0 + 1), slope_l,
                  jnp.where(blk_lane == feat0 + 2, -slope_l * past,
                            jnp.where(blk_lane == feat0 + 3, -slope_l * t_l, 0.0))))
    feat_bf = feat.astype(BF16)

    kn = kn_ref[...]
    vn = vn_ref[...]
    q_sc = q_rows * (float(hd) ** -0.5)
    own = []
    for t in range(tq):
        so = jnp.sum(q_sc * kn[t:t + 1, :], axis=1, keepdims=True)
        so = so - slope * (t_row - t).astype(F32)
        own.append(jnp.where(t_row >= t, so, NEG_BIG))
    m_elem = jnp.full((rows, blk), NEG_BIG, F32)
    for n in range(n_blk):
        s = s_ref[slot, n] + jnp.dot(feat_bf, bias_ref[n], preferred_element_type=F32)
        s_ref[slot, n] = s
        m_elem = jnp.maximum(m_elem, s)
    m_run = jnp.maximum(functools.reduce(jnp.maximum, own),
                        jnp.max(m_elem, axis=1, keepdims=True))

    acc = jnp.zeros((rows, aw), F32)
    l_run = jnp.zeros((rows, 1), F32)
    for t in range(tq):
        po = jnp.exp(own[t] - m_run)
        l_run = l_run + po
        acc = acc + po * vn[t:t + 1, :]
    m_l = jnp.broadcast_to(m_run, (rows, blk))
    l_elem = jnp.zeros((rows, blk), F32)
    for n in range(n_blk):
        pr = jnp.exp(s_ref[slot, n] - m_l)
        s_ref[slot, n] = pr
        l_elem = l_elem + pr
    l_run = l_run + jnp.sum(l_elem, axis=1, keepdims=True)
    stat_ref[slot] = jnp.broadcast_to(l_run, gate.shape)
    own_rows = acc * hm_rows / l_run
    own_ref[slot, 0:tq, :] = jnp.sum(own_rows.reshape(tq, n_heads, aw), axis=1)


def _moba_sample(q, k_new, v_new, cache_k, cache_v, page_table, slope_tab, n_heads):
    bd, tq, aw = q.shape
    n_pages = page_table.shape[1]
    page = cache_k.shape[2]
    past = n_pages * page
    assert MOBA_BLOCK % page == 0 and past % MOBA_BLOCK == 0 and page == LANES
    assert past // MOBA_BLOCK + 4 <= LANES
    rows = tq * n_heads
    n_blk = past // MOBA_BLOCK
    n_sel = min(MOBA_TOPK, n_blk)
    hd = aw // n_heads
    assert tq <= LANES and rows % SUBLANES == 0
    key_tok = lambda s, pt: (jnp.minimum(s, bd - 1), 0, 0)
    val_tok = lambda s, pt: (jnp.maximum(s - 1, 0), 0, 0)
    return pl.pallas_call(
        functools.partial(_moba_sample_kernel, n_pages, n_heads),
        grid_spec=pltpu.PrefetchScalarGridSpec(
            num_scalar_prefetch=1,
            grid=(bd + 1,),
            in_specs=[
                pl.BlockSpec((None, tq, aw), key_tok),
                pl.BlockSpec((None, tq, aw), key_tok),
                pl.BlockSpec((None, tq, aw), key_tok),
                pl.BlockSpec(slope_tab.shape, lambda s, pt: (0, 0)),
                pl.BlockSpec(memory_space=pl.ANY),
                pl.BlockSpec(memory_space=pl.ANY),
            ],
            out_specs=pl.BlockSpec((None, tq, aw), val_tok),
            scratch_shapes=[
                pltpu.VMEM((2, n_pages, aw, page), F32),
                pltpu.VMEM((2, rows * n_sel, MOBA_BLOCK // page, hd, page), F32),
                pltpu.VMEM((2, n_blk, rows, MOBA_BLOCK), F32),
                pltpu.VMEM((aw, LANES), F32),
                pltpu.VMEM((n_blk, LANES, MOBA_BLOCK), BF16),
                pltpu.VMEM((2, rows, LANES), jnp.int32),
                pltpu.SMEM((2, rows, LANES), jnp.int32),
                pltpu.VMEM((2, rows, LANES), F32),
                pltpu.VMEM((2, SUBLANES, aw), F32),
                pltpu.VMEM((SUBLANES, aw), F32),
                pltpu.SemaphoreType.DMA((2,)),
                pltpu.SemaphoreType.DMA((2,)),
                pltpu.SemaphoreType.DMA(()),
            ],
        ),
        out_shape=jax.ShapeDtypeStruct((bd, tq, aw), BF16),
        compiler_params=_params(("arbitrary",)),
        name="moba_sample",
    )(page_table, q, k_new, v_new, slope_tab, cache_k, cache_v)


def _row_parts(tm):
    n_part = 2 if tm % (2 * MOBA_BLOCK) == 0 else 1
    return [slice(i * (tm // n_part), (i + 1) * (tm // n_part)) for i in range(n_part)]


def _mod_rows(mod_ref, i, rs):
    return mod_ref[i] if mod_ref.shape[1] == 1 else mod_ref[i, rs, :]


def _ffn_input(x1, mod_ref, rs, g):
    return (_rms(x1, g) * (1.0 + _mod_rows(mod_ref, 4, rs))
            + _mod_rows(mod_ref, 3, rs)).astype(BF16)


def _ffn_residual(x1, h, mod_ref, rs, w1_ref, w2_ref, n_chunk):
    ck = w1_ref.shape[1] // n_chunk
    acc = jnp.zeros(x1.shape, F32)
    for j in range(n_chunk):
        hid = jnp.dot(h, w1_ref[:, j * ck:(j + 1) * ck], preferred_element_type=F32)
        hid = jnp.square(jnp.maximum(hid, 0.0)).astype(BF16)
        acc = acc + jnp.dot(hid, w2_ref[j * ck:(j + 1) * ck, :], preferred_element_type=F32)
    return x1 + _mod_rows(mod_ref, 5, rs) * acc


def _back_ab_kernel(n_chunk, x_ref, attn_ref, gated_ref, mod_ref, g_ref, wo_ref, w1_ref, w2_ref,
                    o_ref):
    aw = attn_ref.shape[1]
    parts = _row_parts(x_ref.shape[0])
    x1s = []
    for rs in parts:
        mix = (jnp.dot(attn_ref[rs, :], wo_ref[0:aw, :], preferred_element_type=F32)
               + jnp.dot(gated_ref[rs, :], wo_ref[aw:, :], preferred_element_type=F32))
        x1s.append(x_ref[rs, :] + _mod_rows(mod_ref, 2, rs) * mix)
    hs = [_ffn_input(x1, mod_ref, rs, g_ref[...]) for x1, rs in zip(x1s, parts)]
    for x1, h, rs in zip(x1s, hs, parts):
        o_ref[rs, :] = _ffn_residual(x1, h, mod_ref, rs, w1_ref, w2_ref, n_chunk)


def _mod_spec(mods, layer, n_rows_per_group, tm):
    d = mods.shape[-1]
    if mods.ndim == 5:
        nt = n_rows_per_group // tm
        return pl.BlockSpec((None, None, 6, 1, d), lambda r: (layer, r // nt, 0, 0, 0))
    return pl.BlockSpec((None, 6, tm, d), lambda r: (layer, 0, r, 0))


def _back_ab(x, attn, gated, mods, g, w_out, w1, w2, layer, rows_per_group, tm):
    n, d = x.shape
    aw = attn.shape[1]
    cw = gated.shape[1]
    row = lambda r: (r, 0)
    return pl.pallas_call(
        functools.partial(_back_ab_kernel, 4),
        grid=(n // tm,),
        in_specs=[
            pl.BlockSpec((tm, d), row),
            pl.BlockSpec((tm, aw), row),
            pl.BlockSpec((tm, cw), row),
            _mod_spec(mods, layer, rows_per_group, tm),
            _resident(g.shape),
            _resident(w_out.shape),
            _resident_layer(w1.shape, layer),
            _resident_layer(w2.shape, layer),
        ],
        out_specs=pl.BlockSpec((tm, d), row),
        out_shape=jax.ShapeDtypeStruct((n, d), F32),
        compiler_params=_params(("arbitrary",)),
        name="back_ab",
    )(x, attn, gated, mods, g, w_out, w1, w2)


def _gmlp_kernel(n_chunk, emit_v, x_ref, mod_ref, g1_ref, g2_ref, wi_ref, gv_ref, ws_ref, bs_ref,
                 wo_ref, w1_ref, w2_ref, gf_ref, *rest):
    if emit_v:
        o_ref, cv_ref, us_ref = rest
    else:
        o_ref, us_ref = rest
    tm = x_ref.shape[0]
    cwid = gv_ref.shape[1]
    n_grp, chunk, _ = ws_ref.shape
    grp = cwid // n_grp
    x = x_ref[...]
    hb = (_rms(x, g1_ref[...]) * (1.0 + mod_ref[1]) + mod_ref[0]).astype(BF16)
    v = _rms(jnp.dot(hb, wi_ref[:, cwid:], preferred_element_type=F32), gv_ref[...])
    u = jnp.dot(hb, wi_ref[:, 0:cwid], preferred_element_type=F32)
    if emit_v:
        cv_ref[...] = v
    vb = v.astype(BF16)
    n_chunks = tm // chunk
    side = 2 if n_chunks % 2 == 0 else 1
    for c in range(0, n_chunks, side):
        rss = [slice((c + i) * chunk, (c + i + 1) * chunk) for i in range(side)]
        for gi in range(n_grp):
            ls = slice(gi * grp, (gi + 1) * grp)
            rhs = jnp.concatenate([vb[rs, ls] for rs in rss], axis=1)
            s = jnp.dot(ws_ref[gi], rhs, preferred_element_type=F32)
            for i, rs in enumerate(rss):
                s_i = s[:, i * grp:(i + 1) * grp] + bs_ref[gi]
                us_ref[rs, ls] = (u[rs, ls] * s_i).astype(BF16)
    parts = _row_parts(tm)
    x1s = []
    for rs in parts:
        mix = jnp.dot(us_ref[rs, :], wo_ref[...], preferred_element_type=F32)
        x1s.append(x[rs] + _mod_rows(mod_ref, 2, rs) * mix)
    hs = [_ffn_input(x1, mod_ref, rs, g2_ref[...]) for x1, rs in zip(x1s, parts)]
    for x1, h, rs in zip(x1s, hs, parts):
        x2 = _ffn_residual(x1, h, mod_ref, rs, w1_ref, w2_ref, n_chunk)
        o_ref[rs, :] = _rms(x2, gf_ref[...])


def _gmlp_layer(x, mods, g1, g2, w_in, g_v, ws_mat, bs_full, w_out, w1, w2, g_final, layer,
                rows_per_group, tm, emit_v):
    n, d = x.shape
    cwid = g_v.shape[1]
    row = lambda r: (r, 0)
    out_shape = [jax.ShapeDtypeStruct((n, d), F32)]
    out_specs = [pl.BlockSpec((tm, d), row)]
    if emit_v:
        out_shape.append(jax.ShapeDtypeStruct((n, cwid), F32))
        out_specs.append(pl.BlockSpec((tm, cwid), row))
    return pl.pallas_call(
        functools.partial(_gmlp_kernel, 4, emit_v),
        grid=(n // tm,),
        in_specs=[
            pl.BlockSpec((tm, d), row),
            _mod_spec(mods, layer, rows_per_group, tm),
            _resident(g1.shape),
            _resident(g2.shape),
            _resident(w_in.shape),
            _resident(g_v.shape),
            _resident(ws_mat.shape),
            _resident(bs_full.shape),
            _resident(w_out.shape),
            _resident_layer(w1.shape, layer),
            _resident_layer(w2.shape, layer),
            _resident(g_final.shape),
        ],
        out_specs=out_specs,
        out_shape=out_shape,
        scratch_shapes=[pltpu.VMEM((tm, cwid), BF16)],
        compiler_params=_params(("arbitrary",)),
        name="gmlp_layer",
    )(x, mods, g1, g2, w_in, g_v, ws_mat, bs_full, w_out, w1, w2, g_final)


def _alibi_slopes(n_heads):
    return jnp.exp2(-8.0 * jnp.arange(1, n_heads + 1, dtype=F32) / n_heads)


def kernel(x_prompt, x_sample, cache_k, cache_v, state_conv, page_table, c_prompt, c_sample,
           norm_g, w_ada, b_ada, w_in_ab, conv_w, w_out_ab, w_in_c, g_v, w_s, b_s, w_out_c,
           w_ff1, w_ff2, g_final):
    batch, seq, d = x_prompt.shape
    bd, tq, _ = x_sample.shape
    n_ab, n_pool, page, n_heads, hd = cache_k.shape
    aw = n_heads * hd
    cw = state_conv.shape[-1]
    n_c, n_grp, chunk, _ = w_s.shape
    cwid = g_v.shape[-1]
    depth = norm_g.shape[0]
    assert depth == 2 and n_ab == 1 and n_c == 1 and hd * 2 == LANES
    assert tq <= chunk and chunk % tq == 0 and (bd * tq) % chunk == 0
    n_p, n_s = batch * seq, bd * tq
    tm = 512 if seq % 512 == 0 else chunk

    wb = lambda w: w.astype(BF16)
    w_in_ab_b, w_out_ab_b = wb(w_in_ab[0]), wb(w_out_ab[0])
    w_in_c_b, w_out_c_b = wb(w_in_c[0]), wb(w_out_c[0])
    w_ff1_b, w_ff2_b = wb(w_ff1), wb(w_ff2)

    bp = -(-(n_s + batch) // SUBLANES) * SUBLANES
    c_all = jnp.concatenate([jnp.repeat(c_sample, tq, axis=0), c_prompt,
                             jnp.zeros((bp - n_s - batch, d), F32)], axis=0)
    ada = _ada_terms(c_all, w_ada, b_ada)
    mods_p = ada[:, :, n_s:n_s + batch].transpose(0, 2, 1, 3).reshape(depth, batch, 6, 1, d)
    mods_s = ada

    slopes = _alibi_slopes(n_heads)
    slope_tab = jnp.broadcast_to(slopes[:, None], (n_heads, LANES))
    g = lambda l, j: norm_g[l, j].reshape(1, d)

    tril = jnp.tril(jnp.ones((chunk, chunk), bool))
    ws_p = jnp.where(tril, w_s[0], 0.0).astype(BF16)
    bs_p = jnp.broadcast_to(b_s[0][:, :, None], (n_grp, chunk, cwid // n_grp))
    ws_small = jnp.where(tril[:tq, :tq], w_s[0][:, :tq, :tq], 0.0)
    eye = jnp.eye(chunk // tq, dtype=F32)
    ws_s = jnp.einsum("ab,gts->gatbs", eye, ws_small).reshape(n_grp, chunk, chunk).astype(BF16)
    bs_s = jnp.broadcast_to(jnp.tile(b_s[0][:, :tq], (1, chunk // tq))[:, :, None],
                            (n_grp, chunk, cwid // n_grp))

    xp = x_prompt.reshape(n_p, d)
    prev0 = jnp.zeros((batch, CONV_W - 1, cw), F32)
    qp, kp_b, kp_t, vp_t, km_p, gated_p, conv_p = _front_prompt(
        xp, mods_p, g(0, 0), w_in_ab_b, conv_w[0], prev0, 0, tm, n_heads, page)
    attn_p = _moba_prompt(qp, kp_b, vp_t, km_p.reshape(n_p // MOBA_BLOCK, aw), slopes, batch)
    xp = _back_ab(xp, attn_p, gated_p, mods_p, g(0, 1), w_out_ab_b, w_ff1_b, w_ff2_b, 0,
                  seq, tm)
    (yp,) = _gmlp_layer(xp, mods_p, g(1, 0), g(1, 1), w_in_c_b, g_v, ws_p, bs_p, w_out_c_b,
                        w_ff1_b, w_ff2_b, g_final.reshape(1, d), 1, seq, tm, False)

    pool_pages = lambda c: c.transpose(0, 1, 3, 4, 2).reshape(n_pool, aw, page)
    xs = x_sample.reshape(n_s, d)
    st = state_conv[0]
    pad = lambda a: jnp.concatenate([a, jnp.zeros((bd, tq - a.shape[1], cw), F32)], axis=1)
    pe1 = pad(st[:, 1:2]).reshape(n_s, cw)
    pe2 = pad(st).reshape(n_s, cw)
    qs, ks, vs, gated_s, u_s = _front_sample(xs, mods_s, g(0, 0), w_in_ab_b, conv_w[0], pe1, pe2,
                                             0, tq)
    attn_s = _moba_sample(qs.reshape(bd, tq, aw), ks.reshape(bd, tq, aw), vs.reshape(bd, tq, aw),
                          pool_pages(cache_k), pool_pages(cache_v), page_table, slope_tab,
                          n_heads)
    xs = _back_ab(xs, attn_s.reshape(n_s, aw), gated_s, mods_s, g(0, 1), w_out_ab_b, w_ff1_b,
                  w_ff2_b, 0, n_s, chunk)
    ys, cv_s = _gmlp_layer(xs, mods_s, g(1, 0), g(1, 1), w_in_c_b, g_v, ws_s, bs_s, w_out_c_b,
                           w_ff1_b, w_ff2_b, g_final.reshape(1, d), 1, n_s, chunk, True)

    return (
        yp.reshape(batch, seq, d),
        ys.reshape(bd, tq, d),
        kp_t.transpose(0, 3, 1, 2)[None],
        vp_t.transpose(0, 3, 1, 2)[None],
        conv_p[None],
        ks.reshape(1, bd, tq, n_heads, hd),
        vs.reshape(1, bd, tq, n_heads, hd),
        u_s.reshape(bd, tq, cw)[:, tq - (CONV_W - 1):][None],
        cv_s.reshape(1, bd, tq, cwid),
    )
```

```python
import functools

import jax
import jax.numpy as jnp
from jax import lax
from jax.experimental import pallas as pl
from jax.experimental.pallas import tpu as pltpu

F32 = jnp.float32
BF16 = jnp.bfloat16

EPS = 1e-6
MOBA_BLOCK = 256
MOBA_BLOCK_SHIFT = 8
MOBA_TOPK = 3
CONV_W = 3
LANES = 128
SUBLANES = 8
ONES_ROWS = 2 * SUBLANES
NEG_BIG = -(2.0 ** 100)
VMEM_LIMIT_BYTES = 56 * 1024 * 1024

_NT = (((1,), (1,)), ((), ()))


def _params(sem):
    return pltpu.CompilerParams(dimension_semantics=sem, vmem_limit_bytes=VMEM_LIMIT_BYTES)


def _resident(shape):
    n = len(shape)
    return pl.BlockSpec(shape, lambda *_: (0,) * n, pipeline_mode=pl.Buffered(1))


def _resident_layer(shape, layer):
    n = len(shape) - 1
    return pl.BlockSpec((None,) + tuple(shape[1:]), lambda *_: (layer,) + (0,) * n,
                        pipeline_mode=pl.Buffered(1))


def _rms(x, g):
    return x * lax.rsqrt(jnp.mean(x * x, axis=-1, keepdims=True) + EPS) * g


def _split_bf16(x):
    hi = x.astype(BF16)
    lo = (x - hi.astype(F32)).astype(BF16)
    return hi, lo


def _ada_kernel(c_ref, w_ref, b_ref, o_ref):
    c = c_ref[...]
    s = (c * jax.nn.sigmoid(c)).astype(BF16)
    o_ref[0, 0] = jnp.dot(s, w_ref[0].astype(BF16), preferred_element_type=F32) + b_ref[0]


def _ada_terms(c_all, w_ada, b_ada):
    n_layers, d, d6 = w_ada.shape
    bp = c_all.shape[0]
    return pl.pallas_call(
        _ada_kernel,
        grid=(n_layers, d6 // d),
        in_specs=[
            pl.BlockSpec((bp, d), lambda l, j: (0, 0)),
            pl.BlockSpec((1, d, d), lambda l, j: (l, 0, j)),
            pl.BlockSpec((1, 1, d), lambda l, j: (l, 0, j)),
        ],
        out_specs=pl.BlockSpec((1, 1, bp, d), lambda l, j: (l, j, 0, 0)),
        out_shape=jax.ShapeDtypeStruct((n_layers, d6 // d, bp, d), F32),
        compiler_params=_params(("arbitrary", "arbitrary")),
        name="ada_terms",
    )(c_all, w_ada, b_ada.reshape(n_layers, 1, d6))


def _front_kernel(multi_seq, seq_len, *refs):
    if multi_seq:
        (x_ref, mod_ref, g_ref, w_ref, cw_ref, pe1_ref, pe2_ref,
         q_ref, k_ref, v_ref, gated_ref, u_ref) = refs
    else:
        (x_ref, mod_ref, g_ref, w_ref, cw_ref, prev_ref,
         q_ref, kb_ref, kt_ref, vt_ref, km_ref, gated_ref, cs_ref, carry_ref) = refs

        @pl.when(pl.program_id(1) == 0)
        def _():
            carry_ref[SUBLANES - 2:SUBLANES, :] = prev_ref[...]

    tm = x_ref.shape[0]
    cw = gated_ref.shape[1]
    parts = _row_parts(tm)
    hbs = [(_rms(x_ref[rs, :], g_ref[...]) * (1.0 + _mod_rows(mod_ref, 1, rs))
            + _mod_rows(mod_ref, 0, rs)).astype(BF16) for rs in parts]
    hb = jnp.concatenate(hbs, axis=0)

    def proj(j, rows=hb):
        return jnp.dot(rows, w_ref[:, j * cw:(j + 1) * cw], preferred_element_type=F32)

    u = jnp.concatenate([proj(4, part) for part in hbs], axis=0) * proj(5)
    rows = lax.broadcasted_iota(jnp.int32, u.shape, 0)
    u1 = pltpu.roll(u, 1, axis=0)
    u2 = pltpu.roll(u, 2, axis=0)
    if multi_seq:
        assert seq_len & (seq_len - 1) == 0
        t = rows & (seq_len - 1)
        u1 = jnp.where(t == 0, pe1_ref[...], u1)
        u2 = jnp.where(t < 2, pe2_ref[...], u2)
        u_ref[...] = u
    else:
        p0 = carry_ref[SUBLANES - 2:SUBLANES - 1, :]
        p1 = carry_ref[SUBLANES - 1:SUBLANES, :]
        u1 = jnp.where(rows == 0, p1, u1)
        u2 = jnp.where(rows == 0, p0, jnp.where(rows == 1, p1, u2))
        carry_ref[...] = u[tm - SUBLANES:tm, :]
        cs_ref[...] = u[tm - 2:tm, :]

    yc = u2 * cw_ref[0:1, :] + u1 * cw_ref[1:2, :] + u * cw_ref[2:3, :]

    if multi_seq:
        k_ref[...] = proj(1)
        v_ref[...] = proj(2)
    else:
        n_page, n_heads, hd, page = kt_ref.shape
        yk = proj(1)
        kb_ref[...] = yk.astype(BF16)
        for j in range(tm // MOBA_BLOCK):
            km_ref[j] = jnp.mean(yk[j * MOBA_BLOCK:(j + 1) * MOBA_BLOCK, :], axis=0, keepdims=True)
        yv = proj(2)
        for y, t_ref in ((yk, kt_ref), (yv, vt_ref)):
            yt = y.T
            for j in range(n_page):
                t_ref[j] = yt[:, j * page:(j + 1) * page].reshape(n_heads, hd, page)
    q_ref[...] = proj(0)
    gated_ref[...] = (proj(3) * yc).astype(BF16)


def _front_prompt(x, mods, g, w_in, conv_w, prev, layer, tm, n_heads, page):
    n, d = x.shape
    nb = prev.shape[0]
    cw = prev.shape[2]
    nt = n // nb // tm
    assert tm % page == 0 and tm % MOBA_BLOCK == 0
    row = lambda b, i: (b * nt + i, 0)
    pages = (tm // page, n_heads, cw // n_heads, page)
    page_map = lambda b, i: (b * nt + i, 0, 0, 0)
    n_mean = tm // MOBA_BLOCK
    return pl.pallas_call(
        functools.partial(_front_kernel, False, 0),
        grid=(nb, nt),
        in_specs=[
            pl.BlockSpec((tm, d), row),
            pl.BlockSpec((None, None, 6, 1, d), lambda b, i: (layer, b, 0, 0, 0)),
            _resident(g.shape),
            _resident(w_in.shape),
            _resident(conv_w.shape),
            pl.BlockSpec((None, CONV_W - 1, cw), lambda b, i: (b, 0, 0)),
        ],
        out_specs=[
            pl.BlockSpec((tm, cw), row),
            pl.BlockSpec((tm, cw), row),
            pl.BlockSpec(pages, page_map),
            pl.BlockSpec(pages, page_map),
            pl.BlockSpec((n_mean, 1, cw), lambda b, i: (b * nt + i, 0, 0)),
            pl.BlockSpec((tm, cw), row),
            pl.BlockSpec((None, CONV_W - 1, cw), lambda b, i: (b, 0, 0)),
        ],
        out_shape=[
            jax.ShapeDtypeStruct((n, cw), F32),
            jax.ShapeDtypeStruct((n, cw), BF16),
            jax.ShapeDtypeStruct((n // page,) + pages[1:], F32),
            jax.ShapeDtypeStruct((n // page,) + pages[1:], F32),
            jax.ShapeDtypeStruct((n // MOBA_BLOCK, 1, cw), F32),
            jax.ShapeDtypeStruct((n, cw), BF16),
            jax.ShapeDtypeStruct(prev.shape, F32),
        ],
        scratch_shapes=[pltpu.VMEM((SUBLANES, cw), F32)],
        compiler_params=_params(("arbitrary", "arbitrary")),
        name="front_prompt",
    )(x, mods, g, w_in, conv_w, prev)


def _front_sample(x, mods_rows, g, w_in, conv_w, pe1, pe2, layer, seq_len):
    n, d = x.shape
    cw = pe1.shape[1]
    full = lambda i: (0, 0)
    return pl.pallas_call(
        functools.partial(_front_kernel, True, seq_len),
        grid=(1,),
        in_specs=[
            pl.BlockSpec((n, d), full),
            pl.BlockSpec((None, 6, n, d), lambda i: (layer, 0, 0, 0)),
            _resident(g.shape),
            _resident(w_in.shape),
            _resident(conv_w.shape),
            pl.BlockSpec((n, cw), full),
            pl.BlockSpec((n, cw), full),
        ],
        out_specs=[pl.BlockSpec((n, cw), full)] * 5,
        out_shape=[
            jax.ShapeDtypeStruct((n, cw), F32),
            jax.ShapeDtypeStruct((n, cw), F32),
            jax.ShapeDtypeStruct((n, cw), F32),
            jax.ShapeDtypeStruct((n, cw), BF16),
            jax.ShapeDtypeStruct((n, cw), F32),
        ],
        compiler_params=_params(("arbitrary",)),
        name="front_sample",
    )(x, mods_rows, g, w_in, conv_w, pe1, pe2)


def _moba_item(step, lag, n_items, n_half, n_pair):
    item = jnp.clip(step - lag, 0, n_items - 1)
    seq_pair = item // n_half
    return seq_pair // n_pair, seq_pair % n_pair, item % n_half


def _moba_prompt_kernel(n_blk, n_pair, slopes_ref, qa_ref, qb_ref, k_ref, v_ref, km_ref, o_ref,
                        kb_ref, vt_ref, qf_ref, st_ref, um_ref, m_ref, acc_ref):
    step = pl.program_id(0)
    n_items = pl.num_programs(0) - 1
    n_half = n_blk // 2
    blk = MOBA_BLOCK
    seq = k_ref.shape[0]
    hd = v_ref.shape[2]
    page = v_ref.shape[3]
    ppb = blk // page
    n_prev = n_blk - 1
    scale = float(hd) ** -0.5
    b, p, j = _moba_item(step, 0, n_items, n_half, n_pair)
    b_old, p_old, j_old = _moba_item(step, 1, n_items, n_half, n_pair)
    slot = (b * n_pair + p) % 2
    slot_old = (b_old * n_pair + p_old) % 2
    tiles = (j, n_blk - 1 - j)
    tiles_old = (j_old, n_blk - 1 - j_old)

    @pl.when(step == 0)
    def _():
        r = lax.broadcasted_iota(jnp.int32, (seq, LANES), 0)
        c = lax.broadcasted_iota(jnp.int32, (seq, LANES), 1)
        n = lax.shift_right_logical(r, MOBA_BLOCK_SHIFT)
        off = r & (blk - 1)
        feat = jnp.where(c == n, 1.0,
                         jnp.where(c == n_blk, n.astype(F32),
                                   jnp.where(c == n_blk + 1, off.astype(F32),
                                             jnp.where(c == n_blk + 2, 1.0, 0.0))))
        kb_ref[:, LANES:2 * LANES] = feat.astype(BF16)
        st_ref[...] = jnp.zeros_like(st_ref)
        m_ref[...] = jnp.zeros_like(m_ref)
        vt_ref[:, :, :, hd:, :] = jnp.ones(vt_ref.shape[:3] + (ONES_ROWS, blk), BF16)

    @pl.when((j == 0) & (step < n_items))
    def _():
        kb_ref[:, 0:LANES] = k_ref[...]
        for n in range(n_blk):
            for e in range(2):
                vt_ref[slot, n, e, 0:hd, :] = jnp.concatenate(
                    [v_ref[n * ppb + g, e] for g in range(ppb)], axis=1).astype(BF16)

    _moba_setup(n_blk, scale, tiles, p, slopes_ref, qa_ref, qb_ref, km_ref, qf_ref)

    n_grp = blk // SUBLANES
    wide = 2 * blk
    feat_lane = lax.broadcasted_iota(jnp.int32, (1, 2 * LANES), 1) - LANES
    ind_off = jnp.where((feat_lane >= 0) & (feat_lane < n_blk), 0.0, 1.0).astype(BF16)
    key_row = lax.broadcasted_iota(jnp.int32, (blk, wide), 0)
    q_col = lax.broadcasted_iota(jnp.int32, (blk, wide), 1) & (blk - 1)

    def unit_ids(u, jj, item_tiles):
        if u >= n_prev:
            return u - n_prev, item_tiles[u - n_prev]
        is_a = u < jj
        return jnp.where(is_a, 0, 1), jnp.where(is_a, u, u - jj)

    acc_ref[...] = jnp.zeros_like(acc_ref)
    for u in range(n_prev + 2):
        t_old, n_old = unit_ids(u, j_old, tiles_old)
        pr = jnp.exp(st_ref[u].reshape(n_grp, SUBLANES, wide) - m_ref[t_old][None])
        pb = pr.reshape(blk, wide).astype(BF16)
        for e in range(2):
            acc_ref[t_old, e] = acc_ref[t_old, e] + jnp.dot(
                vt_ref[slot_old, n_old, e], pb[:, e * blk:(e + 1) * blk],
                preferred_element_type=F32)
        t_new, n_new = unit_ids(u, j, tiles)
        own = u >= n_prev
        kblk = kb_ref[pl.ds(pl.multiple_of(n_new * blk, blk), blk), :]
        if own:
            kblk = kblk * ind_off
        s = jnp.dot(kblk, qf_ref[t_new], preferred_element_type=F32)
        if own:
            s = jnp.where(key_row <= q_col, s, NEG_BIG)
        st_ref[u] = s
        um_ref[u] = jnp.max(s.reshape(n_grp, SUBLANES, wide), axis=0)

    for t in range(2):
        out_t = jnp.concatenate(
            [acc_ref[t, e, 0:hd, :] / acc_ref[t, e, hd:hd + 1, :] for e in range(2)], axis=0)
        o_ref[pl.ds(pl.multiple_of(tiles_old[t] * blk, blk), blk), :] = (
            out_t.T.astype(o_ref.dtype))
    m_new = [um_ref[n_prev], um_ref[n_prev + 1]]
    for u in range(n_prev):
        val = um_ref[u]
        is_a = u < j
        m_new[0] = jnp.maximum(m_new[0], jnp.where(is_a, val, -jnp.inf))
        m_new[1] = jnp.maximum(m_new[1], jnp.where(is_a, -jnp.inf, val))
    for t in range(2):
        m_ref[t] = jnp.broadcast_to(jnp.max(m_new[t], axis=0, keepdims=True), m_new[t].shape)


def _moba_setup(n_blk, scale, tiles, pair, slopes_ref, qa_ref, qb_ref, km_ref, qf_ref):
    blk = MOBA_BLOCK
    hd = LANES // 2
    km = km_ref[...]
    lane = lax.broadcasted_iota(jnp.int32, (1, LANES), 1)
    km_heads = jnp.concatenate([km * (lane < hd).astype(F32), km * (lane >= hd).astype(F32)],
                               axis=0)
    km_hi, km_lo = _split_bf16(km_heads)
    km_hi_lo = jnp.concatenate([km_hi, km_lo], axis=0)
    qrow = lax.broadcasted_iota(jnp.int32, (LANES, blk), 0)
    head_rows = [(qrow < hd).astype(F32), (qrow >= hd).astype(F32)]
    blk_row = lax.broadcasted_iota(jnp.int32, (n_blk, blk), 0)
    blk_row_f = blk_row.astype(F32)
    for t, q_ref in enumerate((qa_ref, qb_ref)):
        qt = q_ref[...].T
        qt_hi, qt_lo = _split_bf16(qt)
        g_hi = jnp.dot(km_hi_lo, qt_hi, preferred_element_type=F32)
        gate_both = (g_hi[0:2 * n_blk] + g_hi[2 * n_blk:]
                     + jnp.dot(km_hi, qt_lo, preferred_element_type=F32))
        tile_f = tiles[t].astype(F32)
        valid = blk_row < tiles[t]
        for e in range(2):
            slope = slopes_ref[2 * pair + e]
            gate_t = jnp.where(valid, gate_both[e * n_blk:(e + 1) * n_blk], -jnp.inf)
            taken = jnp.zeros((n_blk, blk), F32)
            for _ in range(MOBA_TOPK):
                is_max = gate_t == jnp.max(gate_t, axis=0, keepdims=True)
                first = jnp.min(jnp.where(is_max, blk_row_f, float(n_blk)), axis=0,
                                keepdims=True)
                pick = blk_row_f == first
                taken = jnp.where(pick, 1.0, taken)
                gate_t = jnp.where(pick, -jnp.inf, gate_t)
            mask_t = jnp.where(valid, jnp.where(taken > 0.0, 0.0, NEG_BIG), NEG_BIG)
            consts = jnp.where(qrow == n_blk, slope * blk,
                               jnp.where(qrow == n_blk + 1, slope,
                                         jnp.where(qrow == n_blk + 2, -slope * blk * tile_f, 0.0)))
            feat_t = jnp.concatenate([mask_t, jnp.zeros((LANES - n_blk, blk), F32)],
                                     axis=0) + consts
            q_e = (qt * (head_rows[e] * scale)).astype(BF16)
            qf_ref[t, :, e * blk:(e + 1) * blk] = jnp.concatenate(
                [q_e, feat_t.astype(BF16)], axis=0)


def _moba_prompt(q, k_rows, v_pages, k_mean, slopes, batch):
    n, aw = q.shape
    _, n_heads, hd, page = v_pages.shape
    seq = n // batch
    blk = MOBA_BLOCK
    n_blk = seq // blk
    assert seq % blk == 0 and blk % page == 0 and 2 * hd == LANES
    assert n_blk % SUBLANES == 0 and n_blk + 3 <= LANES
    n_pair = aw // LANES
    n_half = n_blk // 2
    n_items = batch * n_pair * n_half
    n_unit = n_blk + 1

    def item_map(lag, fn):
        return lambda s, _: fn(*_moba_item(s, lag, n_items, n_half, n_pair))

    return pl.pallas_call(
        functools.partial(_moba_prompt_kernel, n_blk, n_pair),
        grid_spec=pltpu.PrefetchScalarGridSpec(
            num_scalar_prefetch=1,
            grid=(n_items + 1,),
            in_specs=[
                pl.BlockSpec((blk, LANES), item_map(0, lambda b, p, j: (b * n_blk + j, p))),
                pl.BlockSpec((blk, LANES),
                             item_map(0, lambda b, p, j: (b * n_blk + n_blk - 1 - j, p))),
                pl.BlockSpec((seq, LANES), item_map(0, lambda b, p, j: (b, p))),
                pl.BlockSpec((seq // page, 2, hd, page),
                             item_map(0, lambda b, p, j: (b, p, 0, 0))),
                pl.BlockSpec((n_blk, LANES), item_map(0, lambda b, p, j: (b, p))),
            ],
            out_specs=pl.BlockSpec((seq, LANES), item_map(1, lambda b, p, j: (b, p))),
            scratch_shapes=[
                pltpu.VMEM((seq, 2 * LANES), BF16),
                pltpu.VMEM((2, n_blk, 2, hd + ONES_ROWS, blk), BF16),
                pltpu.VMEM((2, 2 * LANES, 2 * blk), BF16),
                pltpu.VMEM((n_unit, blk, 2 * blk), F32),
                pltpu.VMEM((n_unit, SUBLANES, 2 * blk), F32),
                pltpu.VMEM((2, SUBLANES, 2 * blk), F32),
                pltpu.VMEM((2, 2, hd + ONES_ROWS, blk), F32),
            ],
        ),
        out_shape=jax.ShapeDtypeStruct((n, aw), BF16),
        compiler_params=_params(("arbitrary",)),
        name="moba_prompt",
    )(slopes, q, q, k_rows, v_pages, k_mean)


def _moba_sample_kernel(n_pages, n_heads, pt_ref, q_ref, kn_ref, vn_ref, slope_ref,
                        ck_hbm, cv_hbm, o_ref, kbuf, vsel, s_ref, km_ref, bias_ref, idx_ref,
                        idx_smem, stat_ref, own_ref, orow_ref, ksem, vsem, isem):
    step = pl.program_id(0)
    n_samples = pl.num_programs(0) - 1
    blk = MOBA_BLOCK
    tq, aw = q_ref.shape
    page = kbuf.shape[3]
    ppb = blk // page
    hd = aw // n_heads
    rows = tq * n_heads
    past = n_pages * page
    n_blk = past // blk
    n_sel = min(MOBA_TOPK, n_blk)
    slot = step % 2
    slot_old = 1 - slot
    feat0 = n_blk

    def k_copy(sample, pg):
        return pltpu.make_async_copy(ck_hbm.at[pt_ref[sample, pg]], kbuf.at[sample % 2, pg],
                                     ksem.at[sample % 2])

    def for_pages(fn):
        def go(pg, c):
            fn(pg)
            return c
        lax.fori_loop(0, n_pages, go, 0)

    def idx_copy(sl):
        return pltpu.make_async_copy(idx_ref.at[sl], idx_smem.at[sl], isem)

    def v_copies(sample, sl, row_range=range(rows)):
        for r in row_range:
            for j in range(n_sel):
                first_page = idx_smem[sl, r, j] * ppb
                for g in range(ppb):
                    yield pltpu.make_async_copy(
                        cv_hbm.at[pt_ref[sample, first_page + g],
                                  pl.ds((r % n_heads) * hd, hd), :],
                        vsel.at[sl, r * n_sel + j, g], vsem.at[sl])

    lane = lax.broadcasted_iota(jnp.int32, (n_heads, aw), 1)
    hrow = lax.broadcasted_iota(jnp.int32, (n_heads, aw), 0)
    head_mask = jnp.where(lane >= hrow * hd, jnp.where(lane < (hrow + 1) * hd, 1.0, 0.0), 0.0)
    hm_rows = jnp.concatenate([head_mask] * tq, axis=0)

    @pl.when(step == 0)
    def _():
        for_pages(lambda pg: k_copy(0, pg).start())
        blk_id = lax.broadcasted_iota(jnp.int32, bias_ref.shape, 0)
        f = lax.broadcasted_iota(jnp.int32, bias_ref.shape, 1)
        off = lax.broadcasted_iota(jnp.int32, bias_ref.shape, 2)
        bias_ref[...] = jnp.where(
            f == blk_id, 1.0,
            jnp.where(f == feat0, (blk_id * blk).astype(F32),
                      jnp.where(f == feat0 + 1, off.astype(F32),
                                jnp.where((f == feat0 + 2) | (f == feat0 + 3), 1.0, 0.0)))
        ).astype(BF16)

    @pl.when(step < n_samples)
    def _():
        _moba_sample_keys(step, slot, n_pages, n_heads, n_sel, hm_rows, q_ref, kn_ref, vn_ref,
                          slope_ref, kbuf, s_ref, km_ref, bias_ref, idx_ref, stat_ref, own_ref,
                          k_copy, for_pages)
        idx_copy(slot).start()

    def values_and_gathers(do_values, do_gathers):
        if do_values:
            for copy in v_copies(step - 1, slot_old):
                copy.wait()
            inv_l = 1.0 / stat_ref[slot_old]
        hidden_rows = rows // 4 if do_values else 0
        folded = []
        for r in range(rows):
            if do_values:
                acc = jnp.zeros((hd, blk), F32)
                for j in range(n_sel):
                    pr = s_ref[slot_old, idx_smem[slot_old, r, j], r:r + 1, :]
                    v_blk = jnp.concatenate(
                        [vsel[slot_old, r * n_sel + j, g] for g in range(ppb)], axis=1)
                    acc = acc + pr * v_blk
                fold = functools.reduce(
                    jnp.add, [acc[:, c * LANES:(c + 1) * LANES] for c in range(blk // LANES)])
                folded.append(fold * inv_l[r:r + 1, 0:1])
                if r % n_heads == n_heads - 1:
                    tok = r // n_heads
                    tok_t = jnp.concatenate(folded, axis=0).T
                    folded = []
                    orow_ref[tok:tok + 1, :] = (jnp.sum(tok_t, axis=0, keepdims=True)
                                                + own_ref[slot_old, tok:tok + 1, :])
            if do_gathers and r >= hidden_rows:
                if r == hidden_rows:
                    idx_copy(slot).wait()
                first = 0 if r == hidden_rows else r
                for copy in v_copies(step, slot, range(first, r + 1)):
                    copy.start()
        if do_values:
            o_ref[...] = orow_ref[0:tq, :].astype(o_ref.dtype)

    pl.when(step == 0)(lambda: values_and_gathers(False, True))
    pl.when((step > 0) & (step < n_samples))(lambda: values_and_gathers(True, True))
    pl.when(step == n_samples)(lambda: values_and_gathers(True, False))


def _moba_sample_keys(b, slot, n_pages, n_heads, n_sel, hm_rows, q_ref, kn_ref, vn_ref,
                      slope_ref, kbuf, s_ref, km_ref, bias_ref, idx_ref, stat_ref, own_ref,
                      k_copy, for_pages):
    blk = MOBA_BLOCK
    tq, aw = q_ref.shape
    page = kbuf.shape[3]
    ppb = blk // page
    hd = aw // n_heads
    rows = tq * n_heads
    past = n_pages * page
    n_blk = past // blk
    feat0 = n_blk
    q = q_ref[...]
    q_rows = jnp.concatenate(
        [jnp.broadcast_to(q[t:t + 1, :], (n_heads, aw)) for t in range(tq)], axis=0) * hm_rows
    qs_bf = (q_rows * (float(hd) ** -0.5)).astype(BF16)
    slope = jnp.concatenate([slope_ref[:, 0:1]] * tq, axis=0)
    t_row = jnp.concatenate([jnp.full((n_heads, 1), t, jnp.int32) for t in range(tq)], axis=0)

    for_pages(lambda pg: k_copy(b, pg).wait())

    @pl.when(b + 2 < pl.num_programs(0))
    def _():
        for_pages(lambda pg: k_copy(b + 1, pg).start())

    km_ref[...] = jnp.zeros_like(km_ref)
    for n in range(n_blk):
        pages_n = [kbuf[slot, n * ppb + g] for g in range(ppb)]
        km_ref[:, n:n + 1] = jnp.sum(functools.reduce(jnp.add, pages_n), axis=1,
                                     keepdims=True) * (1.0 / blk)
        s_ref[slot, n] = jnp.dot(qs_bf, jnp.concatenate(pages_n, axis=1).astype(BF16),
                                 preferred_element_type=F32)
    km_t = km_ref[...]

    q_hi, q_lo = _split_bf16(q_rows)
    km_hi, km_lo = _split_bf16(km_t)
    gate = (jnp.dot(q_hi, km_hi, preferred_element_type=F32)
            + jnp.dot(q_hi, km_lo, preferred_element_type=F32)
            + jnp.dot(q_lo, km_hi, preferred_element_type=F32))
    blk_lane = lax.broadcasted_iota(jnp.int32, gate.shape, 1)
    gate = jnp.where(blk_lane < n_blk, gate, -jnp.inf)
    lane_f = blk_lane.astype(F32)
    taken = jnp.zeros(gate.shape, F32)
    picked = jnp.zeros(gate.shape, F32)
    for j in range(n_sel):
        is_max = gate == jnp.max(gate, axis=1, keepdims=True)
        first = jnp.min(jnp.where(is_max, lane_f, float(LANES)), axis=1, keepdims=True)
        pick = lane_f == first
        taken = jnp.where(pick, 1.0, taken)
        gate = jnp.where(pick, -jnp.inf, gate)
        picked = jnp.where(blk_lane == j, first, picked)
    idx_ref[slot] = picked.astype(jnp.int32)
    slope_l = jnp.broadcast_to(slope, gate.shape)
    t_l = jnp.broadcast_to(t_row, gate.shape).astype(F32)
    feat = jnp.where(
        blk_lane < n_blk, jnp.where(taken > 0.0, 0.0, NEG_BIG),
        jnp.where((blk_lane == feat0) | (blk_lane == feat0 + 1), slope_l,
                  jnp.where(blk_lane == feat0 + 2, -slope_l * past,
                            jnp.where(blk_lane == feat0 + 3, -slope_l * t_l, 0.0))))
    feat_bf = feat.astype(BF16)

    kn = kn_ref[...]
    vn = vn_ref[...]
    q_sc = q_rows * (float(hd) ** -0.5)
    own = []
    for t in range(tq):
        so = jnp.sum(q_sc * kn[t:t + 1, :], axis=1, keepdims=True)
        so = so - slope * (t_row - t).astype(F32)
        own.append(jnp.where(t_row >= t, so, NEG_BIG))
    m_elem = jnp.full((rows, blk), NEG_BIG, F32)
    for n in range(n_blk):
        s = s_ref[slot, n] + jnp.dot(feat_bf, bias_ref[n], preferred_element_type=F32)
        s_ref[slot, n] = s
        m_elem = jnp.maximum(m_elem, s)
    m_run = jnp.maximum(functools.reduce(jnp.maximum, own),
                        jnp.max(m_elem, axis=1, keepdims=True))

    acc = jnp.zeros((rows, aw), F32)
    l_run = jnp.zeros((rows, 1), F32)
    for t in range(tq):
        po = jnp.exp(own[t] - m_run)
        l_run = l_run + po
        acc = acc + po * vn[t:t + 1, :]
    m_l = jnp.broadcast_to(m_run, (rows, blk))
    l_elem = jnp.zeros((rows, blk), F32)
    for n in range(n_blk):
        pr = jnp.exp(s_ref[slot, n] - m_l)
        s_ref[slot, n] = pr
        l_elem = l_elem + pr
    l_run = l_run + jnp.sum(l_elem, axis=1, keepdims=True)
    stat_ref[slot] = jnp.broadcast_to(l_run, gate.shape)
    own_rows = acc * hm_rows / l_run
    own_ref[slot, 0:tq, :] = jnp.sum(own_rows.reshape(tq, n_heads, aw), axis=1)


def _moba_sample(q, k_new, v_new, cache_k, cache_v, page_table, slope_tab, n_heads):
    bd, tq, aw = q.shape
    n_pages = page_table.shape[1]
    page = cache_k.shape[2]
    past = n_pages * page
    assert MOBA_BLOCK % page == 0 and past % MOBA_BLOCK == 0 and page == LANES
    assert past // MOBA_BLOCK + 4 <= LANES
    rows = tq * n_heads
    n_blk = past // MOBA_BLOCK
    n_sel = min(MOBA_TOPK, n_blk)
    hd = aw // n_heads
    assert tq <= LANES and rows % SUBLANES == 0
    key_tok = lambda s, pt: (jnp.minimum(s, bd - 1), 0, 0)
    val_tok = lambda s, pt: (jnp.maximum(s - 1, 0), 0, 0)
    return pl.pallas_call(
        functools.partial(_moba_sample_kernel, n_pages, n_heads),
        grid_spec=pltpu.PrefetchScalarGridSpec(
            num_scalar_prefetch=1,
            grid=(bd + 1,),
            in_specs=[
                pl.BlockSpec((None, tq, aw), key_tok),
                pl.BlockSpec((None, tq, aw), key_tok),
                pl.BlockSpec((None, tq, aw), key_tok),
                pl.BlockSpec(slope_tab.shape, lambda s, pt: (0, 0)),
                pl.BlockSpec(memory_space=pl.ANY),
                pl.BlockSpec(memory_space=pl.ANY),
            ],
            out_specs=pl.BlockSpec((None, tq, aw), val_tok),
            scratch_shapes=[
                pltpu.VMEM((2, n_pages, aw, page), F32),
                pltpu.VMEM((2, rows * n_sel, MOBA_BLOCK // page, hd, page), F32),
                pltpu.VMEM((2, n_blk, rows, MOBA_BLOCK), F32),
                pltpu.VMEM((aw, LANES), F32),
                pltpu.VMEM((n_blk, LANES, MOBA_BLOCK), BF16),
                pltpu.VMEM((2, rows, LANES), jnp.int32),
                pltpu.SMEM((2, rows, LANES), jnp.int32),
                pltpu.VMEM((2, rows, LANES), F32),
                pltpu.VMEM((2, SUBLANES, aw), F32),
                pltpu.VMEM((SUBLANES, aw), F32),
                pltpu.SemaphoreType.DMA((2,)),
                pltpu.SemaphoreType.DMA((2,)),
                pltpu.SemaphoreType.DMA(()),
            ],
        ),
        out_shape=jax.ShapeDtypeStruct((bd, tq, aw), BF16),
        compiler_params=_params(("arbitrary",)),
        name="moba_sample",
    )(page_table, q, k_new, v_new, slope_tab, cache_k, cache_v)


def _row_parts(tm):
    n_part = 2 if tm % (2 * MOBA_BLOCK) == 0 else 1
    return [slice(i * (tm // n_part), (i + 1) * (tm // n_part)) for i in range(n_part)]


def _mod_rows(mod_ref, i, rs):
    return mod_ref[i] if mod_ref.shape[1] == 1 else mod_ref[i, rs, :]


def _ffn_input(x1, mod_ref, rs, g):
    return (_rms(x1, g) * (1.0 + _mod_rows(mod_ref, 4, rs))
            + _mod_rows(mod_ref, 3, rs)).astype(BF16)


def _ffn_residual(x1, h, mod_ref, rs, w1_ref, w2_ref, n_chunk):
    ck = w1_ref.shape[1] // n_chunk
    acc = jnp.zeros(x1.shape, F32)
    for j in range(n_chunk):
        hid = jnp.dot(h, w1_ref[:, j * ck:(j + 1) * ck], preferred_element_type=F32)
        hid = jnp.square(jnp.maximum(hid, 0.0)).astype(BF16)
        acc = acc + jnp.dot(hid, w2_ref[j * ck:(j + 1) * ck, :], preferred_element_type=F32)
    return x1 + _mod_rows(mod_ref, 5, rs) * acc


def _back_ab_kernel(n_chunk, x_ref, attn_ref, gated_ref, mod_ref, g_ref, wo_ref, w1_ref, w2_ref,
                    o_ref):
    aw = attn_ref.shape[1]
    parts = _row_parts(x_ref.shape[0])
    x1s = []
    for rs in parts:
        mix = (jnp.dot(attn_ref[rs, :], wo_ref[0:aw, :], preferred_element_type=F32)
               + jnp.dot(gated_ref[rs, :], wo_ref[aw:, :], preferred_element_type=F32))
        x1s.append(x_ref[rs, :] + _mod_rows(mod_ref, 2, rs) * mix)
    hs = [_ffn_input(x1, mod_ref, rs, g_ref[...]) for x1, rs in zip(x1s, parts)]
    for x1, h, rs in zip(x1s, hs, parts):
        o_ref[rs, :] = _ffn_residual(x1, h, mod_ref, rs, w1_ref, w2_ref, n_chunk)


def _both_groups(kernel_fn, n_prompt_refs, n_sample_refs, n_shared, n_out_p, n_out_s, *refs):
    i0 = n_prompt_refs
    i1 = i0 + n_sample_refs
    i2 = i1 + n_shared
    i3 = i2 + n_out_p
    i4 = i3 + n_out_s
    shared, scratch = refs[i1:i2], refs[i4:]
    last = pl.num_programs(0) - 1

    @pl.when(pl.program_id(0) < last)
    def _():
        kernel_fn(False, *refs[0:i0], *shared, *refs[i2:i3], *scratch)

    @pl.when(pl.program_id(0) == last)
    def _():
        kernel_fn(True, *refs[i0:i1], *shared, *refs[i3:i4], *scratch)


def _group_specs(mods_p, mods_s, layer, rows_per_group, n_p, n_s, tm):
    d = mods_p.shape[-1]
    nt = n_p // tm
    per_group = rows_per_group // tm
    tile = lambda r: jnp.minimum(r, nt - 1)
    prompt_rows = lambda width: pl.BlockSpec((tm, width), lambda r: (tile(r), 0))
    sample_rows = lambda width: pl.BlockSpec((n_s, width), lambda r: (0, 0))
    mod_p = pl.BlockSpec((None, None, 6, 1, d), lambda r: (layer, tile(r) // per_group, 0, 0, 0))
    mod_s = pl.BlockSpec((None, 6, n_s, d), lambda r: (layer, 0, 0, 0))
    return nt, prompt_rows, sample_rows, mod_p, mod_s


def _back_ab_group_kernel(n_chunk, is_sample, *refs):
    _back_ab_kernel(n_chunk, *refs)


def _back_ab(x, attn, gated, mods, xs, attn_s, gated_s, mods_s, g, w_out, w1, w2, layer,
             rows_per_group, tm):
    n, d = x.shape
    n_s = xs.shape[0]
    aw = attn.shape[1]
    cw = gated.shape[1]
    nt, prompt_rows, sample_rows, mod_p, mod_s = _group_specs(mods, mods_s, layer,
                                                              rows_per_group, n, n_s, tm)
    return pl.pallas_call(
        functools.partial(_both_groups, functools.partial(_back_ab_group_kernel, 4),
                          4, 4, 4, 1, 1),
        grid=(nt + 1,),
        in_specs=[
            prompt_rows(d), prompt_rows(aw), prompt_rows(cw), mod_p,
            sample_rows(d), sample_rows(aw), sample_rows(cw), mod_s,
            _resident(g.shape),
            _resident(w_out.shape),
            _resident_layer(w1.shape, layer),
            _resident_layer(w2.shape, layer),
        ],
        out_specs=[prompt_rows(d), sample_rows(d)],
        out_shape=[jax.ShapeDtypeStruct((n, d), F32), jax.ShapeDtypeStruct((n_s, d), F32)],
        compiler_params=_params(("arbitrary",)),
        name="back_ab",
    )(x, attn, gated, mods, xs, attn_s, gated_s, mods_s, g, w_out, w1, w2)


def _gmlp_kernel(n_chunk, emit_v, x_ref, mod_ref, g1_ref, g2_ref, wi_ref, gv_ref, ws_ref, bs_ref,
                 wo_ref, w1_ref, w2_ref, gf_ref, *rest):
    if emit_v:
        o_ref, cv_ref, us_ref = rest
    else:
        o_ref, us_ref = rest
    tm = x_ref.shape[0]
    cwid = gv_ref.shape[1]
    n_grp, chunk, _ = ws_ref.shape
    grp = cwid // n_grp
    x = x_ref[...]
    hbs = [(_rms(x[rs], g1_ref[...]) * (1.0 + _mod_rows(mod_ref, 1, rs))
            + _mod_rows(mod_ref, 0, rs)).astype(BF16) for rs in _row_parts(tm)]
    hb = jnp.concatenate(hbs, axis=0)
    v = _rms(jnp.concatenate([jnp.dot(part, wi_ref[:, cwid:], preferred_element_type=F32)
                              for part in hbs], axis=0), gv_ref[...])
    u = jnp.dot(hb, wi_ref[:, 0:cwid], preferred_element_type=F32)
    if emit_v:
        cv_ref[...] = v
    vb = v.astype(BF16)
    n_chunks = tm // chunk
    side = 2 if n_chunks % 2 == 0 else 1
    for c in range(0, n_chunks, side):
        rss = [slice((c + i) * chunk, (c + i + 1) * chunk) for i in range(side)]
        for gi in range(n_grp):
            ls = slice(gi * grp, (gi + 1) * grp)
            rhs = jnp.concatenate([vb[rs, ls] for rs in rss], axis=1)
            s = jnp.dot(ws_ref[gi], rhs, preferred_element_type=F32)
            for i, rs in enumerate(rss):
                s_i = s[:, i * grp:(i + 1) * grp] + bs_ref[gi]
                us_ref[rs, ls] = (u[rs, ls] * s_i).astype(BF16)
    parts = _row_parts(tm)
    x1s = []
    for rs in parts:
        mix = jnp.dot(us_ref[rs, :], wo_ref[...], preferred_element_type=F32)
        x1s.append(x[rs] + _mod_rows(mod_ref, 2, rs) * mix)
    hs = [_ffn_input(x1, mod_ref, rs, g2_ref[...]) for x1, rs in zip(x1s, parts)]
    for x1, h, rs in zip(x1s, hs, parts):
        x2 = _ffn_residual(x1, h, mod_ref, rs, w1_ref, w2_ref, n_chunk)
        o_ref[rs, :] = _rms(x2, gf_ref[...])


def _gmlp_group_kernel(n_chunk, is_sample, x_ref, mod_ref, ws_ref, bs_ref, g1_ref, g2_ref, wi_ref,
                       gv_ref, wo_ref, w1_ref, w2_ref, gf_ref, *rest):
    _gmlp_kernel(n_chunk, is_sample, x_ref, mod_ref, g1_ref, g2_ref, wi_ref, gv_ref, ws_ref, bs_ref,
                 wo_ref, w1_ref, w2_ref, gf_ref, *rest)


def _gmlp_layer(x, mods, ws_p, bs_p, xs, mods_s, ws_s, bs_s, g1, g2, w_in, g_v, w_out, w1, w2,
                g_final, layer, rows_per_group, tm):
    n, d = x.shape
    n_s = xs.shape[0]
    cwid = g_v.shape[1]
    nt, prompt_rows, sample_rows, mod_p, mod_s = _group_specs(mods, mods_s, layer,
                                                              rows_per_group, n, n_s, tm)
    return pl.pallas_call(
        functools.partial(_both_groups, functools.partial(_gmlp_group_kernel, 4),
                          4, 4, 8, 1, 2),
        grid=(nt + 1,),
        in_specs=[
            prompt_rows(d), mod_p, _resident(ws_p.shape), _resident(bs_p.shape),
            sample_rows(d), mod_s, _resident(ws_s.shape), _resident(bs_s.shape),
            _resident(g1.shape),
            _resident(g2.shape),
            _resident(w_in.shape),
            _resident(g_v.shape),
            _resident(w_out.shape),
            _resident_layer(w1.shape, layer),
            _resident_layer(w2.shape, layer),
            _resident(g_final.shape),
        ],
        out_specs=[prompt_rows(d), sample_rows(d), sample_rows(cwid)],
        out_shape=[jax.ShapeDtypeStruct((n, d), F32), jax.ShapeDtypeStruct((n_s, d), F32),
                   jax.ShapeDtypeStruct((n_s, cwid), F32)],
        scratch_shapes=[pltpu.VMEM((tm, cwid), BF16)],
        compiler_params=_params(("arbitrary",)),
        name="gmlp_layer",
    )(x, mods, ws_p, bs_p, xs, mods_s, ws_s, bs_s, g1, g2, w_in, g_v, w_out, w1, w2, g_final)


def _alibi_slopes(n_heads):
    return jnp.exp2(-8.0 * jnp.arange(1, n_heads + 1, dtype=F32) / n_heads)


def kernel(x_prompt, x_sample, cache_k, cache_v, state_conv, page_table, c_prompt, c_sample,
           norm_g, w_ada, b_ada, w_in_ab, conv_w, w_out_ab, w_in_c, g_v, w_s, b_s, w_out_c,
           w_ff1, w_ff2, g_final):
    batch, seq, d = x_prompt.shape
    bd, tq, _ = x_sample.shape
    n_ab, n_pool, page, n_heads, hd = cache_k.shape
    aw = n_heads * hd
    cw = state_conv.shape[-1]
    n_c, n_grp, chunk, _ = w_s.shape
    cwid = g_v.shape[-1]
    depth = norm_g.shape[0]
    assert depth == 2 and n_ab == 1 and n_c == 1 and hd * 2 == LANES
    assert tq <= chunk and chunk % tq == 0 and (bd * tq) % chunk == 0
    n_p, n_s = batch * seq, bd * tq
    tm = 512 if seq % 512 == 0 else chunk

    wb = lambda w: w.astype(BF16)
    w_in_ab_b, w_out_ab_b = wb(w_in_ab[0]), wb(w_out_ab[0])
    w_in_c_b, w_out_c_b = wb(w_in_c[0]), wb(w_out_c[0])
    w_ff1_b, w_ff2_b = wb(w_ff1), wb(w_ff2)

    bp = -(-(n_s + batch) // SUBLANES) * SUBLANES
    c_all = jnp.concatenate([jnp.repeat(c_sample, tq, axis=0), c_prompt,
                             jnp.zeros((bp - n_s - batch, d), F32)], axis=0)
    ada = _ada_terms(c_all, w_ada, b_ada)
    mods_p = ada[:, :, n_s:n_s + batch].transpose(0, 2, 1, 3).reshape(depth, batch, 6, 1, d)
    mods_s = ada

    slopes = _alibi_slopes(n_heads)
    slope_tab = jnp.broadcast_to(slopes[:, None], (n_heads, LANES))
    g = lambda l, j: norm_g[l, j].reshape(1, d)

    tril = jnp.tril(jnp.ones((chunk, chunk), bool))
    ws_p = jnp.where(tril, w_s[0], 0.0).astype(BF16)
    bs_p = jnp.broadcast_to(b_s[0][:, :, None], (n_grp, chunk, cwid // n_grp))
    ws_small = jnp.where(tril[:tq, :tq], w_s[0][:, :tq, :tq], 0.0)
    eye = jnp.eye(chunk // tq, dtype=F32)
    ws_s = jnp.einsum("ab,gts->gatbs", eye, ws_small).reshape(n_grp, chunk, chunk).astype(BF16)
    bs_s = jnp.broadcast_to(jnp.tile(b_s[0][:, :tq], (1, chunk // tq))[:, :, None],
                            (n_grp, chunk, cwid // n_grp))

    xp = x_prompt.reshape(n_p, d)
    prev0 = jnp.zeros((batch, CONV_W - 1, cw), F32)
    qp, kp_b, kp_t, vp_t, km_p, gated_p, conv_p = _front_prompt(
        xp, mods_p, g(0, 0), w_in_ab_b, conv_w[0], prev0, 0, tm, n_heads, page)
    attn_p = _moba_prompt(qp, kp_b, vp_t, km_p.reshape(n_p // MOBA_BLOCK, aw), slopes, batch)

    pool_pages = lambda c: c.transpose(0, 1, 3, 4, 2).reshape(n_pool, aw, page)
    xs = x_sample.reshape(n_s, d)
    st = state_conv[0]
    pad = lambda a: jnp.concatenate([a, jnp.zeros((bd, tq - a.shape[1], cw), F32)], axis=1)
    pe1 = pad(st[:, 1:2]).reshape(n_s, cw)
    pe2 = pad(st).reshape(n_s, cw)
    qs, ks, vs, gated_s, u_s = _front_sample(xs, mods_s, g(0, 0), w_in_ab_b, conv_w[0], pe1, pe2,
                                             0, tq)
    attn_s = _moba_sample(qs.reshape(bd, tq, aw), ks.reshape(bd, tq, aw), vs.reshape(bd, tq, aw),
                          pool_pages(cache_k), pool_pages(cache_v), page_table, slope_tab,
                          n_heads)

    xp, xs = _back_ab(xp, attn_p, gated_p, mods_p, xs, attn_s.reshape(n_s, aw), gated_s, mods_s,
                      g(0, 1), w_out_ab_b, w_ff1_b, w_ff2_b, 0, seq, tm)
    yp, ys, cv_s = _gmlp_layer(xp, mods_p, ws_p, bs_p, xs, mods_s, ws_s, bs_s, g(1, 0), g(1, 1),
                               w_in_c_b, g_v, w_out_c_b, w_ff1_b, w_ff2_b,
                               g_final.reshape(1, d), 1, seq, tm)

    return (
        yp.reshape(batch, seq, d),
        ys.reshape(bd, tq, d),
        kp_t.transpose(0, 3, 1, 2)[None],
        vp_t.transpose(0, 3, 1, 2)[None],
        conv_p[None],
        ks.reshape(1, bd, tq, n_heads, hd),
        vs.reshape(1, bd, tq, n_heads, hd),
        u_s.reshape(bd, tq, cw)[:, tq - (CONV_W - 1):][None],
        cv_s.reshape(1, bd, tq, cwid),
    )
```

```python
import functools

import jax
import jax.numpy as jnp
from jax import lax
from jax.experimental import pallas as pl
from jax.experimental.pallas import tpu as pltpu

F32 = jnp.float32
BF16 = jnp.bfloat16

EPS = 1e-6
MOBA_BLOCK = 256
MOBA_BLOCK_SHIFT = 8
MOBA_TOPK = 3
CONV_W = 3
LANES = 128
SUBLANES = 8
ONES_ROWS = 2 * SUBLANES
NEG_BIG = -(2.0 ** 100)
VMEM_LIMIT_BYTES = 56 * 1024 * 1024

_NT = (((1,), (1,)), ((), ()))


def _params(sem):
    return pltpu.CompilerParams(dimension_semantics=sem, vmem_limit_bytes=VMEM_LIMIT_BYTES)


def _resident(shape):
    n = len(shape)
    return pl.BlockSpec(shape, lambda *_: (0,) * n, pipeline_mode=pl.Buffered(1))


def _resident_layer(shape, layer):
    n = len(shape) - 1
    return pl.BlockSpec((None,) + tuple(shape[1:]), lambda *_: (layer,) + (0,) * n,
                        pipeline_mode=pl.Buffered(1))


def _rms(x, g):
    return x * lax.rsqrt(jnp.mean(x * x, axis=-1, keepdims=True) + EPS) * g


def _split_bf16(x):
    hi = x.astype(BF16)
    lo = (x - hi.astype(F32)).astype(BF16)
    return hi, lo


def _ada_kernel(c_ref, w_ref, b_ref, o_ref):
    c = c_ref[...]
    s = (c * jax.nn.sigmoid(c)).astype(BF16)
    o_ref[0, 0] = jnp.dot(s, w_ref[0].astype(BF16), preferred_element_type=F32) + b_ref[0]


def _ada_terms(c_all, w_ada, b_ada):
    n_layers, d, d6 = w_ada.shape
    bp = c_all.shape[0]
    return pl.pallas_call(
        _ada_kernel,
        grid=(n_layers, d6 // d),
        in_specs=[
            pl.BlockSpec((bp, d), lambda l, j: (0, 0)),
            pl.BlockSpec((1, d, d), lambda l, j: (l, 0, j)),
            pl.BlockSpec((1, 1, d), lambda l, j: (l, 0, j)),
        ],
        out_specs=pl.BlockSpec((1, 1, bp, d), lambda l, j: (l, j, 0, 0)),
        out_shape=jax.ShapeDtypeStruct((n_layers, d6 // d, bp, d), F32),
        compiler_params=_params(("arbitrary", "arbitrary")),
        name="ada_terms",
    )(c_all, w_ada, b_ada.reshape(n_layers, 1, d6))


def _front_kernel(multi_seq, seq_len, *refs):
    if multi_seq:
        (x_ref, mod_ref, g_ref, w_ref, cw_ref, pe1_ref, pe2_ref,
         q_ref, k_ref, v_ref, gated_ref, u_ref) = refs
    else:
        (x_ref, mod_ref, g_ref, w_ref, cw_ref, prev_ref,
         q_ref, kb_ref, kt_ref, vt_ref, km_ref, gated_ref, cs_ref, carry_ref) = refs

        @pl.when(pl.program_id(1) == 0)
        def _():
            carry_ref[SUBLANES - 2:SUBLANES, :] = prev_ref[...]

    tm = x_ref.shape[0]
    cw = gated_ref.shape[1]
    h = _rms(x_ref[...], g_ref[...]) * (1.0 + mod_ref[1]) + mod_ref[0]
    hb = h.astype(BF16)

    def proj(j):
        return jnp.dot(hb, w_ref[:, j * cw:(j + 1) * cw], preferred_element_type=F32)

    u = proj(4) * proj(5)
    rows = lax.broadcasted_iota(jnp.int32, u.shape, 0)
    u1 = pltpu.roll(u, 1, axis=0)
    u2 = pltpu.roll(u, 2, axis=0)
    if multi_seq:
        assert seq_len & (seq_len - 1) == 0
        t = rows & (seq_len - 1)
        u1 = jnp.where(t == 0, pe1_ref[...], u1)
        u2 = jnp.where(t < 2, pe2_ref[...], u2)
        u_ref[...] = u
    else:
        p0 = carry_ref[SUBLANES - 2:SUBLANES - 1, :]
        p1 = carry_ref[SUBLANES - 1:SUBLANES, :]
        u1 = jnp.where(rows == 0, p1, u1)
        u2 = jnp.where(rows == 0, p0, jnp.where(rows == 1, p1, u2))
        carry_ref[...] = u[tm - SUBLANES:tm, :]
        cs_ref[...] = u[tm - 2:tm, :]

    yc = u2 * cw_ref[0:1, :] + u1 * cw_ref[1:2, :] + u * cw_ref[2:3, :]

    if multi_seq:
        k_ref[...] = proj(1)
        v_ref[...] = proj(2)
    else:
        n_page, n_heads, hd, page = kt_ref.shape
        yk = proj(1)
        kb_ref[...] = yk.astype(BF16)
        for j in range(tm // MOBA_BLOCK):
            km_ref[j] = jnp.mean(yk[j * MOBA_BLOCK:(j + 1) * MOBA_BLOCK, :], axis=0, keepdims=True)
        yv = proj(2)
        for y, t_ref in ((yk, kt_ref), (yv, vt_ref)):
            yt = y.T
            for j in range(n_page):
                t_ref[j] = yt[:, j * page:(j + 1) * page].reshape(n_heads, hd, page)
    q_ref[...] = proj(0)
    gated_ref[...] = (proj(3) * yc).astype(BF16)


def _front_prompt(x, mods, g, w_in, conv_w, prev, layer, tm, n_heads, page):
    n, d = x.shape
    nb = prev.shape[0]
    cw = prev.shape[2]
    nt = n // nb // tm
    assert tm % page == 0 and tm % MOBA_BLOCK == 0
    row = lambda b, i: (b * nt + i, 0)
    pages = (tm // page, n_heads, cw // n_heads, page)
    page_map = lambda b, i: (b * nt + i, 0, 0, 0)
    n_mean = tm // MOBA_BLOCK
    return pl.pallas_call(
        functools.partial(_front_kernel, False, 0),
        grid=(nb, nt),
        in_specs=[
            pl.BlockSpec((tm, d), row),
            pl.BlockSpec((None, None, 6, 1, d), lambda b, i: (layer, b, 0, 0, 0)),
            _resident(g.shape),
            _resident(w_in.shape),
            _resident(conv_w.shape),
            pl.BlockSpec((None, CONV_W - 1, cw), lambda b, i: (b, 0, 0)),
        ],
        out_specs=[
            pl.BlockSpec((tm, cw), row),
            pl.BlockSpec((tm, cw), row),
            pl.BlockSpec(pages, page_map),
            pl.BlockSpec(pages, page_map),
            pl.BlockSpec((n_mean, 1, cw), lambda b, i: (b * nt + i, 0, 0)),
            pl.BlockSpec((tm, cw), row),
            pl.BlockSpec((None, CONV_W - 1, cw), lambda b, i: (b, 0, 0)),
        ],
        out_shape=[
            jax.ShapeDtypeStruct((n, cw), F32),
            jax.ShapeDtypeStruct((n, cw), BF16),
            jax.ShapeDtypeStruct((n // page,) + pages[1:], F32),
            jax.ShapeDtypeStruct((n // page,) + pages[1:], F32),
            jax.ShapeDtypeStruct((n // MOBA_BLOCK, 1, cw), F32),
            jax.ShapeDtypeStruct((n, cw), BF16),
            jax.ShapeDtypeStruct(prev.shape, F32),
        ],
        scratch_shapes=[pltpu.VMEM((SUBLANES, cw), F32)],
        compiler_params=_params(("arbitrary", "arbitrary")),
        name="front_prompt",
    )(x, mods, g, w_in, conv_w, prev)


def _front_sample(x, mods_rows, g, w_in, conv_w, pe1, pe2, layer, seq_len):
    n, d = x.shape
    cw = pe1.shape[1]
    full = lambda i: (0, 0)
    return pl.pallas_call(
        functools.partial(_front_kernel, True, seq_len),
        grid=(1,),
        in_specs=[
            pl.BlockSpec((n, d), full),
            pl.BlockSpec((None, 6, n, d), lambda i: (layer, 0, 0, 0)),
            _resident(g.shape),
            _resident(w_in.shape),
            _resident(conv_w.shape),
            pl.BlockSpec((n, cw), full),
            pl.BlockSpec((n, cw), full),
        ],
        out_specs=[pl.BlockSpec((n, cw), full)] * 5,
        out_shape=[
            jax.ShapeDtypeStruct((n, cw), F32),
            jax.ShapeDtypeStruct((n, cw), F32),
            jax.ShapeDtypeStruct((n, cw), F32),
            jax.ShapeDtypeStruct((n, cw), BF16),
            jax.ShapeDtypeStruct((n, cw), F32),
        ],
        compiler_params=_params(("arbitrary",)),
        name="front_sample",
    )(x, mods_rows, g, w_in, conv_w, pe1, pe2)


def _moba_item(step, lag, n_items, n_half, n_pair):
    item = jnp.clip(step - lag, 0, n_items - 1)
    seq_pair = item // n_half
    return seq_pair // n_pair, seq_pair % n_pair, item % n_half


def _moba_prompt_kernel(n_blk, n_pair, slopes_ref, qa_ref, qb_ref, k_ref, v_ref, km_ref, o_ref,
                        kb_ref, vt_ref, qf_ref, st_ref, um_ref, m_ref, acc_ref):
    step = pl.program_id(0)
    n_items = pl.num_programs(0) - 1
    n_half = n_blk // 2
    blk = MOBA_BLOCK
    seq = k_ref.shape[0]
    hd = v_ref.shape[2]
    page = v_ref.shape[3]
    ppb = blk // page
    n_prev = n_blk - 1
    scale = float(hd) ** -0.5
    b, p, j = _moba_item(step, 0, n_items, n_half, n_pair)
    b_old, p_old, j_old = _moba_item(step, 1, n_items, n_half, n_pair)
    slot = (b * n_pair + p) % 2
    slot_old = (b_old * n_pair + p_old) % 2
    tiles = (j, n_blk - 1 - j)
    tiles_old = (j_old, n_blk - 1 - j_old)

    @pl.when(step == 0)
    def _():
        r = lax.broadcasted_iota(jnp.int32, (seq, LANES), 0)
        c = lax.broadcasted_iota(jnp.int32, (seq, LANES), 1)
        n = lax.shift_right_logical(r, MOBA_BLOCK_SHIFT)
        off = r & (blk - 1)
        feat = jnp.where(c == n, 1.0,
                         jnp.where(c == n_blk, n.astype(F32),
                                   jnp.where(c == n_blk + 1, off.astype(F32),
                                             jnp.where(c == n_blk + 2, 1.0, 0.0))))
        kb_ref[:, LANES:2 * LANES] = feat.astype(BF16)
        st_ref[...] = jnp.zeros_like(st_ref)
        m_ref[...] = jnp.zeros_like(m_ref)
        vt_ref[:, :, :, hd:, :] = jnp.ones(vt_ref.shape[:3] + (ONES_ROWS, blk), BF16)

    @pl.when((j == 0) & (step < n_items))
    def _():
        kb_ref[:, 0:LANES] = k_ref[...]
        for n in range(n_blk):
            for e in range(2):
                vt_ref[slot, n, e, 0:hd, :] = jnp.concatenate(
                    [v_ref[n * ppb + g, e] for g in range(ppb)], axis=1).astype(BF16)

    _moba_setup(n_blk, scale, tiles, p, slopes_ref, qa_ref, qb_ref, km_ref, qf_ref)

    n_grp = blk // SUBLANES
    wide = 2 * blk
    feat_lane = lax.broadcasted_iota(jnp.int32, (1, 2 * LANES), 1) - LANES
    ind_off = jnp.where((feat_lane >= 0) & (feat_lane < n_blk), 0.0, 1.0).astype(BF16)
    key_row = lax.broadcasted_iota(jnp.int32, (blk, wide), 0)
    q_col = lax.broadcasted_iota(jnp.int32, (blk, wide), 1) & (blk - 1)

    def unit_ids(u, jj, item_tiles):
        if u >= n_prev:
            return u - n_prev, item_tiles[u - n_prev]
        is_a = u < jj
        return jnp.where(is_a, 0, 1), jnp.where(is_a, u, u - jj)

    acc_ref[...] = jnp.zeros_like(acc_ref)
    for u in range(n_prev + 2):
        t_old, n_old = unit_ids(u, j_old, tiles_old)
        pr = jnp.exp(st_ref[u].reshape(n_grp, SUBLANES, wide) - m_ref[t_old][None])
        pb = pr.reshape(blk, wide).astype(BF16)
        for e in range(2):
            acc_ref[t_old, e] = acc_ref[t_old, e] + jnp.dot(
                vt_ref[slot_old, n_old, e], pb[:, e * blk:(e + 1) * blk],
                preferred_element_type=F32)
        t_new, n_new = unit_ids(u, j, tiles)
        own = u >= n_prev
        kblk = kb_ref[pl.ds(pl.multiple_of(n_new * blk, blk), blk), :]
        if own:
            kblk = kblk * ind_off
        s = jnp.dot(kblk, qf_ref[t_new], preferred_element_type=F32)
        if own:
            s = jnp.where(key_row <= q_col, s, NEG_BIG)
        st_ref[u] = s
        um_ref[u] = jnp.max(s.reshape(n_grp, SUBLANES, wide), axis=0)

    for t in range(2):
        out_t = jnp.concatenate(
            [acc_ref[t, e, 0:hd, :] / acc_ref[t, e, hd:hd + 1, :] for e in range(2)], axis=0)
        o_ref[pl.ds(pl.multiple_of(tiles_old[t] * blk, blk), blk), :] = (
            out_t.T.astype(o_ref.dtype))
    m_new = [um_ref[n_prev], um_ref[n_prev + 1]]
    for u in range(n_prev):
        val = um_ref[u]
        is_a = u < j
        m_new[0] = jnp.maximum(m_new[0], jnp.where(is_a, val, -jnp.inf))
        m_new[1] = jnp.maximum(m_new[1], jnp.where(is_a, -jnp.inf, val))
    for t in range(2):
        m_ref[t] = jnp.broadcast_to(jnp.max(m_new[t], axis=0, keepdims=True), m_new[t].shape)


def _moba_setup(n_blk, scale, tiles, pair, slopes_ref, qa_ref, qb_ref, km_ref, qf_ref):
    blk = MOBA_BLOCK
    hd = LANES // 2
    km = km_ref[...]
    lane = lax.broadcasted_iota(jnp.int32, (1, LANES), 1)
    km_heads = jnp.concatenate([km * (lane < hd).astype(F32), km * (lane >= hd).astype(F32)],
                               axis=0)
    km_hi, km_lo = _split_bf16(km_heads)
    km_hi_lo = jnp.concatenate([km_hi, km_lo], axis=0)
    qrow = lax.broadcasted_iota(jnp.int32, (LANES, blk), 0)
    head_rows = [(qrow < hd).astype(F32), (qrow >= hd).astype(F32)]
    blk_row = lax.broadcasted_iota(jnp.int32, (n_blk, blk), 0)
    blk_row_f = blk_row.astype(F32)
    for t, q_ref in enumerate((qa_ref, qb_ref)):
        qt = q_ref[...].T
        qt_hi, qt_lo = _split_bf16(qt)
        g_hi = jnp.dot(km_hi_lo, qt_hi, preferred_element_type=F32)
        gate_both = (g_hi[0:2 * n_blk] + g_hi[2 * n_blk:]
                     + jnp.dot(km_hi, qt_lo, preferred_element_type=F32))
        tile_f = tiles[t].astype(F32)
        valid = blk_row < tiles[t]
        for e in range(2):
            slope = slopes_ref[2 * pair + e]
            gate_t = jnp.where(valid, gate_both[e * n_blk:(e + 1) * n_blk], -jnp.inf)
            taken = jnp.zeros((n_blk, blk), F32)
            for _ in range(MOBA_TOPK):
                is_max = gate_t == jnp.max(gate_t, axis=0, keepdims=True)
                first = jnp.min(jnp.where(is_max, blk_row_f, float(n_blk)), axis=0,
                                keepdims=True)
                pick = blk_row_f == first
                taken = jnp.where(pick, 1.0, taken)
                gate_t = jnp.where(pick, -jnp.inf, gate_t)
            mask_t = jnp.where(valid, jnp.where(taken > 0.0, 0.0, NEG_BIG), NEG_BIG)
            consts = jnp.where(qrow == n_blk, slope * blk,
                               jnp.where(qrow == n_blk + 1, slope,
                                         jnp.where(qrow == n_blk + 2, -slope * blk * tile_f, 0.0)))
            feat_t = jnp.concatenate([mask_t, jnp.zeros((LANES - n_blk, blk), F32)],
                                     axis=0) + consts
            q_e = (qt * (head_rows[e] * scale)).astype(BF16)
            qf_ref[t, :, e * blk:(e + 1) * blk] = jnp.concatenate(
                [q_e, feat_t.astype(BF16)], axis=0)


def _moba_prompt(q, k_rows, v_pages, k_mean, slopes, batch):
    n, aw = q.shape
    _, n_heads, hd, page = v_pages.shape
    seq = n // batch
    blk = MOBA_BLOCK
    n_blk = seq // blk
    assert seq % blk == 0 and blk % page == 0 and 2 * hd == LANES
    assert n_blk % SUBLANES == 0 and n_blk + 3 <= LANES
    n_pair = aw // LANES
    n_half = n_blk // 2
    n_items = batch * n_pair * n_half
    n_unit = n_blk + 1

    def item_map(lag, fn):
        return lambda s, _: fn(*_moba_item(s, lag, n_items, n_half, n_pair))

    return pl.pallas_call(
        functools.partial(_moba_prompt_kernel, n_blk, n_pair),
        grid_spec=pltpu.PrefetchScalarGridSpec(
            num_scalar_prefetch=1,
            grid=(n_items + 1,),
            in_specs=[
                pl.BlockSpec((blk, LANES), item_map(0, lambda b, p, j: (b * n_blk + j, p))),
                pl.BlockSpec((blk, LANES),
                             item_map(0, lambda b, p, j: (b * n_blk + n_blk - 1 - j, p))),
                pl.BlockSpec((seq, LANES), item_map(0, lambda b, p, j: (b, p))),
                pl.BlockSpec((seq // page, 2, hd, page),
                             item_map(0, lambda b, p, j: (b, p, 0, 0))),
                pl.BlockSpec((n_blk, LANES), item_map(0, lambda b, p, j: (b, p))),
            ],
            out_specs=pl.BlockSpec((seq, LANES), item_map(1, lambda b, p, j: (b, p))),
            scratch_shapes=[
                pltpu.VMEM((seq, 2 * LANES), BF16),
                pltpu.VMEM((2, n_blk, 2, hd + ONES_ROWS, blk), BF16),
                pltpu.VMEM((2, 2 * LANES, 2 * blk), BF16),
                pltpu.VMEM((n_unit, blk, 2 * blk), F32),
                pltpu.VMEM((n_unit, SUBLANES, 2 * blk), F32),
                pltpu.VMEM((2, SUBLANES, 2 * blk), F32),
                pltpu.VMEM((2, 2, hd + ONES_ROWS, blk), F32),
            ],
        ),
        out_shape=jax.ShapeDtypeStruct((n, aw), BF16),
        compiler_params=_params(("arbitrary",)),
        name="moba_prompt",
    )(slopes, q, q, k_rows, v_pages, k_mean)


def _moba_sample_kernel(n_pages, n_heads, pt_ref, q_ref, kn_ref, vn_ref, slope_ref,
                        ck_hbm, cv_hbm, o_ref, kbuf, vsel, s_ref, km_ref, bias_ref, idx_ref,
                        idx_smem, stat_ref, own_ref, orow_ref, ksem, vsem, isem):
    step = pl.program_id(0)
    n_samples = pl.num_programs(0) - 1
    blk = MOBA_BLOCK
    tq, aw = q_ref.shape
    page = kbuf.shape[3]
    ppb = blk // page
    hd = aw // n_heads
    rows = tq * n_heads
    past = n_pages * page
    n_blk = past // blk
    n_sel = min(MOBA_TOPK, n_blk)
    slot = step % 2
    slot_old = 1 - slot
    feat0 = n_blk

    def k_copy(sample, pg):
        return pltpu.make_async_copy(ck_hbm.at[pt_ref[sample, pg]], kbuf.at[sample % 2, pg],
                                     ksem.at[sample % 2])

    def for_pages(fn):
        def go(pg, c):
            fn(pg)
            return c
        lax.fori_loop(0, n_pages, go, 0)

    def idx_copy(sl):
        return pltpu.make_async_copy(idx_ref.at[sl], idx_smem.at[sl], isem)

    def v_copies(sample, sl, row_range=range(rows)):
        for r in row_range:
            for j in range(n_sel):
                first_page = idx_smem[sl, r, j] * ppb
                for g in range(ppb):
                    yield pltpu.make_async_copy(
                        cv_hbm.at[pt_ref[sample, first_page + g],
                                  pl.ds((r % n_heads) * hd, hd), :],
                        vsel.at[sl, r * n_sel + j, g], vsem.at[sl])

    lane = lax.broadcasted_iota(jnp.int32, (n_heads, aw), 1)
    hrow = lax.broadcasted_iota(jnp.int32, (n_heads, aw), 0)
    head_mask = jnp.where(lane >= hrow * hd, jnp.where(lane < (hrow + 1) * hd, 1.0, 0.0), 0.0)
    hm_rows = jnp.concatenate([head_mask] * tq, axis=0)

    @pl.when(step == 0)
    def _():
        for_pages(lambda pg: k_copy(0, pg).start())
        blk_id = lax.broadcasted_iota(jnp.int32, bias_ref.shape, 0)
        f = lax.broadcasted_iota(jnp.int32, bias_ref.shape, 1)
        off = lax.broadcasted_iota(jnp.int32, bias_ref.shape, 2)
        bias_ref[...] = jnp.where(
            f == blk_id, 1.0,
            jnp.where(f == feat0, (blk_id * blk).astype(F32),
                      jnp.where(f == feat0 + 1, off.astype(F32),
                                jnp.where((f == feat0 + 2) | (f == feat0 + 3), 1.0, 0.0)))
        ).astype(BF16)

    @pl.when(step < n_samples)
    def _():
        _moba_sample_keys(step, slot, n_pages, n_heads, n_sel, hm_rows, q_ref, kn_ref, vn_ref,
                          slope_ref, kbuf, s_ref, km_ref, bias_ref, idx_ref, stat_ref, own_ref,
                          k_copy, for_pages)
        idx_copy(slot).start()

    def values_and_gathers(do_values, do_gathers):
        if do_values:
            for copy in v_copies(step - 1, slot_old):
                copy.wait()
            inv_l = 1.0 / stat_ref[slot_old]
        hidden_rows = rows // 4 if do_values else 0
        folded = []
        for r in range(rows):
            if do_values:
                acc = jnp.zeros((hd, blk), F32)
                for j in range(n_sel):
                    pr = s_ref[slot_old, idx_smem[slot_old, r, j], r:r + 1, :]
                    v_blk = jnp.concatenate(
                        [vsel[slot_old, r * n_sel + j, g] for g in range(ppb)], axis=1)
                    acc = acc + pr * v_blk
                fold = functools.reduce(
                    jnp.add, [acc[:, c * LANES:(c + 1) * LANES] for c in range(blk // LANES)])
                folded.append(fold * inv_l[r:r + 1, 0:1])
                if r % n_heads == n_heads - 1:
                    tok = r // n_heads
                    tok_t = jnp.concatenate(folded, axis=0).T
                    folded = []
                    orow_ref[tok:tok + 1, :] = (jnp.sum(tok_t, axis=0, keepdims=True)
                                                + own_ref[slot_old, tok:tok + 1, :])
            if do_gathers and r >= hidden_rows:
                if r == hidden_rows:
                    idx_copy(slot).wait()
                first = 0 if r == hidden_rows else r
                for copy in v_copies(step, slot, range(first, r + 1)):
                    copy.start()
        if do_values:
            o_ref[...] = orow_ref[0:tq, :].astype(o_ref.dtype)

    pl.when(step == 0)(lambda: values_and_gathers(False, True))
    pl.when((step > 0) & (step < n_samples))(lambda: values_and_gathers(True, True))
    pl.when(step == n_samples)(lambda: values_and_gathers(True, False))


def _moba_sample_keys(b, slot, n_pages, n_heads, n_sel, hm_rows, q_ref, kn_ref, vn_ref,
                      slope_ref, kbuf, s_ref, km_ref, bias_ref, idx_ref, stat_ref, own_ref,
                      k_copy, for_pages):
    blk = MOBA_BLOCK
    tq, aw = q_ref.shape
    page = kbuf.shape[3]
    ppb = blk // page
    hd = aw // n_heads
    rows = tq * n_heads
    past = n_pages * page
    n_blk = past // blk
    feat0 = n_blk
    q = q_ref[...]
    q_rows = jnp.concatenate(
        [jnp.broadcast_to(q[t:t + 1, :], (n_heads, aw)) for t in range(tq)], axis=0) * hm_rows
    qs_bf = (q_rows * (float(hd) ** -0.5)).astype(BF16)
    slope = jnp.concatenate([slope_ref[:, 0:1]] * tq, axis=0)
    t_row = jnp.concatenate([jnp.full((n_heads, 1), t, jnp.int32) for t in range(tq)], axis=0)

    for_pages(lambda pg: k_copy(b, pg).wait())

    @pl.when(b + 2 < pl.num_programs(0))
    def _():
        for_pages(lambda pg: k_copy(b + 1, pg).start())

    km_ref[...] = jnp.zeros_like(km_ref)
    for n in range(n_blk):
        pages_n = [kbuf[slot, n * ppb + g] for g in range(ppb)]
        km_ref[:, n:n + 1] = jnp.sum(functools.reduce(jnp.add, pages_n), axis=1,
                                     keepdims=True) * (1.0 / blk)
        s_ref[slot, n] = jnp.dot(qs_bf, jnp.concatenate(pages_n, axis=1).astype(BF16),
                                 preferred_element_type=F32)
    km_t = km_ref[...]

    q_hi, q_lo = _split_bf16(q_rows)
    km_hi, km_lo = _split_bf16(km_t)
    gate = (jnp.dot(q_hi, km_hi, preferred_element_type=F32)
            + jnp.dot(q_hi, km_lo, preferred_element_type=F32)
            + jnp.dot(q_lo, km_hi, preferred_element_type=F32))
    blk_lane = lax.broadcasted_iota(jnp.int32, gate.shape, 1)
    gate = jnp.where(blk_lane < n_blk, gate, -jnp.inf)
    lane_f = blk_lane.astype(F32)
    taken = jnp.zeros(gate.shape, F32)
    picked = jnp.zeros(gate.shape, F32)
    for j in range(n_sel):
        is_max = gate == jnp.max(gate, axis=1, keepdims=True)
        first = jnp.min(jnp.where(is_max, lane_f, float(LANES)), axis=1, keepdims=True)
        pick = lane_f == first
        taken = jnp.where(pick, 1.0, taken)
        gate = jnp.where(pick, -jnp.inf, gate)
        picked = jnp.where(blk_lane == j, first, picked)
    idx_ref[slot] = picked.astype(jnp.int32)
    slope_l = jnp.broadcast_to(slope, gate.shape)
    t_l = jnp.broadcast_to(t_row, gate.shape).astype(F32)
    feat = jnp.where(
        blk_lane < n_blk, jnp.where(taken > 0.0, 0.0, NEG_BIG),
        jnp.where((blk_lane == feat0) | (blk_lane == feat0 + 1), slope_l,
                  jnp.where(blk_lane == feat0 + 2, -slope_l * past,
                            jnp.where(blk_lane == feat0 + 3, -slope_l * t_l, 0.0))))
    feat_bf = feat.astype(BF16)

    kn = kn_ref[...]
    vn = vn_ref[...]
    q_sc = q_rows * (float(hd) ** -0.5)
    own = []
    for t in range(tq):
        so = jnp.sum(q_sc * kn[t:t + 1, :], axis=1, keepdims=True)
        so = so - slope * (t_row - t).astype(F32)
        own.append(jnp.where(t_row >= t, so, NEG_BIG))
    m_elem = jnp.full((rows, blk), NEG_BIG, F32)
    for n in range(n_blk):
        s = s_ref[slot, n] + jnp.dot(feat_bf, bias_ref[n], preferred_element_type=F32)
        s_ref[slot, n] = s
        m_elem = jnp.maximum(m_elem, s)
    m_run = jnp.maximum(functools.reduce(jnp.maximum, own),
                        jnp.max(m_elem, axis=1, keepdims=True))

    acc = jnp.zeros((rows, aw), F32)
    l_run = jnp.zeros((rows, 1), F32)
    for t in range(tq):
        po = jnp.exp(own[t] - m_run)
        l_run = l_run + po
        acc = acc + po * vn[t:t + 1, :]
    m_l = jnp.broadcast_to(m_run, (rows, blk))
    l_elem = jnp.zeros((rows, blk), F32)
    for n in range(n_blk):
        pr = jnp.exp(s_ref[slot, n] - m_l)
        s_ref[slot, n] = pr
        l_elem = l_elem + pr
    l_run = l_run + jnp.sum(l_elem, axis=1, keepdims=True)
    stat_ref[slot] = jnp.broadcast_to(l_run, gate.shape)
    own_rows = acc * hm_rows / l_run
    own_ref[slot, 0:tq, :] = jnp.sum(own_rows.reshape(tq, n_heads, aw), axis=1)


def _moba_sample(q, k_new, v_new, cache_k, cache_v, page_table, slope_tab, n_heads):
    bd, tq, aw = q.shape
    n_pages = page_table.shape[1]
    page = cache_k.shape[2]
    past = n_pages * page
    assert MOBA_BLOCK % page == 0 and past % MOBA_BLOCK == 0 and page == LANES
    assert past // MOBA_BLOCK + 4 <= LANES
    rows = tq * n_heads
    n_blk = past // MOBA_BLOCK
    n_sel = min(MOBA_TOPK, n_blk)
    hd = aw // n_heads
    assert tq <= LANES and rows % SUBLANES == 0
    key_tok = lambda s, pt: (jnp.minimum(s, bd - 1), 0, 0)
    val_tok = lambda s, pt: (jnp.maximum(s - 1, 0), 0, 0)
    return pl.pallas_call(
        functools.partial(_moba_sample_kernel, n_pages, n_heads),
        grid_spec=pltpu.PrefetchScalarGridSpec(
            num_scalar_prefetch=1,
            grid=(bd + 1,),
            in_specs=[
                pl.BlockSpec((None, tq, aw), key_tok),
                pl.BlockSpec((None, tq, aw), key_tok),
                pl.BlockSpec((None, tq, aw), key_tok),
                pl.BlockSpec(slope_tab.shape, lambda s, pt: (0, 0)),
                pl.BlockSpec(memory_space=pl.ANY),
                pl.BlockSpec(memory_space=pl.ANY),
            ],
            out_specs=pl.BlockSpec((None, tq, aw), val_tok),
            scratch_shapes=[
                pltpu.VMEM((2, n_pages, aw, page), F32),
                pltpu.VMEM((2, rows * n_sel, MOBA_BLOCK // page, hd, page), F32),
                pltpu.VMEM((2, n_blk, rows, MOBA_BLOCK), F32),
                pltpu.VMEM((aw, LANES), F32),
                pltpu.VMEM((n_blk, LANES, MOBA_BLOCK), BF16),
                pltpu.VMEM((2, rows, LANES), jnp.int32),
                pltpu.SMEM((2, rows, LANES), jnp.int32),
                pltpu.VMEM((2, rows, LANES), F32),
                pltpu.VMEM((2, SUBLANES, aw), F32),
                pltpu.VMEM((SUBLANES, aw), F32),
                pltpu.SemaphoreType.DMA((2,)),
                pltpu.SemaphoreType.DMA((2,)),
                pltpu.SemaphoreType.DMA(()),
            ],
        ),
        out_shape=jax.ShapeDtypeStruct((bd, tq, aw), BF16),
        compiler_params=_params(("arbitrary",)),
        name="moba_sample",
    )(page_table, q, k_new, v_new, slope_tab, cache_k, cache_v)


def _row_parts(tm):
    n_part = 2 if tm % (2 * MOBA_BLOCK) == 0 else 1
    return [slice(i * (tm // n_part), (i + 1) * (tm // n_part)) for i in range(n_part)]


def _mod_rows(mod_ref, i, rs):
    return mod_ref[i] if mod_ref.shape[1] == 1 else mod_ref[i, rs, :]


def _ffn_input(x1, mod_ref, rs, g):
    return (_rms(x1, g) * (1.0 + _mod_rows(mod_ref, 4, rs))
            + _mod_rows(mod_ref, 3, rs)).astype(BF16)


def _ffn_residual(x1, h, mod_ref, rs, w1_ref, w2_ref, n_chunk):
    ck = w1_ref.shape[1] // n_chunk
    acc = jnp.zeros(x1.shape, F32)
    for j in range(n_chunk):
        hid = jnp.dot(h, w1_ref[:, j * ck:(j + 1) * ck], preferred_element_type=F32)
        hid = jnp.square(jnp.maximum(hid, 0.0)).astype(BF16)
        acc = acc + jnp.dot(hid, w2_ref[j * ck:(j + 1) * ck, :], preferred_element_type=F32)
    return x1 + _mod_rows(mod_ref, 5, rs) * acc


def _back_ab_kernel(n_chunk, x_ref, attn_ref, gated_ref, mod_ref, g_ref, wo_ref, w1_ref, w2_ref,
                    o_ref):
    aw = attn_ref.shape[1]
    parts = _row_parts(x_ref.shape[0])
    x1s = []
    for rs in parts:
        mix = (jnp.dot(attn_ref[rs, :], wo_ref[0:aw, :], preferred_element_type=F32)
               + jnp.dot(gated_ref[rs, :], wo_ref[aw:, :], preferred_element_type=F32))
        x1s.append(x_ref[rs, :] + _mod_rows(mod_ref, 2, rs) * mix)
    hs = [_ffn_input(x1, mod_ref, rs, g_ref[...]) for x1, rs in zip(x1s, parts)]
    for x1, h, rs in zip(x1s, hs, parts):
        o_ref[rs, :] = _ffn_residual(x1, h, mod_ref, rs, w1_ref, w2_ref, n_chunk)


def _both_groups(kernel_fn, n_prompt_refs, n_sample_refs, n_shared, n_out_p, n_out_s, *refs):
    i0 = n_prompt_refs
    i1 = i0 + n_sample_refs
    i2 = i1 + n_shared
    i3 = i2 + n_out_p
    i4 = i3 + n_out_s
    shared, scratch = refs[i1:i2], refs[i4:]
    last = pl.num_programs(0) - 1

    @pl.when(pl.program_id(0) < last)
    def _():
        kernel_fn(False, *refs[0:i0], *shared, *refs[i2:i3], *scratch)

    @pl.when(pl.program_id(0) == last)
    def _():
        kernel_fn(True, *refs[i0:i1], *shared, *refs[i3:i4], *scratch)


def _group_specs(mods_p, mods_s, layer, rows_per_group, n_p, n_s, tm):
    d = mods_p.shape[-1]
    nt = n_p // tm
    per_group = rows_per_group // tm
    tile = lambda r: jnp.minimum(r, nt - 1)
    prompt_rows = lambda width: pl.BlockSpec((tm, width), lambda r: (tile(r), 0))
    sample_rows = lambda width: pl.BlockSpec((n_s, width), lambda r: (0, 0))
    mod_p = pl.BlockSpec((None, None, 6, 1, d), lambda r: (layer, tile(r) // per_group, 0, 0, 0))
    mod_s = pl.BlockSpec((None, 6, n_s, d), lambda r: (layer, 0, 0, 0))
    return nt, prompt_rows, sample_rows, mod_p, mod_s


def _back_ab_group_kernel(n_chunk, is_sample, *refs):
    _back_ab_kernel(n_chunk, *refs)


def _back_ab(x, attn, gated, mods, xs, attn_s, gated_s, mods_s, g, w_out, w1, w2, layer,
             rows_per_group, tm):
    n, d = x.shape
    n_s = xs.shape[0]
    aw = attn.shape[1]
    cw = gated.shape[1]
    nt, prompt_rows, sample_rows, mod_p, mod_s = _group_specs(mods, mods_s, layer,
                                                              rows_per_group, n, n_s, tm)
    return pl.pallas_call(
        functools.partial(_both_groups, functools.partial(_back_ab_group_kernel, 4),
                          4, 4, 4, 1, 1),
        grid=(nt + 1,),
        in_specs=[
            prompt_rows(d), prompt_rows(aw), prompt_rows(cw), mod_p,
            sample_rows(d), sample_rows(aw), sample_rows(cw), mod_s,
            _resident(g.shape),
            _resident(w_out.shape),
            _resident_layer(w1.shape, layer),
            _resident_layer(w2.shape, layer),
        ],
        out_specs=[prompt_rows(d), sample_rows(d)],
        out_shape=[jax.ShapeDtypeStruct((n, d), F32), jax.ShapeDtypeStruct((n_s, d), F32)],
        compiler_params=_params(("arbitrary",)),
        name="back_ab",
    )(x, attn, gated, mods, xs, attn_s, gated_s, mods_s, g, w_out, w1, w2)


def _gmlp_kernel(n_chunk, emit_v, x_ref, mod_ref, g1_ref, g2_ref, wi_ref, gv_ref, ws_ref, bs_ref,
                 wo_ref, w1_ref, w2_ref, gf_ref, *rest):
    if emit_v:
        o_ref, cv_ref, us_ref = rest
    else:
        o_ref, us_ref = rest
    tm = x_ref.shape[0]
    cwid = gv_ref.shape[1]
    n_grp, chunk, _ = ws_ref.shape
    grp = cwid // n_grp
    x = x_ref[...]
    hb = (_rms(x, g1_ref[...]) * (1.0 + mod_ref[1]) + mod_ref[0]).astype(BF16)
    v = _rms(jnp.dot(hb, wi_ref[:, cwid:], preferred_element_type=F32), gv_ref[...])
    u = jnp.dot(hb, wi_ref[:, 0:cwid], preferred_element_type=F32)
    if emit_v:
        cv_ref[...] = v
    vb = v.astype(BF16)
    n_chunks = tm // chunk
    side = 2 if n_chunks % 2 == 0 else 1
    for c in range(0, n_chunks, side):
        rss = [slice((c + i) * chunk, (c + i + 1) * chunk) for i in range(side)]
        for gi in range(n_grp):
            ls = slice(gi * grp, (gi + 1) * grp)
            rhs = jnp.concatenate([vb[rs, ls] for rs in rss], axis=1)
            s = jnp.dot(ws_ref[gi], rhs, preferred_element_type=F32)
            for i, rs in enumerate(rss):
                s_i = s[:, i * grp:(i + 1) * grp] + bs_ref[gi]
                us_ref[rs, ls] = (u[rs, ls] * s_i).astype(BF16)
    parts = _row_parts(tm)
    x1s = []
    for rs in parts:
        mix = jnp.dot(us_ref[rs, :], wo_ref[...], preferred_element_type=F32)
        x1s.append(x[rs] + _mod_rows(mod_ref, 2, rs) * mix)
    hs = [_ffn_input(x1, mod_ref, rs, g2_ref[...]) for x1, rs in zip(x1s, parts)]
    for x1, h, rs in zip(x1s, hs, parts):
        x2 = _ffn_residual(x1, h, mod_ref, rs, w1_ref, w2_ref, n_chunk)
        o_ref[rs, :] = _rms(x2, gf_ref[...])


def _gmlp_group_kernel(n_chunk, is_sample, x_ref, mod_ref, ws_ref, bs_ref, g1_ref, g2_ref, wi_ref,
                       gv_ref, wo_ref, w1_ref, w2_ref, gf_ref, *rest):
    _gmlp_kernel(n_chunk, is_sample, x_ref, mod_ref, g1_ref, g2_ref, wi_ref, gv_ref, ws_ref, bs_ref,
                 wo_ref, w1_ref, w2_ref, gf_ref, *rest)


def _gmlp_layer(x, mods, ws_p, bs_p, xs, mods_s, ws_s, bs_s, g1, g2, w_in, g_v, w_out, w1, w2,
                g_final, layer, rows_per_group, tm):
    n, d = x.shape
    n_s = xs.shape[0]
    cwid = g_v.shape[1]
    nt, prompt_rows, sample_rows, mod_p, mod_s = _group_specs(mods, mods_s, layer,
                                                              rows_per_group, n, n_s, tm)
    return pl.pallas_call(
        functools.partial(_both_groups, functools.partial(_gmlp_group_kernel, 4),
                          4, 4, 8, 1, 2),
        grid=(nt + 1,),
        in_specs=[
            prompt_rows(d), mod_p, _resident(ws_p.shape), _resident(bs_p.shape),
            sample_rows(d), mod_s, _resident(ws_s.shape), _resident(bs_s.shape),
            _resident(g1.shape),
            _resident(g2.shape),
            _resident(w_in.shape),
            _resident(g_v.shape),
            _resident(w_out.shape),
            _resident_layer(w1.shape, layer),
            _resident_layer(w2.shape, layer),
            _resident(g_final.shape),
        ],
        out_specs=[prompt_rows(d), sample_rows(d), sample_rows(cwid)],
        out_shape=[jax.ShapeDtypeStruct((n, d), F32), jax.ShapeDtypeStruct((n_s, d), F32),
                   jax.ShapeDtypeStruct((n_s, cwid), F32)],
        scratch_shapes=[pltpu.VMEM((tm, cwid), BF16)],
        compiler_params=_params(("arbitrary",)),
        name="gmlp_layer",
    )(x, mods, ws_p, bs_p, xs, mods_s, ws_s, bs_s, g1, g2, w_in, g_v, w_out, w1, w2, g_final)


def _alibi_slopes(n_heads):
    return jnp.exp2(-8.0 * jnp.arange(1, n_heads + 1, dtype=F32) / n_heads)


def kernel(x_prompt, x_sample, cache_k, cache_v, state_conv, page_table, c_prompt, c_sample,
           norm_g, w_ada, b_ada, w_in_ab, conv_w, w_out_ab, w_in_c, g_v, w_s, b_s, w_out_c,
           w_ff1, w_ff2, g_final):
    batch, seq, d = x_prompt.shape
    bd, tq, _ = x_sample.shape
    n_ab, n_pool, page, n_heads, hd = cache_k.shape
    aw = n_heads * hd
    cw = state_conv.shape[-1]
    n_c, n_grp, chunk, _ = w_s.shape
    cwid = g_v.shape[-1]
    depth = norm_g.shape[0]
    assert depth == 2 and n_ab == 1 and n_c == 1 and hd * 2 == LANES
    assert tq <= chunk and chunk % tq == 0 and (bd * tq) % chunk == 0
    n_p, n_s = batch * seq, bd * tq
    tm = 512 if seq % 512 == 0 else chunk

    wb = lambda w: w.astype(BF16)
    w_in_ab_b, w_out_ab_b = wb(w_in_ab[0]), wb(w_out_ab[0])
    w_in_c_b, w_out_c_b = wb(w_in_c[0]), wb(w_out_c[0])
    w_ff1_b, w_ff2_b = wb(w_ff1), wb(w_ff2)

    bp = -(-(n_s + batch) // SUBLANES) * SUBLANES
    c_all = jnp.concatenate([jnp.repeat(c_sample, tq, axis=0), c_prompt,
                             jnp.zeros((bp - n_s - batch, d), F32)], axis=0)
    ada = _ada_terms(c_all, w_ada, b_ada)
    mods_p = ada[:, :, n_s:n_s + batch].transpose(0, 2, 1, 3).reshape(depth, batch, 6, 1, d)
    mods_s = ada

    slopes = _alibi_slopes(n_heads)
    slope_tab = jnp.broadcast_to(slopes[:, None], (n_heads, LANES))
    g = lambda l, j: norm_g[l, j].reshape(1, d)

    tril = jnp.tril(jnp.ones((chunk, chunk), bool))
    ws_p = jnp.where(tril, w_s[0], 0.0).astype(BF16)
    bs_p = jnp.broadcast_to(b_s[0][:, :, None], (n_grp, chunk, cwid // n_grp))
    ws_small = jnp.where(tril[:tq, :tq], w_s[0][:, :tq, :tq], 0.0)
    eye = jnp.eye(chunk // tq, dtype=F32)
    ws_s = jnp.einsum("ab,gts->gatbs", eye, ws_small).reshape(n_grp, chunk, chunk).astype(BF16)
    bs_s = jnp.broadcast_to(jnp.tile(b_s[0][:, :tq], (1, chunk // tq))[:, :, None],
                            (n_grp, chunk, cwid // n_grp))

    xp = x_prompt.reshape(n_p, d)
    prev0 = jnp.zeros((batch, CONV_W - 1, cw), F32)
    qp, kp_b, kp_t, vp_t, km_p, gated_p, conv_p = _front_prompt(
        xp, mods_p, g(0, 0), w_in_ab_b, conv_w[0], prev0, 0, tm, n_heads, page)
    attn_p = _moba_prompt(qp, kp_b, vp_t, km_p.reshape(n_p // MOBA_BLOCK, aw), slopes, batch)

    pool_pages = lambda c: c.transpose(0, 1, 3, 4, 2).reshape(n_pool, aw, page)
    xs = x_sample.reshape(n_s, d)
    st = state_conv[0]
    pad = lambda a: jnp.concatenate([a, jnp.zeros((bd, tq - a.shape[1], cw), F32)], axis=1)
    pe1 = pad(st[:, 1:2]).reshape(n_s, cw)
    pe2 = pad(st).reshape(n_s, cw)
    qs, ks, vs, gated_s, u_s = _front_sample(xs, mods_s, g(0, 0), w_in_ab_b, conv_w[0], pe1, pe2,
                                             0, tq)
    attn_s = _moba_sample(qs.reshape(bd, tq, aw), ks.reshape(bd, tq, aw), vs.reshape(bd, tq, aw),
                          pool_pages(cache_k), pool_pages(cache_v), page_table, slope_tab,
                          n_heads)

    xp, xs = _back_ab(xp, attn_p, gated_p, mods_p, xs, attn_s.reshape(n_s, aw), gated_s, mods_s,
                      g(0, 1), w_out_ab_b, w_ff1_b, w_ff2_b, 0, seq, tm)
    yp, ys, cv_s = _gmlp_layer(xp, mods_p, ws_p, bs_p, xs, mods_s, ws_s, bs_s, g(1, 0), g(1, 1),
                               w_in_c_b, g_v, w_out_c_b, w_ff1_b, w_ff2_b,
                               g_final.reshape(1, d), 1, seq, tm)

    return (
        yp.reshape(batch, seq, d),
        ys.reshape(bd, tq, d),
        kp_t.transpose(0, 3, 1, 2)[None],
        vp_t.transpose(0, 3, 1, 2)[None],
        conv_p[None],
        ks.reshape(1, bd, tq, n_heads, hd),
        vs.reshape(1, bd, tq, n_heads, hd),
        u_s.reshape(bd, tq, cw)[:, tq - (CONV_W - 1):][None],
        cv_s.reshape(1, bd, tq, cwid),
    )
```
